```python
import jax
import jax.numpy as jnp
from jax import lax
import numpy as np

D_MODEL = 1024
BATCH = 16
SEQ = 256
DEPTH = 1
DEC_BATCH = 2
DEC_SEQ = 2048
PAST_LEN = 256

GRID_W = 64
RWKV_WIDTH = 512
HEAD_DIM = 64
N_HEADS = RWKV_WIDTH // HEAD_DIM
CONV_WIDTH = 512
CONV_K = 31
D_FF = 4 * D_MODEL
DECAY_LORA = 64
ICLR_LORA = 64
GATE_LORA = 128
N_DIR = 2
IN_COLS = 3 * RWKV_WIDTH + 2 * CONV_WIDTH + 2 * D_MODEL
RMS_EPS = 1e-6
LN_EPS = 1e-5
GN_EPS = 64e-5

kernel_name = 'hybrid_rwkv7_conformer_dit_step'


def _rmsnorm(x, g):
    xf = x.astype(jnp.float32)
    y = xf * lax.rsqrt(jnp.mean(xf * xf, axis=-1, keepdims=True) + RMS_EPS) * g.astype(jnp.float32)
    return y.astype(x.dtype)


def _heads(t):
    return t.reshape(t.shape[0], t.shape[1], N_HEADS, HEAD_DIM)


def _token_shift(u, mu_prev, mu_next):
    prev = jnp.pad(u, ((0, 0), (1, 0), (0, 0)))[:, :-1]
    nxt = jnp.pad(u, ((0, 0), (0, 1), (0, 0)))[:, 1:]
    return u + (prev - u) * mu_prev + (nxt - u) * mu_next


def _wkv(s0, r, w, k, v, a, b, reverse):
    xs = tuple(jnp.moveaxis(t.astype(jnp.float32), 1, 0) for t in (r, w, k, v, a, b))

    def step(S, inp):
        rt, wt, kt, vt, at, bt = inp
        sa = jnp.einsum('bhvk,bhk->bhv', S, at)
        S = S * wt[:, :, None, :] + sa[..., None] * bt[:, :, None, :] + vt[..., None] * kt[:, :, None, :]
        return S, jnp.einsum('bhvk,bhk->bhv', S, rt)

    s_fin, ys = lax.scan(step, s0.astype(jnp.float32), xs, reverse=reverse)
    return jnp.moveaxis(ys, 0, 1), s_fin


def _rwkv_branch(xn, rkv, s0_f, s0_b, lp):
    B, T, _ = xn.shape
    rkv = _token_shift(rkv, lp['mu_prev'], lp['mu_next'])
    r, k, v = jnp.split(rkv, 3, axis=-1)
    kk = _heads(k * lp['k_k']).astype(jnp.float32)
    kk = kk * lax.rsqrt(jnp.maximum(jnp.sum(kk * kk, axis=-1, keepdims=True), 1e-24))
    g = jax.nn.sigmoid(xn @ lp['gate_g1']) @ lp['gate_g2']
    rh, vh = _heads(r), _heads(v)

    def direction(d, s0, reverse):
        w_log = -jax.nn.softplus(-(lp['decay_w0'][d] + jnp.tanh(xn @ lp['decay_w1'][d]) @ lp['decay_w2'][d])) - 0.5
        decay = jnp.exp(-jnp.exp(w_log.astype(jnp.float32)))
        a = jax.nn.sigmoid(lp['iclr_a0'][d] + (xn @ lp['iclr_a1'][d]) @ lp['iclr_a2'][d])
        kd = _heads(k * (1 + (a - 1) * lp['k_a']))
        ah = _heads(a).astype(jnp.float32)
        y, s_fin = _wkv(s0, rh, _heads(decay), kd, vh, -kk, kk * ah, reverse)
        bonus = jnp.sum((rh * kd).astype(jnp.float32) * lp['r_k'], axis=-1, keepdims=True) * vh.astype(jnp.float32)
        return y, bonus, s_fin

    y_f, bonus_f, s_f = direction(0, s0_f, False)
    y_b, bonus_b, s_b = direction(1, s0_b, True)
    y = y_f + y_b
    mu = jnp.mean(y, axis=-1, keepdims=True)
    var = jnp.mean(jnp.square(y - mu), axis=-1, keepdims=True)
    yn = ((y - mu) * lax.rsqrt(var + GN_EPS)).reshape(B, T, RWKV_WIDTH) * lp['lnx_g'] + lp['lnx_b']
    out = (yn + (bonus_f + bonus_b).reshape(B, T, RWKV_WIDTH)) * g
    return out.astype(xn.dtype) @ lp['w_out_rwkv'], s_f, s_b


def _conv_branch(glu_in, rows, lp):
    a, gt = jnp.split(glu_in, 2, axis=-1)
    u = a * jax.nn.sigmoid(gt)
    B, T, C = u.shape
    if rows is not None:
        u = u.reshape(B * rows, T // rows, C)
    u = lax.conv_general_dilated(u, lp['conv_w'][:, None, :], (1,), [(CONV_K // 2, CONV_K // 2)],
                                 dimension_numbers=('NWC', 'WIO', 'NWC'), feature_group_count=C) + lp['conv_b']
    u = u.reshape(B, T, C)
    uf = u.astype(jnp.float32)
    mu = jnp.mean(uf, axis=-1, keepdims=True)
    var = jnp.mean(jnp.square(uf - mu), axis=-1, keepdims=True)
    un = ((uf - mu) * lax.rsqrt(var + LN_EPS) * lp['conv_ln_g'] + lp['conv_ln_b']).astype(u.dtype)
    return jax.nn.silu(un) @ lp['w_out_conv']


def _layer(x, cond, s0_f, s0_b, rows, lp):
    mod = jax.nn.silu(cond) @ lp['ada_w'] + lp['ada_b']
    sh1, sc1, g1, sh2, sc2, g2 = jnp.split(mod[:, None, :], 6, axis=-1)
    xn = _rmsnorm(x, lp['norm1_g']) * (1 + sc1) + sh1
    proj = xn @ lp['w_in']
    rkv, glu_in, gates = jnp.split(proj, [3 * RWKV_WIDTH, 3 * RWKV_WIDTH + 2 * CONV_WIDTH], axis=-1)
    gate_r, gate_c = jnp.split(gates, 2, axis=-1)
    y_r, s_f, s_b = _rwkv_branch(xn, rkv, s0_f, s0_b, lp)
    y_c = _conv_branch(glu_in, rows, lp)
    merged = jax.nn.sigmoid(gate_r) * y_r + jax.nn.sigmoid(gate_c) * y_c
    x = x + g1 * (merged @ lp['w_o'])
    xn2 = _rmsnorm(x, lp['norm2_g']) * (1 + sc2) + sh2
    h = jnp.square(jax.nn.relu(xn2 @ lp['mlp_w1']))
    x = x + g2 * (h @ lp['mlp_w2'])
    return x, s_f, s_b


def setup_inputs(seed: int = 0) -> dict:
    key = jax.random.key(seed)
    ks = jax.random.split(key, 40)
    f32 = jnp.float32

    def nrm(i, shape, scale):
        return jax.random.normal(ks[i], shape, f32) * scale

    def uni(i, shape, lo, hi):
        return jax.random.uniform(ks[i], shape, f32, lo, hi)

    st_shape = (DEC_BATCH, DEPTH, N_HEADS, HEAD_DIM, HEAD_DIM)
    return {
        'x_prompt': nrm(0, (BATCH, SEQ, D_MODEL), 1.0),
        'x_sample': nrm(1, (DEC_BATCH, DEC_SEQ, D_MODEL), 1.0),
        'state_fwd': nrm(2, st_shape, 0.1),
        'state_bwd': nrm(3, st_shape, 0.1),
        'c': nrm(4, (DEC_BATCH, D_MODEL), 1.0),
        'c_ctx': nrm(5, (D_MODEL,), 1.0),
        'ada_w': nrm(6, (DEPTH, D_MODEL, 6 * D_MODEL), D_MODEL ** -0.5),
        'ada_b': nrm(7, (DEPTH, 6 * D_MODEL), 0.02),
        'norm1_g': 1.0 + nrm(8, (DEPTH, D_MODEL), 0.02),
        'norm2_g': 1.0 + nrm(9, (DEPTH, D_MODEL), 0.02),
        'w_in': nrm(10, (DEPTH, D_MODEL, IN_COLS), D_MODEL ** -0.5),
        'mu_prev': uni(11, (DEPTH, 3 * RWKV_WIDTH), 0.0, 0.5),
        'mu_next': uni(12, (DEPTH, 3 * RWKV_WIDTH), 0.0, 0.5),
        'decay_w0': uni(13, (DEPTH, N_DIR, RWKV_WIDTH), -6.0, -1.0),
        'decay_w1': nrm(14, (DEPTH, N_DIR, D_MODEL, DECAY_LORA), D_MODEL ** -0.5),
        'decay_w2': nrm(15, (DEPTH, N_DIR, DECAY_LORA, RWKV_WIDTH), 0.5 * DECAY_LORA ** -0.5),
        'iclr_a0': nrm(16, (DEPTH, N_DIR, RWKV_WIDTH), 0.1),
        'iclr_a1': nrm(17, (DEPTH, N_DIR, D_MODEL, ICLR_LORA), D_MODEL ** -0.5),
        'iclr_a2': nrm(18, (DEPTH, N_DIR, ICLR_LORA, RWKV_WIDTH), ICLR_LORA ** -0.5),
        'gate_g1': nrm(19, (DEPTH, D_MODEL, GATE_LORA), D_MODEL ** -0.5),
        'gate_g2': nrm(20, (DEPTH, GATE_LORA, RWKV_WIDTH), GATE_LORA ** -0.5),
        'k_k': 0.85 + nrm(21, (DEPTH, RWKV_WIDTH), 0.05),
        'k_a': 1.0 + nrm(22, (DEPTH, RWKV_WIDTH), 0.05),
        'r_k': nrm(23, (DEPTH, N_HEADS, HEAD_DIM), 0.1),
        'lnx_g': 1.0 + nrm(24, (DEPTH, RWKV_WIDTH), 0.02),
        'lnx_b': nrm(25, (DEPTH, RWKV_WIDTH), 0.02),
        'w_out_rwkv': nrm(26, (DEPTH, RWKV_WIDTH, D_MODEL), RWKV_WIDTH ** -0.5),
        'conv_w': nrm(27, (DEPTH, CONV_K, CONV_WIDTH), CONV_K ** -0.5),
        'conv_b': nrm(28, (DEPTH, CONV_WIDTH), 0.02),
        'conv_ln_g': 1.0 + nrm(29, (DEPTH, CONV_WIDTH), 0.02),
        'conv_ln_b': nrm(30, (DEPTH, CONV_WIDTH), 0.02),
        'w_out_conv': nrm(31, (DEPTH, CONV_WIDTH, D_MODEL), CONV_WIDTH ** -0.5),
        'w_o': nrm(32, (DEPTH, D_MODEL, D_MODEL), D_MODEL ** -0.5),
        'mlp_w1': nrm(33, (DEPTH, D_MODEL, D_FF), D_MODEL ** -0.5),
        'mlp_w2': nrm(34, (DEPTH, D_FF, D_MODEL), D_FF ** -0.5),
        'final_g': 1.0 + nrm(35, (D_MODEL,), 0.02),
    }


def reference(x_prompt, x_sample, state_fwd, state_bwd, c, c_ctx, ada_w, ada_b, norm1_g, norm2_g, w_in,
              mu_prev, mu_next, decay_w0, decay_w1, decay_w2, iclr_a0, iclr_a1, iclr_a2, gate_g1, gate_g2,
              k_k, k_a, r_k, lnx_g, lnx_b, w_out_rwkv, conv_w, conv_b, conv_ln_g, conv_ln_b, w_out_conv,
              w_o, mlp_w1, mlp_w2, final_g):
    rows = x_sample.shape[1] // GRID_W
    n_ctx_batch = x_prompt.shape[0]
    xp = x_prompt
    xs = x_sample
    new_f = []
    new_b = []
    for l in range(DEPTH):
        lp = dict(ada_w=ada_w[l], ada_b=ada_b[l], norm1_g=norm1_g[l], norm2_g=norm2_g[l], w_in=w_in[l],
                  mu_prev=mu_prev[l], mu_next=mu_next[l], decay_w0=decay_w0[l], decay_w1=decay_w1[l],
                  decay_w2=decay_w2[l], iclr_a0=iclr_a0[l], iclr_a1=iclr_a1[l], iclr_a2=iclr_a2[l],
                  gate_g1=gate_g1[l], gate_g2=gate_g2[l], k_k=k_k[l], k_a=k_a[l], r_k=r_k[l],
                  lnx_g=lnx_g[l], lnx_b=lnx_b[l], w_out_rwkv=w_out_rwkv[l], conv_w=conv_w[l], conv_b=conv_b[l],
                  conv_ln_g=conv_ln_g[l], conv_ln_b=conv_ln_b[l], w_out_conv=w_out_conv[l], w_o=w_o[l],
                  mlp_w1=mlp_w1[l], mlp_w2=mlp_w2[l])
        zeros = jnp.zeros((n_ctx_batch, N_HEADS, HEAD_DIM, HEAD_DIM), jnp.float32)
        xp, s_f, s_b = _layer(xp, c_ctx[None, :], zeros, zeros, None, lp)
        new_f.append(s_f.astype(x_prompt.dtype))
        new_b.append(s_b.astype(x_prompt.dtype))
        xs, _, _ = _layer(xs, c, state_fwd[:, l], state_bwd[:, l], rows, lp)
    y_prompt = _rmsnorm(xp, final_g)
    y_sample = _rmsnorm(xs, final_g)
    new_state_fwd = jnp.stack(new_f, axis=1)
    new_state_bwd = jnp.stack(new_b, axis=1)
    return (y_prompt, y_sample, new_state_fwd, new_state_bwd)
```

```python
import functools

import jax
import jax.numpy as jnp
from jax import lax
from jax.experimental import pallas as pl
from jax.experimental.pallas import tpu as pltpu

D_MODEL = 1024
RWKV_WIDTH = 512
HEAD_DIM = 64
N_HEADS = RWKV_WIDTH // HEAD_DIM
CONV_WIDTH = 512
CONV_K = 31
D_FF = 4 * D_MODEL
GRID_W = 64
LORA_W = 64
GATE_LORA = 128
RMS_EPS = 1e-6
LN_EPS = 1e-5
GN_EPS = 64e-5

LANES = 128
TM = 256
CHUNK = 64
N_CHUNK = TM // CHUNK
N_PAIR = RWKV_WIDTH // LANES
HALO = 8
CONV_PAD = 16
CONV_ROWS = 32
VMEM_LIMIT = 56 * 1024 * 1024

F32 = jnp.float32
BF16 = jnp.bfloat16
HIGHEST = lax.Precision.HIGHEST
DECAY_SCALE = 0.6065306597126334


def _mm(a, b):
    return jnp.dot(a.astype(BF16), b.astype(BF16), preferred_element_type=F32)


def _mm_nt(a, b):
    return lax.dot_general(a.astype(BF16), b.astype(BF16), (((1,), (1,)), ((), ())),
                           preferred_element_type=F32)


def _mm_x3(a, b):
    ah = a.astype(BF16)
    bh = b.astype(BF16)
    al = (a - ah.astype(F32)).astype(BF16)
    bl = (b - bh.astype(F32)).astype(BF16)
    dot = functools.partial(jnp.dot, preferred_element_type=F32)
    return dot(ah, bh) + dot(ah, bl) + dot(al, bh)


def _mm_f32(a, b):
    return jnp.dot(a, b, precision=HIGHEST, preferred_element_type=F32)


def _head_sum(x, bd):
    return jnp.concatenate(
        [_mm_f32(x[:, p * LANES:(p + 1) * LANES], bd) for p in range(x.shape[1] // LANES)], axis=1)


def _block_diag_ones():
    ri = lax.broadcasted_iota(jnp.int32, (LANES, LANES), 0)
    ci = lax.broadcasted_iota(jnp.int32, (LANES, LANES), 1)
    shift = HEAD_DIM.bit_length() - 1
    return ((ri >> shift) == (ci >> shift)).astype(F32)


def _rms(x):
    return x * lax.rsqrt(jnp.mean(x * x, axis=-1, keepdims=True) + RMS_EPS)


def _ada_kernel(c_ref, w_ref, b_ref, o_ref):
    c = c_ref[...]
    o_ref[...] = _mm_f32(c * jax.nn.sigmoid(c), w_ref[...]) + b_ref[...]


def _ada(cond, w, b):
    n = w.shape[1]
    tn = 1024
    return pl.pallas_call(
        _ada_kernel,
        grid=(n // tn,),
        in_specs=[pl.BlockSpec((8, D_MODEL), lambda i: (0, 0)),
                  pl.BlockSpec((D_MODEL, tn), lambda i: (0, i)),
                  pl.BlockSpec((1, tn), lambda i: (0, i))],
        out_specs=pl.BlockSpec((8, tn), lambda i: (0, i)),
        out_shape=jax.ShapeDtypeStruct((8, n), F32),
        compiler_params=pltpu.CompilerParams(dimension_semantics=("arbitrary",),
                                             vmem_limit_bytes=VMEM_LIMIT),
        name="ada",
    )(cond, w, b.reshape(1, n))


def _front_kernel(x_ref, mod_ref, g_ref, wr_ref, wg_ref, wt_ref, wl_ref, wq_ref,
                  rkv_ref, glu_ref, gates_ref, lora_ref, lg_ref):
    mod = mod_ref[0]
    sh1 = mod[:, 0:D_MODEL]
    sc1 = mod[:, D_MODEL:2 * D_MODEL]
    xn = (_rms(x_ref[...]) * g_ref[...] * (1.0 + sc1) + sh1).astype(BF16)
    rkv_ref[...] = jnp.dot(xn, wr_ref[...], preferred_element_type=F32)
    glu_ref[...] = jnp.dot(xn, wg_ref[...], preferred_element_type=F32)
    gates_ref[...] = jnp.dot(xn, wt_ref[...], preferred_element_type=F32)
    lora_ref[...] = jnp.dot(xn, wl_ref[...], preferred_element_type=F32)
    lg_ref[...] = jnp.dot(xn, wq_ref[...], preferred_element_type=F32)


def _front(x, mod3, mod_row, g, wr, wg, wt, wl, wq):
    n = x.shape[0]
    const = lambda i: (0, 0)
    tok = lambda i: (i, 0)
    widths = (wr.shape[1], wg.shape[1], wt.shape[1], wl.shape[1], wq.shape[1])
    return pl.pallas_call(
        _front_kernel,
        grid=(n // TM,),
        in_specs=[pl.BlockSpec((TM, D_MODEL), tok),
                  pl.BlockSpec((1, 1, 6 * D_MODEL), lambda i: (mod_row(i), 0, 0)),
                  pl.BlockSpec((1, D_MODEL), const)]
                 + [pl.BlockSpec((D_MODEL, w), const) for w in widths],
        out_specs=[pl.BlockSpec((TM, w), tok) for w in widths],
        out_shape=[jax.ShapeDtypeStruct((n, w), F32) for w in widths],
        compiler_params=pltpu.CompilerParams(dimension_semantics=("arbitrary",),
                                             vmem_limit_bytes=VMEM_LIMIT),
        name="front",
    )(x, mod3, g, wr, wg, wt, wl, wq)


def _wkv_kernel(rkv_ref, prev_ref, next_ref, lora_ref, s0_ref, mup_ref, mun_ref, kk_ref, ka_ref,
                rk_ref, w0_ref, w2_ref, a0_ref, a2_ref,
                y_ref, bonus_ref, sout_ref,
                s_scr, r_scr, kd_scr, v_scr, aa_scr, b_scr, lw_scr, *, nblk):
    d = pl.program_id(0)
    j = pl.program_id(2)
    sgn = 1 - 2 * d
    blk = j + d * (nblk - 1 - 2 * j)

    @pl.when(j == 0)
    def _():
        s_scr[...] = s0_ref[0, 0]

    u = rkv_ref[...]
    row = lax.broadcasted_iota(jnp.int32, (TM, 1), 0)
    prow = jnp.where(blk > 0, prev_ref[HALO - 1:HALO, :], 0.0)
    nrow = jnp.where(blk < nblk - 1, next_ref[0:1, :], 0.0)
    prev = jnp.where(row == 0, prow, pltpu.roll(u, 1, axis=0))
    nxt = jnp.where(row == TM - 1, nrow, pltpu.roll(u, TM - 1, axis=0))
    x = u + (prev - u) * mup_ref[...] + (nxt - u) * mun_ref[...]
    r = x[:, 0:RWKV_WIDTH]
    k = x[:, RWKV_WIDTH:2 * RWKV_WIDTH]
    v = x[:, 2 * RWKV_WIDTH:]

    bd = _block_diag_ones()
    kkr = k * kk_ref[...]
    kk = kkr * lax.rsqrt(jnp.maximum(_head_sum(kkr * kkr, bd), 1e-24))
    lo = lora_ref[...]
    z = w0_ref[0] + _mm(jnp.tanh(lo[:, 0:2 * LORA_W]), w2_ref[0])
    lw_scr[...] = -DECAY_SCALE * jax.nn.sigmoid(z)
    a = jax.nn.sigmoid(a0_ref[0] + _mm(lo[:, 2 * LORA_W:4 * LORA_W], a2_ref[0]))
    kd = k * (1.0 + (a - 1.0) * ka_ref[...])
    bonus_ref[0] = _head_sum(r * kd * rk_ref[...], bd) * v
    r_scr[...] = r
    kd_scr[...] = kd
    v_scr[...] = v
    aa_scr[...] = -kk
    b_scr[...] = kk * a

    ri = lax.broadcasted_iota(jnp.int32, (LANES, LANES), 0)
    ci = lax.broadcasted_iota(jnp.int32, (LANES, LANES), 1)
    shift = CHUNK.bit_length() - 1
    same = (ri >> shift) == (ci >> shift)
    rel = (ci - ri) * sgn
    strict = same & (rel < 0)
    incl = same & (rel <= 0)
    eye = (ri == ci).astype(F32)
    levels = []
    for lv in range(shift):
        levels.append(((ri >> (lv + 1)) == (ci >> (lv + 1)))
                      & (((ri >> lv) & 1) == 1 - d) & (((ci >> lv) & 1) == d))
    r64 = lax.broadcasted_iota(jnp.int32, (CHUNK, CHUNK), 0)
    c64 = lax.broadcasted_iota(jnp.int32, (CHUNK, CHUNK), 1)
    cum_m = (((c64 - r64) * sgn) <= 0).astype(F32)
    head0 = lax.broadcasted_iota(jnp.int32, (CHUNK, LANES), 1) < HEAD_DIM

    def stack(t):
        return jnp.concatenate([jnp.where(head0, t, 0.0), jnp.where(head0, 0.0, t)], axis=0)

    def chunk_body(c, carry):
        cidx = c + d * (N_CHUNK - 1 - 2 * c)
        rows = pl.ds(pl.multiple_of(cidx * CHUNK, CHUNK), CHUNK)
        lw = lw_scr[rows, :]
        cl = _mm_f32(cum_m, lw)
        tot = jnp.sum(lw, axis=0, keepdims=True)
        e_in = jnp.exp(cl)
        e_out = jnp.exp(-cl)
        e_end = jnp.exp(tot - cl)
        b_c = b_scr[rows, :]
        kd_c = kd_scr[rows, :]
        a_t = aa_scr[rows, :] * jnp.exp(cl - lw)
        r_t = r_scr[rows, :] * e_in
        b_t = b_c * e_out
        k_t = kd_c * e_out
        b_h = b_c * e_end
        k_h = kd_c * e_end
        v_c = v_scr[rows, :]
        w_tot = jnp.exp(tot)
        for p in range(N_PAIR):
            sl = slice(p * LANES, (p + 1) * LANES)
            a_s = stack(a_t[:, sl]).astype(BF16)
            r_s = stack(r_t[:, sl]).astype(BF16)
            b_s = stack(b_t[:, sl]).astype(BF16)
            k_s = stack(k_t[:, sl]).astype(BF16)
            v_s = stack(v_c[:, sl]).astype(BF16)
            bh_t = jnp.transpose(stack(b_h[:, sl])).astype(BF16)
            kh_t = jnp.transpose(stack(k_h[:, sl])).astype(BF16)
            a_ab = jnp.where(strict, _mm_nt(a_s, b_s), 0.0)
            a_ak = jnp.where(strict, _mm_nt(a_s, k_s), 0.0)
            a_rb = jnp.where(incl, _mm_nt(r_s, b_s), 0.0)
            a_rk = jnp.where(incl, _mm_nt(r_s, k_s), 0.0)
            t_inv = eye + jnp.where(levels[0], a_ab, 0.0)
            for lvl in levels[1:]:
                t_inv = t_inv + _mm(t_inv, _mm(jnp.where(lvl, a_ab, 0.0), t_inv))
            st = s_scr[p]
            p_m = _mm_x3(t_inv, _mm(a_s, st) + _mm(a_ak, v_s))
            y_s = _mm(r_s, st) + _mm(a_rb, p_m) + _mm(a_rk, v_s)
            y_ref[0, rows, sl] = y_s[0:CHUNK] + y_s[CHUNK:]
            w_col = jnp.transpose(jnp.broadcast_to(w_tot[:, sl], (LANES, LANES)))
            s_scr[p] = w_col * st + _mm(bh_t, p_m) + _mm(kh_t, v_s)
        return carry

    lax.fori_loop(0, N_CHUNK, chunk_body, 0)

    @pl.when(j == nblk - 1)
    def _():
        sout_ref[0, 0] = s_scr[...]


def _wkv(rkv, lora, s0, mup, mun, k_k, k_a, r_k, w0, w2p, a0, a2p, *, nseq, nblk):
    n = rkv.shape[0]
    hb = TM // HALO

    def tile(d, s, j):
        return s * nblk + j + d * (nblk - 1 - 2 * j)

    s0_seq = (lambda s: s) if s0.shape[1] > 1 else (lambda s: 0)
    const = lambda d, s, j: (0, 0)
    per_dir = lambda d, s, j: (d, 0, 0)
    kern = functools.partial(_wkv_kernel, nblk=nblk)
    tok = pltpu.VMEM((TM, RWKV_WIDTH), F32)
    return pl.pallas_call(
        kern,
        grid=(2, nseq, nblk),
        in_specs=[
            pl.BlockSpec((TM, 3 * RWKV_WIDTH), lambda d, s, j: (tile(d, s, j), 0)),
            pl.BlockSpec((HALO, 3 * RWKV_WIDTH),
                         lambda d, s, j: (jnp.maximum(tile(d, s, j) * hb - 1, 0), 0)),
            pl.BlockSpec((HALO, 3 * RWKV_WIDTH),
                         lambda d, s, j: (jnp.minimum((tile(d, s, j) + 1) * hb, n // HALO - 1), 0)),
            pl.BlockSpec((TM, 4 * LORA_W), lambda d, s, j: (tile(d, s, j), 0)),
            pl.BlockSpec((1, 1, N_PAIR, LANES, LANES), lambda d, s, j: (d, s0_seq(s), 0, 0, 0)),
            pl.BlockSpec((1, 3 * RWKV_WIDTH), const),
            pl.BlockSpec((1, 3 * RWKV_WIDTH), const),
            pl.BlockSpec((1, RWKV_WIDTH), const),
            pl.BlockSpec((1, RWKV_WIDTH), const),
            pl.BlockSpec((1, RWKV_WIDTH), const),
            pl.BlockSpec((1, 1, RWKV_WIDTH), per_dir),
            pl.BlockSpec((1, 2 * LORA_W, RWKV_WIDTH), per_dir),
            pl.BlockSpec((1, 1, RWKV_WIDTH), per_dir),
            pl.BlockSpec((1, 2 * LORA_W, RWKV_WIDTH), per_dir),
        ],
        out_specs=[
            pl.BlockSpec((1, TM, RWKV_WIDTH), lambda d, s, j: (d, tile(d, s, j), 0)),
            pl.BlockSpec((1, TM, RWKV_WIDTH), lambda d, s, j: (d, tile(d, s, j), 0)),
            pl.BlockSpec((1, 1, N_PAIR, LANES, LANES), lambda d, s, j: (d, s, 0, 0, 0)),
        ],
        out_shape=[jax.ShapeDtypeStruct((2, n, RWKV_WIDTH), F32),
                   jax.ShapeDtypeStruct((2, n, RWKV_WIDTH), F32),
                   jax.ShapeDtypeStruct((2, nseq, N_PAIR, LANES, LANES), F32)],
        scratch_shapes=[pltpu.VMEM((N_PAIR, LANES, LANES), F32), tok, tok, tok, tok, tok, tok],
        compiler_params=pltpu.CompilerParams(
            dimension_semantics=("arbitrary", "arbitrary", "arbitrary"),
            vmem_limit_bytes=VMEM_LIMIT),
        name="wkv",
    )(rkv, rkv, rkv, lora, s0, mup, mun, k_k, k_a, r_k, w0, w2p, a0, a2p)


def _mid_kernel(y_ref, bn_ref, lg_ref, glu_ref, gates_ref, x_ref, mod_ref, g2w_ref, lnxg_ref,
                lnxb_ref, wor_ref, cw_ref, cb_ref, clg_ref, clb_ref, woc_ref, wo_ref, n2g_ref,
                x1_ref, xn2_ref, pad_scr, *, seq):
    mod = mod_ref[0]
    g1 = mod[:, 2 * D_MODEL:3 * D_MODEL]
    sh2 = mod[:, 3 * D_MODEL:4 * D_MODEL]
    sc2 = mod[:, 4 * D_MODEL:5 * D_MODEL]

    bd = _block_diag_ones()
    y = y_ref[0] + y_ref[1]
    yc = y - _head_sum(y, bd) * (1.0 / HEAD_DIM)
    var = _head_sum(yc * yc, bd) * (1.0 / HEAD_DIM)
    yn = yc * lax.rsqrt(var + GN_EPS) * lnxg_ref[...] + lnxb_ref[...]
    g = _mm(jax.nn.sigmoid(lg_ref[...]), g2w_ref[...])
    y_r = _mm((yn + bn_ref[0] + bn_ref[1]) * g, wor_ref[...])

    glu = glu_ref[...]
    uu = glu[:, 0:CONV_WIDTH] * jax.nn.sigmoid(glu[:, CONV_WIDTH:])
    nseq = TM // seq
    zpad = jnp.zeros((nseq, CONV_PAD, CONV_WIDTH), F32)
    pad_scr[:, 0:CONV_PAD, :] = zpad
    pad_scr[:, CONV_PAD + seq:, :] = zpad
    for s in range(nseq):
        pad_scr[s, CONV_PAD:CONV_PAD + seq, :] = uu[s * seq:(s + 1) * seq]
    first = CONV_PAD - CONV_K // 2
    parts = []
    for s in range(nseq):
        for r0 in range(0, seq, CONV_ROWS):
            acc = jnp.broadcast_to(cb_ref[...], (CONV_ROWS, CONV_WIDTH))
            for t in range(CONV_K):
                lo = first + r0 + t
                acc = acc + pad_scr[s, lo:lo + CONV_ROWS, :] * cw_ref[t:t + 1, :]
            parts.append(acc)
    cv = jnp.concatenate(parts, axis=0)
    cm = cv - jnp.mean(cv, axis=-1, keepdims=True)
    cvar = jnp.mean(cm * cm, axis=-1, keepdims=True)
    un = cm * lax.rsqrt(cvar + LN_EPS) * clg_ref[...] + clb_ref[...]
    y_c = _mm(un * jax.nn.sigmoid(un), woc_ref[...])

    gates = gates_ref[...]
    merged = jax.nn.sigmoid(gates[:, 0:D_MODEL]) * y_r + jax.nn.sigmoid(gates[:, D_MODEL:]) * y_c
    x1 = x_ref[...] + g1 * _mm(merged, wo_ref[...])
    x1_ref[...] = x1
    xn2_ref[...] = (_rms(x1) * n2g_ref[...] * (1.0 + sc2) + sh2).astype(BF16)


def _mid(y2, bn2, lg, glu, gates, x, mod3, mod_row, g2w, lnxg, lnxb, wor, cw, cb, clg, clb, woc,
         wo, n2g, *, seq):
    n = x.shape[0]
    const = lambda i: (0, 0)
    tok = lambda i: (i, 0)
    both = lambda i: (0, i, 0)

    def full(a):
        return pl.BlockSpec(a.shape, const)

    return pl.pallas_call(
        functools.partial(_mid_kernel, seq=seq),
        grid=(n // TM,),
        in_specs=[pl.BlockSpec((2, TM, RWKV_WIDTH), both),
                  pl.BlockSpec((2, TM, RWKV_WIDTH), both),
                  pl.BlockSpec((TM, GATE_LORA), tok),
                  pl.BlockSpec((TM, 2 * CONV_WIDTH), tok),
                  pl.BlockSpec((TM, 2 * D_MODEL), tok),
                  pl.BlockSpec((TM, D_MODEL), tok),
                  pl.BlockSpec((1, 1, 6 * D_MODEL), lambda i: (mod_row(i), 0, 0)),
                  full(g2w), full(lnxg), full(lnxb), full(wor), full(cw), full(cb), full(clg),
                  full(clb), full(woc), full(wo), full(n2g)],
        out_specs=[pl.BlockSpec((TM, D_MODEL), tok), pl.BlockSpec((TM, D_MODEL), tok)],
        out_shape=[jax.ShapeDtypeStruct((n, D_MODEL), F32),
                   jax.ShapeDtypeStruct((n, D_MODEL), BF16)],
        scratch_shapes=[pltpu.VMEM((TM // seq, seq + 2 * CONV_PAD, CONV_WIDTH), F32)],
        compiler_params=pltpu.CompilerParams(dimension_semantics=("arbitrary",),
                                             vmem_limit_bytes=VMEM_LIMIT),
        name="mid",
    )(y2, bn2, lg, glu, gates, x, mod3, g2w, lnxg, lnxb, wor, cw, cb, clg, clb, woc, wo, n2g)


def _mlp_kernel(x1_ref, xn2_ref, mod_ref, w1_ref, w2_ref, fg_ref, o_ref):
    g2 = mod_ref[0][:, 5 * D_MODEL:6 * D_MODEL]
    h = jnp.maximum(jnp.dot(xn2_ref[...], w1_ref[...], preferred_element_type=F32), 0.0)
    x2 = x1_ref[...] + g2 * _mm(h * h, w2_ref[...])
    o_ref[...] = _rms(x2) * fg_ref[...]


def _mlp(x1, xn2, mod3, mod_row, w1, w2, fg):
    n = x1.shape[0]
    const = lambda i: (0, 0)
    tok = lambda i: (i, 0)
    return pl.pallas_call(
        _mlp_kernel,
        grid=(n // TM,),
        in_specs=[pl.BlockSpec((TM, D_MODEL), tok),
                  pl.BlockSpec((TM, D_MODEL), tok),
                  pl.BlockSpec((1, 1, 6 * D_MODEL), lambda i: (mod_row(i), 0, 0)),
                  pl.BlockSpec(w1.shape, const),
                  pl.BlockSpec(w2.shape, const),
                  pl.BlockSpec((1, D_MODEL), const)],
        out_specs=pl.BlockSpec((TM, D_MODEL), tok),
        out_shape=jax.ShapeDtypeStruct((n, D_MODEL), F32),
        compiler_params=pltpu.CompilerParams(dimension_semantics=("arbitrary",),
                                             vmem_limit_bytes=VMEM_LIMIT),
        name="mlp",
    )(x1, xn2, mod3, w1, w2, fg)


def _state_to_blocks(s):
    b = s.shape[0]
    st = jnp.swapaxes(s.astype(F32), -1, -2).reshape(b, N_PAIR, 2, HEAD_DIM, HEAD_DIM)
    out = jnp.zeros((b, N_PAIR, 2, HEAD_DIM, 2, HEAD_DIM), F32)
    out = out.at[:, :, 0, :, 0, :].set(st[:, :, 0])
    out = out.at[:, :, 1, :, 1, :].set(st[:, :, 1])
    return out.reshape(b, N_PAIR, LANES, LANES)


def _blocks_to_state(sb):
    b = sb.shape[0]
    s6 = sb.reshape(b, N_PAIR, 2, HEAD_DIM, 2, HEAD_DIM)
    st = jnp.stack([s6[:, :, 0, :, 0, :], s6[:, :, 1, :, 1, :]], axis=2)
    return jnp.swapaxes(st, -1, -2).reshape(b, N_HEADS, HEAD_DIM, HEAD_DIM)


def _pad_dir(w):
    z = jnp.zeros_like(w[0])
    return jnp.stack([jnp.concatenate([w[0], z], axis=0), jnp.concatenate([z, w[1]], axis=0)])


def kernel(x_prompt, x_sample, state_fwd, state_bwd, c, c_ctx, ada_w, ada_b, norm1_g, norm2_g, w_in,
           mu_prev, mu_next, decay_w0, decay_w1, decay_w2, iclr_a0, iclr_a1, iclr_a2, gate_g1, gate_g2,
           k_k, k_a, r_k, lnx_g, lnx_b, w_out_rwkv, conv_w, conv_b, conv_ln_g, conv_ln_b, w_out_conv,
           w_o, mlp_w1, mlp_w2, final_g):
    n_ctx, seq_ctx, _ = x_prompt.shape
    n_lat, seq_lat, _ = x_sample.shape
    depth = ada_w.shape[0]
    assert seq_ctx == TM and seq_lat % TM == 0 and GRID_W == CHUNK and c.shape[0] + 1 <= 8
    lat_blk = seq_lat // TM
    row_ctx = lambda i: 0
    row_lat = lambda i: 1 + i // lat_blk

    xp = x_prompt.reshape(n_ctx * seq_ctx, D_MODEL)
    xs = x_sample.reshape(n_lat * seq_lat, D_MODEL)
    cond = jnp.concatenate([c_ctx[None, :], c, jnp.zeros((7 - c.shape[0], D_MODEL), F32)], axis=0)
    zero_state = jnp.zeros((2, 1, N_PAIR, LANES, LANES), F32)
    row = lambda a: a.reshape(1, -1)
    new_f, new_b = [], []
    assert depth == 1
    for l in range(depth):
        mod3 = _ada(cond, ada_w[l], ada_b[l]).reshape(8, 1, 6 * D_MODEL)
        w_l = w_in[l].astype(BF16)
        c0, c1 = 3 * RWKV_WIDTH, 3 * RWKV_WIDTH + 2 * CONV_WIDTH
        wl = jnp.concatenate([decay_w1[l, 0], decay_w1[l, 1], iclr_a1[l, 0], iclr_a1[l, 1]],
                             axis=1).astype(BF16)
        front_w = (row(norm1_g[l]), w_l[:, :c0], w_l[:, c0:c1], w_l[:, c1:], wl, gate_g1[l].astype(BF16))
        wkv_w = (row(mu_prev[l]), row(mu_next[l]), row(k_k[l]), row(k_a[l]), row(r_k[l]),
                 decay_w0[l].reshape(2, 1, RWKV_WIDTH), _pad_dir(decay_w2[l]).astype(BF16),
                 iclr_a0[l].reshape(2, 1, RWKV_WIDTH), _pad_dir(iclr_a2[l]).astype(BF16))
        cw = jnp.concatenate([conv_w[l], jnp.zeros((1, CONV_WIDTH), F32)], axis=0)
        mid_w = (gate_g2[l].astype(BF16), row(lnx_g[l]), row(lnx_b[l]), w_out_rwkv[l].astype(BF16), cw,
                 row(conv_b[l]), row(conv_ln_g[l]), row(conv_ln_b[l]), w_out_conv[l].astype(BF16),
                 w_o[l].astype(BF16), row(norm2_g[l]))
        s0_lat = jnp.stack([_state_to_blocks(state_fwd[:, l]), _state_to_blocks(state_bwd[:, l])])

        def layer(x, mod_row, s0, nseq, nblk, conv_seq):
            rkv, glu, gates, lora, lg = _front(x, mod3, mod_row, *front_w)
            y2, bn2, s_out = _wkv(rkv, lora, s0, *wkv_w, nseq=nseq, nblk=nblk)
            x1, xn2 = _mid(y2, bn2, lg, glu, gates, x, mod3, mod_row, *mid_w, seq=conv_seq)
            return x1, xn2, s_out

        x1p, xn2p, s_ctx = layer(xp, row_ctx, zero_state, n_ctx, 1, seq_ctx)
        x1s, xn2s, _ = layer(xs, row_lat, s0_lat, n_lat, lat_blk, GRID_W)
        new_f.append(_blocks_to_state(s_ctx[0]).astype(x_prompt.dtype))
        new_b.append(_blocks_to_state(s_ctx[1]).astype(x_prompt.dtype))
        mlp_w = (mlp_w1[l].astype(BF16), mlp_w2[l].astype(BF16))
        fg = row(final_g)
        xp = _mlp(x1p, xn2p, mod3, row_ctx, *mlp_w, fg)
        xs = _mlp(x1s, xn2s, mod3, row_lat, *mlp_w, fg)
    y_prompt = xp.reshape(x_prompt.shape)
    y_sample = xs.reshape(x_sample.shape)
    return (y_prompt, y_sample, jnp.stack(new_f, axis=1), jnp.stack(new_b, axis=1))
```

```python
import functools

import jax
import jax.numpy as jnp
from jax import lax
from jax.experimental import pallas as pl
from jax.experimental.pallas import tpu as pltpu

D_MODEL = 1024
RWKV_WIDTH = 512
HEAD_DIM = 64
N_HEADS = RWKV_WIDTH // HEAD_DIM
CONV_WIDTH = 512
CONV_K = 31
D_FF = 4 * D_MODEL
GRID_W = 64
LORA_W = 64
GATE_LORA = 128
RMS_EPS = 1e-6
LN_EPS = 1e-5
GN_EPS = 64e-5

LANES = 128
TM = 256
CHUNK = 64
N_CHUNK = TM // CHUNK
N_PAIR = RWKV_WIDTH // LANES
HALO = 8
CONV_PAD = 16
CONV_ROWS = 32
VMEM_LIMIT = 56 * 1024 * 1024

F32 = jnp.float32
BF16 = jnp.bfloat16
HIGHEST = lax.Precision.HIGHEST
DECAY_SCALE = 0.6065306597126334


def _mm(a, b):
    return jnp.dot(a.astype(BF16), b.astype(BF16), preferred_element_type=F32)


def _mm_nt(a, b):
    return lax.dot_general(a.astype(BF16), b.astype(BF16), (((1,), (1,)), ((), ())),
                           preferred_element_type=F32)


def _mm_split3(m, x):
    dot = functools.partial(jnp.dot, preferred_element_type=F32)
    x1 = x.astype(BF16)
    r1 = x - x1.astype(F32)
    x2 = r1.astype(BF16)
    x3 = (r1 - x2.astype(F32)).astype(BF16)
    return dot(m, x3) + dot(m, x2) + dot(m, x1)


def _mm_f32(a, b):
    return jnp.dot(a, b, precision=HIGHEST, preferred_element_type=F32)


def _head_sum(x, bd):
    return jnp.concatenate(
        [_mm_f32(x[:, p * LANES:(p + 1) * LANES], bd) for p in range(x.shape[1] // LANES)], axis=1)


def _block_diag_ones():
    ri = lax.broadcasted_iota(jnp.int32, (LANES, LANES), 0)
    ci = lax.broadcasted_iota(jnp.int32, (LANES, LANES), 1)
    shift = HEAD_DIM.bit_length() - 1
    return ((ri >> shift) == (ci >> shift)).astype(F32)


def _rms(x):
    return x * lax.rsqrt(jnp.mean(x * x, axis=-1, keepdims=True) + RMS_EPS)


def _ada_kernel(c_ref, w_ref, b_ref, o_ref):
    c = c_ref[...]
    o_ref[...] = _mm_f32(c * jax.nn.sigmoid(c), w_ref[...]) + b_ref[...]


def _ada(cond, w, b):
    n = w.shape[1]
    tn = 1024
    return pl.pallas_call(
        _ada_kernel,
        grid=(n // tn,),
        in_specs=[pl.BlockSpec((8, D_MODEL), lambda i: (0, 0)),
                  pl.BlockSpec((D_MODEL, tn), lambda i: (0, i)),
                  pl.BlockSpec((1, tn), lambda i: (0, i))],
        out_specs=pl.BlockSpec((8, tn), lambda i: (0, i)),
        out_shape=jax.ShapeDtypeStruct((8, n), F32),
        compiler_params=pltpu.CompilerParams(dimension_semantics=("arbitrary",),
                                             vmem_limit_bytes=VMEM_LIMIT),
        name="ada",
    )(cond, w, b.reshape(1, n))


def _front_kernel(x_ref, mod_ref, g_ref, wr_ref, wg_ref, wt_ref, wl_ref, wq_ref,
                  rkv_ref, glu_ref, gates_ref, lora_ref, lg_ref):
    mod = mod_ref[0]
    sh1 = mod[:, 0:D_MODEL]
    sc1 = mod[:, D_MODEL:2 * D_MODEL]
    xn = (_rms(x_ref[...]) * g_ref[...] * (1.0 + sc1) + sh1).astype(BF16)
    rkv_ref[...] = jnp.dot(xn, wr_ref[...], preferred_element_type=F32)
    glu_ref[...] = jnp.dot(xn, wg_ref[...], preferred_element_type=F32)
    gates_ref[...] = jnp.dot(xn, wt_ref[...], preferred_element_type=F32)
    lora_ref[...] = jnp.dot(xn, wl_ref[...], preferred_element_type=F32)
    lg_ref[...] = jnp.dot(xn, wq_ref[...], preferred_element_type=F32)


def _front(x, mod3, mod_row, g, wr, wg, wt, wl, wq):
    n = x.shape[0]
    const = lambda i: (0, 0)
    tok = lambda i: (i, 0)
    widths = (wr.shape[1], wg.shape[1], wt.shape[1], wl.shape[1], wq.shape[1])
    return pl.pallas_call(
        _front_kernel,
        grid=(n // TM,),
        in_specs=[pl.BlockSpec((TM, D_MODEL), tok),
                  pl.BlockSpec((1, 1, 6 * D_MODEL), lambda i: (mod_row(i), 0, 0)),
                  pl.BlockSpec((1, D_MODEL), const)]
                 + [pl.BlockSpec((D_MODEL, w), const) for w in widths],
        out_specs=[pl.BlockSpec((TM, w), tok) for w in widths],
        out_shape=[jax.ShapeDtypeStruct((n, w), F32) for w in widths],
        compiler_params=pltpu.CompilerParams(dimension_semantics=("arbitrary",),
                                             vmem_limit_bytes=VMEM_LIMIT),
        name="front",
    )(x, mod3, g, wr, wg, wt, wl, wq)


def _wkv_kernel(rkv_ref, prev_ref, next_ref, lora_ref, s0_ref, mup_ref, mun_ref, kk_ref, ka_ref,
                rk_ref, w0_ref, w2_ref, a0_ref, a2_ref,
                y_ref, bonus_ref, sout_ref,
                s_scr, r_scr, kd_scr, v_scr, aa_scr, b_scr, lw_scr, cl_scr,
                rhs_scr, out_scr, upd_scr, ti_scr, vs_scr, wc_scr, *, nblk):
    d = pl.program_id(0)
    j = pl.program_id(2)
    sgn = 1 - 2 * d
    blk = j + d * (nblk - 1 - 2 * j)

    @pl.when(j == 0)
    def _():
        s_scr[...] = s0_ref[0, 0]

    u = rkv_ref[...]
    row = lax.broadcasted_iota(jnp.int32, (TM, 1), 0)
    prow = jnp.where(blk > 0, prev_ref[HALO - 1:HALO, :], 0.0)
    nrow = jnp.where(blk < nblk - 1, next_ref[0:1, :], 0.0)
    prev = jnp.where(row == 0, prow, pltpu.roll(u, 1, axis=0))
    nxt = jnp.where(row == TM - 1, nrow, pltpu.roll(u, TM - 1, axis=0))
    x = u + (prev - u) * mup_ref[...] + (nxt - u) * mun_ref[...]
    r = x[:, 0:RWKV_WIDTH]
    k = x[:, RWKV_WIDTH:2 * RWKV_WIDTH]
    v = x[:, 2 * RWKV_WIDTH:]

    bd = _block_diag_ones()
    kkr = k * kk_ref[...]
    kk = kkr * lax.rsqrt(jnp.maximum(_head_sum(kkr * kkr, bd), 1e-24))
    lo = lora_ref[...]
    z = w0_ref[0] + _mm(jnp.tanh(lo[:, 0:2 * LORA_W]), w2_ref[0])
    lw_scr[...] = -DECAY_SCALE * jax.nn.sigmoid(z)
    a = jax.nn.sigmoid(a0_ref[0] + _mm(lo[:, 2 * LORA_W:4 * LORA_W], a2_ref[0]))
    kd = k * (1.0 + (a - 1.0) * ka_ref[...])
    bonus_ref[0] = _head_sum(r * kd * rk_ref[...], bd) * v
    r_scr[...] = r
    kd_scr[...] = kd
    v_scr[...] = v
    aa_scr[...] = -kk
    b_scr[...] = kk * a

    shift = CHUNK.bit_length() - 1
    rt = lax.broadcasted_iota(jnp.int32, (TM, TM), 0)
    ct = lax.broadcasted_iota(jnp.int32, (TM, TM), 1)
    cum_m = jnp.where(((rt >> shift) == (ct >> shift)) & (((ct - rt) * sgn) <= 0), 1.0, 0.0)
    cl_scr[...] = _mm_split3(cum_m.astype(BF16), lw_scr[...])

    ri = lax.broadcasted_iota(jnp.int32, (LANES, LANES), 0)
    ci = lax.broadcasted_iota(jnp.int32, (LANES, LANES), 1)
    same = (ri >> shift) == (ci >> shift)
    rel = (ci - ri) * sgn
    strict = same & (rel < 0)
    incl = same & (rel <= 0)
    eye = (ri == ci).astype(F32)
    levels = []
    for lv in range(shift):
        levels.append(((ri >> (lv + 1)) == (ci >> (lv + 1)))
                      & (((ri >> lv) & 1) == 1 - d) & (((ci >> lv) & 1) == d))
    head0 = lax.broadcasted_iota(jnp.int32, (CHUNK, LANES), 1) < HEAD_DIM
    dot = functools.partial(jnp.dot, preferred_element_type=F32)

    def stack(t):
        return jnp.concatenate([jnp.where(head0, t, 0.0), jnp.where(head0, 0.0, t)], axis=0)

    def prepare(chunks):
        ops = []
        for c in chunks:
            rows = slice(c * CHUNK, (c + 1) * CHUNK)
            lw = lw_scr[rows, :]
            cl = cl_scr[rows, :]
            tot = jnp.sum(lw, axis=0, keepdims=True)
            e_out = jnp.exp(-cl)
            e_end = jnp.exp(tot - cl)
            b_c = b_scr[rows, :]
            kd_c = kd_scr[rows, :]
            a_t = aa_scr[rows, :] * jnp.exp(cl - lw)
            r_t = r_scr[rows, :] * jnp.exp(cl)
            b_t = b_c * e_out
            k_t = kd_c * e_out
            b_h = b_c * e_end
            k_h = kd_c * e_end
            v_c = v_scr[rows, :]
            w_tot = jnp.exp(tot)
            for p in range(N_PAIR):
                i = c * N_PAIR + p
                sl = slice(p * LANES, (p + 1) * LANES)
                a_s = stack(a_t[:, sl]).astype(BF16)
                r_s = stack(r_t[:, sl]).astype(BF16)
                bk_s = jnp.concatenate([stack(b_t[:, sl]), stack(k_t[:, sl])], axis=0).astype(BF16)
                vs_scr[i] = stack(v_c[:, sl]).astype(BF16)
                upd_scr[i] = jnp.concatenate([jnp.transpose(stack(b_h[:, sl])),
                                              jnp.transpose(stack(k_h[:, sl]))], axis=1).astype(BF16)
                wc_scr[i] = jnp.transpose(jnp.broadcast_to(w_tot[:, sl], (LANES, LANES)))
                ops.append((i, a_s, r_s, bk_s))
        a_abs, t_invs = [], []
        for i, a_s, r_s, bk_s in ops:
            ab = _mm_nt(jnp.concatenate([a_s, r_s], axis=0), bk_s)
            a_ab = jnp.where(strict, ab[0:LANES, 0:LANES], 0.0)
            a_ak = jnp.where(strict, ab[0:LANES, LANES:], 0.0)
            a_rb = jnp.where(incl, ab[LANES:, 0:LANES], 0.0)
            a_rk = jnp.where(incl, ab[LANES:, LANES:], 0.0)
            rhs_scr[i] = jnp.concatenate([a_s, a_ak.astype(BF16)], axis=1)
            out_scr[i] = jnp.concatenate([r_s, a_rb.astype(BF16), a_rk.astype(BF16)], axis=1)
            a_abs.append(a_ab)
            t_invs.append(eye + jnp.where(levels[0], a_ab, 0.0))
        for lvl in levels[1:]:
            ms = [_mm(jnp.where(lvl, a_ab, 0.0), t) for a_ab, t in zip(a_abs, t_invs)]
            t_invs = [t + _mm(t, m) for t, m in zip(t_invs, ms)]
        for (i, _, _, _), t in zip(ops, t_invs):
            th = t.astype(BF16)
            ti_scr[i, 0] = th
            ti_scr[i, 1] = (t - th.astype(F32)).astype(BF16)

    def scan_chunk(step):
        cidx = step + d * (N_CHUNK - 1 - 2 * step)
        rows = pl.ds(pl.multiple_of(cidx * CHUNK, CHUNK), CHUNK)
        base = cidx * N_PAIR
        pairs = range(N_PAIR)
        sts = [s_scr[p] for p in pairs]
        stb = [st.astype(BF16) for st in sts]
        vss = [vs_scr[base + p] for p in pairs]
        rhs = [dot(rhs_scr[base + p], jnp.concatenate([stb[p], vss[p]], axis=0)) for p in pairs]
        pmb = []
        for p in pairs:
            rh = rhs[p].astype(BF16)
            rl = (rhs[p] - rh.astype(F32)).astype(BF16)
            th = ti_scr[base + p, 0]
            pm = dot(th, rh) + dot(th, rl) + dot(ti_scr[base + p, 1], rh)
            pmb.append(pm.astype(BF16))
        for p in pairs:
            s_scr[p] = wc_scr[base + p] * sts[p] + dot(
                upd_scr[base + p], jnp.concatenate([pmb[p], vss[p]], axis=0))
        for p in pairs:
            y_s = dot(out_scr[base + p], jnp.concatenate([stb[p], pmb[p], vss[p]], axis=0))
            y_ref[0, rows, p * LANES:(p + 1) * LANES] = y_s[0:CHUNK] + y_s[CHUNK:]

    half = N_CHUNK // 2
    prepare(range(0, half))
    prepare(range(half, N_CHUNK))
    for step in range(N_CHUNK):
        scan_chunk(step)

    @pl.when(j == nblk - 1)
    def _():
        sout_ref[0, 0] = s_scr[...]


def _wkv(rkv, lora, s0, mup, mun, k_k, k_a, r_k, w0, w2p, a0, a2p, *, nseq, nblk):
    n = rkv.shape[0]
    hb = TM // HALO

    def tile(d, s, j):
        return s * nblk + j + d * (nblk - 1 - 2 * j)

    s0_seq = (lambda s: s) if s0.shape[1] > 1 else (lambda s: 0)
    const = lambda d, s, j: (0, 0)
    per_dir = lambda d, s, j: (d, 0, 0)
    kern = functools.partial(_wkv_kernel, nblk=nblk)
    tok = pltpu.VMEM((TM, RWKV_WIDTH), F32)
    n_prob = N_CHUNK * N_PAIR
    return pl.pallas_call(
        kern,
        grid=(2, nseq, nblk),
        in_specs=[
            pl.BlockSpec((TM, 3 * RWKV_WIDTH), lambda d, s, j: (tile(d, s, j), 0)),
            pl.BlockSpec((HALO, 3 * RWKV_WIDTH),
                         lambda d, s, j: (jnp.maximum(tile(d, s, j) * hb - 1, 0), 0)),
            pl.BlockSpec((HALO, 3 * RWKV_WIDTH),
                         lambda d, s, j: (jnp.minimum((tile(d, s, j) + 1) * hb, n // HALO - 1), 0)),
            pl.BlockSpec((TM, 4 * LORA_W), lambda d, s, j: (tile(d, s, j), 0)),
            pl.BlockSpec((1, 1, N_PAIR, LANES, LANES), lambda d, s, j: (d, s0_seq(s), 0, 0, 0)),
            pl.BlockSpec((1, 3 * RWKV_WIDTH), const),
            pl.BlockSpec((1, 3 * RWKV_WIDTH), const),
            pl.BlockSpec((1, RWKV_WIDTH), const),
            pl.BlockSpec((1, RWKV_WIDTH), const),
            pl.BlockSpec((1, RWKV_WIDTH), const),
            pl.BlockSpec((1, 1, RWKV_WIDTH), per_dir),
            pl.BlockSpec((1, 2 * LORA_W, RWKV_WIDTH), per_dir),
            pl.BlockSpec((1, 1, RWKV_WIDTH), per_dir),
            pl.BlockSpec((1, 2 * LORA_W, RWKV_WIDTH), per_dir),
        ],
        out_specs=[
            pl.BlockSpec((1, TM, RWKV_WIDTH), lambda d, s, j: (d, tile(d, s, j), 0)),
            pl.BlockSpec((1, TM, RWKV_WIDTH), lambda d, s, j: (d, tile(d, s, j), 0)),
            pl.BlockSpec((1, 1, N_PAIR, LANES, LANES), lambda d, s, j: (d, s, 0, 0, 0)),
        ],
        out_shape=[jax.ShapeDtypeStruct((2, n, RWKV_WIDTH), F32),
                   jax.ShapeDtypeStruct((2, n, RWKV_WIDTH), F32),
                   jax.ShapeDtypeStruct((2, nseq, N_PAIR, LANES, LANES), F32)],
        scratch_shapes=[pltpu.VMEM((N_PAIR, LANES, LANES), F32), tok, tok, tok, tok, tok, tok, tok,
                        pltpu.VMEM((n_prob, LANES, 2 * LANES), BF16),
                        pltpu.VMEM((n_prob, LANES, 3 * LANES), BF16),
                        pltpu.VMEM((n_prob, LANES, 2 * LANES), BF16),
                        pltpu.VMEM((n_prob, 2, LANES, LANES), BF16),
                        pltpu.VMEM((n_prob, LANES, LANES), BF16),
                        pltpu.VMEM((n_prob, LANES, LANES), F32)],
        compiler_params=pltpu.CompilerParams(
            dimension_semantics=("arbitrary", "arbitrary", "arbitrary"),
            vmem_limit_bytes=VMEM_LIMIT),
        name="wkv",
    )(rkv, rkv, rkv, lora, s0, mup, mun, k_k, k_a, r_k, w0, w2p, a0, a2p)


def _mid_kernel(y_ref, bn_ref, lg_ref, glu_ref, gates_ref, x_ref, mod_ref, g2w_ref, lnxg_ref,
                lnxb_ref, wor_ref, cw_ref, cb_ref, clg_ref, clb_ref, woc_ref, wo_ref, n2g_ref,
                x1_ref, xn2_ref, pad_scr, *, seq):
    mod = mod_ref[0]
    g1 = mod[:, 2 * D_MODEL:3 * D_MODEL]
    sh2 = mod[:, 3 * D_MODEL:4 * D_MODEL]
    sc2 = mod[:, 4 * D_MODEL:5 * D_MODEL]

    bd = _block_diag_ones()
    y = y_ref[0] + y_ref[1]
    yc = y - _head_sum(y, bd) * (1.0 / HEAD_DIM)
    var = _head_sum(yc * yc, bd) * (1.0 / HEAD_DIM)
    yn = yc * lax.rsqrt(var + GN_EPS) * lnxg_ref[...] + lnxb_ref[...]
    g = _mm(jax.nn.sigmoid(lg_ref[...]), g2w_ref[...])
    y_r = _mm((yn + bn_ref[0] + bn_ref[1]) * g, wor_ref[...])

    glu = glu_ref[...]
    uu = glu[:, 0:CONV_WIDTH] * jax.nn.sigmoid(glu[:, CONV_WIDTH:])
    nseq = TM // seq
    zpad = jnp.zeros((nseq, CONV_PAD, CONV_WIDTH), F32)
    pad_scr[:, 0:CONV_PAD, :] = zpad
    pad_scr[:, CONV_PAD + seq:, :] = zpad
    for s in range(nseq):
        pad_scr[s, CONV_PAD:CONV_PAD + seq, :] = uu[s * seq:(s + 1) * seq]
    first = CONV_PAD - CONV_K // 2
    parts = []
    for s in range(nseq):
        for r0 in range(0, seq, CONV_ROWS):
            acc = jnp.broadcast_to(cb_ref[...], (CONV_ROWS, CONV_WIDTH))
            for t in range(CONV_K):
                lo = first + r0 + t
                acc = acc + pad_scr[s, lo:lo + CONV_ROWS, :] * cw_ref[t:t + 1, :]
            parts.append(acc)
    cv = jnp.concatenate(parts, axis=0)
    cm = cv - jnp.mean(cv, axis=-1, keepdims=True)
    cvar = jnp.mean(cm * cm, axis=-1, keepdims=True)
    un = cm * lax.rsqrt(cvar + LN_EPS) * clg_ref[...] + clb_ref[...]
    y_c = _mm(un * jax.nn.sigmoid(un), woc_ref[...])

    gates = gates_ref[...]
    merged = jax.nn.sigmoid(gates[:, 0:D_MODEL]) * y_r + jax.nn.sigmoid(gates[:, D_MODEL:]) * y_c
    x1 = x_ref[...] + g1 * _mm(merged, wo_ref[...])
    x1_ref[...] = x1
    xn2_ref[...] = (_rms(x1) * n2g_ref[...] * (1.0 + sc2) + sh2).astype(BF16)


def _mid(y2, bn2, lg, glu, gates, x, mod3, mod_row, g2w, lnxg, lnxb, wor, cw, cb, clg, clb, woc,
         wo, n2g, *, seq):
    n = x.shape[0]
    const = lambda i: (0, 0)
    tok = lambda i: (i, 0)
    both = lambda i: (0, i, 0)

    def full(a):
        return pl.BlockSpec(a.shape, const)

    return pl.pallas_call(
        functools.partial(_mid_kernel, seq=seq),
        grid=(n // TM,),
        in_specs=[pl.BlockSpec((2, TM, RWKV_WIDTH), both),
                  pl.BlockSpec((2, TM, RWKV_WIDTH), both),
                  pl.BlockSpec((TM, GATE_LORA), tok),
                  pl.BlockSpec((TM, 2 * CONV_WIDTH), tok),
                  pl.BlockSpec((TM, 2 * D_MODEL), tok),
                  pl.BlockSpec((TM, D_MODEL), tok),
                  pl.BlockSpec((1, 1, 6 * D_MODEL), lambda i: (mod_row(i), 0, 0)),
                  full(g2w), full(lnxg), full(lnxb), full(wor), full(cw), full(cb), full(clg),
                  full(clb), full(woc), full(wo), full(n2g)],
        out_specs=[pl.BlockSpec((TM, D_MODEL), tok), pl.BlockSpec((TM, D_MODEL), tok)],
        out_shape=[jax.ShapeDtypeStruct((n, D_MODEL), F32),
                   jax.ShapeDtypeStruct((n, D_MODEL), BF16)],
        scratch_shapes=[pltpu.VMEM((TM // seq, seq + 2 * CONV_PAD, CONV_WIDTH), F32)],
        compiler_params=pltpu.CompilerParams(dimension_semantics=("arbitrary",),
                                             vmem_limit_bytes=VMEM_LIMIT),
        name="mid",
    )(y2, bn2, lg, glu, gates, x, mod3, g2w, lnxg, lnxb, wor, cw, cb, clg, clb, woc, wo, n2g)


def _mlp_kernel(x1_ref, xn2_ref, mod_ref, w1_ref, w2_ref, fg_ref, o_ref):
    g2 = mod_ref[0][:, 5 * D_MODEL:6 * D_MODEL]
    h = jnp.maximum(jnp.dot(xn2_ref[...], w1_ref[...], preferred_element_type=F32), 0.0)
    x2 = x1_ref[...] + g2 * _mm(h * h, w2_ref[...])
    o_ref[...] = _rms(x2) * fg_ref[...]


def _mlp(x1, xn2, mod3, mod_row, w1, w2, fg):
    n = x1.shape[0]
    const = lambda i: (0, 0)
    tok = lambda i: (i, 0)
    return pl.pallas_call(
        _mlp_kernel,
        grid=(n // TM,),
        in_specs=[pl.BlockSpec((TM, D_MODEL), tok),
                  pl.BlockSpec((TM, D_MODEL), tok),
                  pl.BlockSpec((1, 1, 6 * D_MODEL), lambda i: (mod_row(i), 0, 0)),
                  pl.BlockSpec(w1.shape, const),
                  pl.BlockSpec(w2.shape, const),
                  pl.BlockSpec((1, D_MODEL), const)],
        out_specs=pl.BlockSpec((TM, D_MODEL), tok),
        out_shape=jax.ShapeDtypeStruct((n, D_MODEL), F32),
        compiler_params=pltpu.CompilerParams(dimension_semantics=("arbitrary",),
                                             vmem_limit_bytes=VMEM_LIMIT),
        name="mlp",
    )(x1, xn2, mod3, w1, w2, fg)


def _state_to_blocks(s):
    b = s.shape[0]
    st = jnp.swapaxes(s.astype(F32), -1, -2).reshape(b, N_PAIR, 2, HEAD_DIM, HEAD_DIM)
    out = jnp.zeros((b, N_PAIR, 2, HEAD_DIM, 2, HEAD_DIM), F32)
    out = out.at[:, :, 0, :, 0, :].set(st[:, :, 0])
    out = out.at[:, :, 1, :, 1, :].set(st[:, :, 1])
    return out.reshape(b, N_PAIR, LANES, LANES)


def _blocks_to_state(sb):
    b = sb.shape[0]
    s6 = sb.reshape(b, N_PAIR, 2, HEAD_DIM, 2, HEAD_DIM)
    st = jnp.stack([s6[:, :, 0, :, 0, :], s6[:, :, 1, :, 1, :]], axis=2)
    return jnp.swapaxes(st, -1, -2).reshape(b, N_HEADS, HEAD_DIM, HEAD_DIM)


def _pad_dir(w):
    z = jnp.zeros_like(w[0])
    return jnp.stack([jnp.concatenate([w[0], z], axis=0), jnp.concatenate([z, w[1]], axis=0)])


def kernel(x_prompt, x_sample, state_fwd, state_bwd, c, c_ctx, ada_w, ada_b, norm1_g, norm2_g, w_in,
           mu_prev, mu_next, decay_w0, decay_w1, decay_w2, iclr_a0, iclr_a1, iclr_a2, gate_g1, gate_g2,
           k_k, k_a, r_k, lnx_g, lnx_b, w_out_rwkv, conv_w, conv_b, conv_ln_g, conv_ln_b, w_out_conv,
           w_o, mlp_w1, mlp_w2, final_g):
    n_ctx, seq_ctx, _ = x_prompt.shape
    n_lat, seq_lat, _ = x_sample.shape
    depth = ada_w.shape[0]
    assert seq_ctx == TM and seq_lat % TM == 0 and GRID_W == CHUNK and c.shape[0] + 1 <= 8
    lat_blk = seq_lat // TM
    row_ctx = lambda i: 0
    row_lat = lambda i: 1 + i // lat_blk

    xp = x_prompt.reshape(n_ctx * seq_ctx, D_MODEL)
    xs = x_sample.reshape(n_lat * seq_lat, D_MODEL)
    cond = jnp.concatenate([c_ctx[None, :], c, jnp.zeros((7 - c.shape[0], D_MODEL), F32)], axis=0)
    zero_state = jnp.zeros((2, 1, N_PAIR, LANES, LANES), F32)
    row = lambda a: a.reshape(1, -1)
    new_f, new_b = [], []
    assert depth == 1
    for l in range(depth):
        mod3 = _ada(cond, ada_w[l], ada_b[l]).reshape(8, 1, 6 * D_MODEL)
        w_l = w_in[l].astype(BF16)
        c0, c1 = 3 * RWKV_WIDTH, 3 * RWKV_WIDTH + 2 * CONV_WIDTH
        wl = jnp.concatenate([decay_w1[l, 0], decay_w1[l, 1], iclr_a1[l, 0], iclr_a1[l, 1]],
                             axis=1).astype(BF16)
        front_w = (row(norm1_g[l]), w_l[:, :c0], w_l[:, c0:c1], w_l[:, c1:], wl, gate_g1[l].astype(BF16))
        wkv_w = (row(mu_prev[l]), row(mu_next[l]), row(k_k[l]), row(k_a[l]), row(r_k[l]),
                 decay_w0[l].reshape(2, 1, RWKV_WIDTH), _pad_dir(decay_w2[l]).astype(BF16),
                 iclr_a0[l].reshape(2, 1, RWKV_WIDTH), _pad_dir(iclr_a2[l]).astype(BF16))
        cw = jnp.concatenate([conv_w[l], jnp.zeros((1, CONV_WIDTH), F32)], axis=0)
        mid_w = (gate_g2[l].astype(BF16), row(lnx_g[l]), row(lnx_b[l]), w_out_rwkv[l].astype(BF16), cw,
                 row(conv_b[l]), row(conv_ln_g[l]), row(conv_ln_b[l]), w_out_conv[l].astype(BF16),
                 w_o[l].astype(BF16), row(norm2_g[l]))
        s0_lat = jnp.stack([_state_to_blocks(state_fwd[:, l]), _state_to_blocks(state_bwd[:, l])])

        def layer(x, mod_row, s0, nseq, nblk, conv_seq):
            rkv, glu, gates, lora, lg = _front(x, mod3, mod_row, *front_w)
            y2, bn2, s_out = _wkv(rkv, lora, s0, *wkv_w, nseq=nseq, nblk=nblk)
            x1, xn2 = _mid(y2, bn2, lg, glu, gates, x, mod3, mod_row, *mid_w, seq=conv_seq)
            return x1, xn2, s_out

        x1p, xn2p, s_ctx = layer(xp, row_ctx, zero_state, n_ctx, 1, seq_ctx)
        x1s, xn2s, _ = layer(xs, row_lat, s0_lat, n_lat, lat_blk, GRID_W)
        new_f.append(_blocks_to_state(s_ctx[0]).astype(x_prompt.dtype))
        new_b.append(_blocks_to_state(s_ctx[1]).astype(x_prompt.dtype))
        mlp_w = (mlp_w1[l].astype(BF16), mlp_w2[l].astype(BF16))
        fg = row(final_g)
        xp = _mlp(x1p, xn2p, mod3, row_ctx, *mlp_w, fg)
        xs = _mlp(x1s, xn2s, mod3, row_lat, *mlp_w, fg)
    y_prompt = xp.reshape(x_prompt.shape)
    y_sample = xs.reshape(x_sample.shape)
    return (y_prompt, y_sample, jnp.stack(new_f, axis=1), jnp.stack(new_b, axis=1))
```

```python
import functools

import jax
import jax.numpy as jnp
from jax import lax
from jax.experimental import pallas as pl
from jax.experimental.pallas import tpu as pltpu

D_MODEL = 1024
RWKV_WIDTH = 512
HEAD_DIM = 64
N_HEADS = RWKV_WIDTH // HEAD_DIM
CONV_WIDTH = 512
CONV_K = 31
D_FF = 4 * D_MODEL
GRID_W = 64
LORA_W = 64
GATE_LORA = 128
RMS_EPS = 1e-6
LN_EPS = 1e-5
GN_EPS = 64e-5

LANES = 128
SUBLANES = 8
TM = 256
CHUNK = 64
N_CHUNK = TM // CHUNK
N_PAIR = RWKV_WIDTH // LANES
HALO = SUBLANES
CONV_PAD = 16
CONV_ROWS = 32
VMEM_LIMIT = 56 * 1024 * 1024

F32 = jnp.float32
BF16 = jnp.bfloat16
HIGHEST = lax.Precision.HIGHEST
DECAY_SCALE = 0.6065306597126334


def _mm(a, b):
    return jnp.dot(a.astype(BF16), b.astype(BF16), preferred_element_type=F32)


def _mm_nt(a, b):
    return lax.dot_general(a.astype(BF16), b.astype(BF16), (((1,), (1,)), ((), ())),
                           preferred_element_type=F32)


def _mm_split3(m, x):
    dot = functools.partial(jnp.dot, preferred_element_type=F32)
    x1 = x.astype(BF16)
    r1 = x - x1.astype(F32)
    x2 = r1.astype(BF16)
    x3 = (r1 - x2.astype(F32)).astype(BF16)
    return dot(m, x3) + dot(m, x2) + dot(m, x1)


def _mm_f32(a, b):
    return jnp.dot(a, b, precision=HIGHEST, preferred_element_type=F32)


def _head_sum(x, bd):
    return jnp.concatenate(
        [_mm_f32(x[:, p * LANES:(p + 1) * LANES], bd) for p in range(x.shape[1] // LANES)], axis=1)


def _block_diag_ones():
    ri = lax.broadcasted_iota(jnp.int32, (LANES, LANES), 0)
    ci = lax.broadcasted_iota(jnp.int32, (LANES, LANES), 1)
    shift = HEAD_DIM.bit_length() - 1
    return ((ri >> shift) == (ci >> shift)).astype(F32)


def _rms(x):
    return x * lax.rsqrt(jnp.mean(x * x, axis=-1, keepdims=True) + RMS_EPS)


def _ada_kernel(c_ref, w_ref, b_ref, o_ref):
    c = c_ref[...]
    o_ref[...] = _mm_f32(c * jax.nn.sigmoid(c), w_ref[...]) + b_ref[...]


def _ada(cond, w, b):
    n = w.shape[1]
    tn = 1024
    return pl.pallas_call(
        _ada_kernel,
        grid=(n // tn,),
        in_specs=[pl.BlockSpec((8, D_MODEL), lambda i: (0, 0)),
                  pl.BlockSpec((D_MODEL, tn), lambda i: (0, i)),
                  pl.BlockSpec((1, tn), lambda i: (0, i))],
        out_specs=pl.BlockSpec((8, tn), lambda i: (0, i)),
        out_shape=jax.ShapeDtypeStruct((8, n), F32),
        compiler_params=pltpu.CompilerParams(dimension_semantics=("arbitrary",),
                                             vmem_limit_bytes=VMEM_LIMIT),
        name="ada",
    )(cond, w, b.reshape(1, n))


_IN_SPLIT = (3 * RWKV_WIDTH, 2 * CONV_WIDTH, 2 * D_MODEL)
_LORA_SPLIT = (4 * LORA_W, GATE_LORA)


def _front_kernel(x_ref, mod_ref, g_ref, win_ref, wlo_ref, *out_refs):
    mod = mod_ref[0]
    sh1 = mod[:, 0:D_MODEL]
    sc1 = mod[:, D_MODEL:2 * D_MODEL]
    xn = (_rms(x_ref[...]) * g_ref[...] * (1.0 + sc1) + sh1).astype(BF16)
    outs = iter(out_refs)
    for w_ref, split in ((win_ref, _IN_SPLIT), (wlo_ref, _LORA_SPLIT)):
        lo = 0
        for width in split:
            next(outs)[...] = jnp.dot(xn, w_ref[:, lo:lo + width], preferred_element_type=F32)
            lo += width


def _front(x, mod3, mod_row, g, w_in, w_lora):
    n = x.shape[0]
    const = lambda i: (0, 0)
    tok = lambda i: (i, 0)
    widths = _IN_SPLIT + _LORA_SPLIT
    return pl.pallas_call(
        _front_kernel,
        grid=(n // TM,),
        in_specs=[pl.BlockSpec((TM, D_MODEL), tok),
                  pl.BlockSpec((1, 1, 6 * D_MODEL), lambda i: (mod_row(i), 0, 0)),
                  pl.BlockSpec((1, D_MODEL), const),
                  pl.BlockSpec(w_in.shape, const),
                  pl.BlockSpec(w_lora.shape, const)],
        out_specs=[pl.BlockSpec((TM, w), tok) for w in widths],
        out_shape=[jax.ShapeDtypeStruct((n, w), F32) for w in widths],
        compiler_params=pltpu.CompilerParams(dimension_semantics=("arbitrary",),
                                             vmem_limit_bytes=VMEM_LIMIT),
        name="front",
    )(x, mod3, g, w_in, w_lora)


def _wkv_kernel(rkv_ref, prev_ref, next_ref, lora_ref, s0_ref, mup_ref, mun_ref, kk_ref, ka_ref,
                rk_ref, w0_ref, w2_ref, a0_ref, a2_ref,
                y_ref, bonus_ref, sout_ref,
                s_scr, r_scr, kd_scr, v_scr, aa_scr, b_scr, lw_scr, cl_scr,
                rhs_scr, out_scr, upd_scr, ti_scr, vs_scr, wc_scr, *, nblk):
    d = pl.program_id(0)
    j = pl.program_id(2)
    sgn = 1 - 2 * d
    blk = j + d * (nblk - 1 - 2 * j)

    @pl.when(j == 0)
    def _():
        s_scr[...] = s0_ref[0, 0]

    u = rkv_ref[...]
    row = lax.broadcasted_iota(jnp.int32, (TM, 1), 0)
    prow = jnp.where(blk > 0, prev_ref[HALO - 1:HALO, :], 0.0)
    nrow = jnp.where(blk < nblk - 1, next_ref[0:1, :], 0.0)
    prev = jnp.where(row == 0, prow, pltpu.roll(u, 1, axis=0))
    nxt = jnp.where(row == TM - 1, nrow, pltpu.roll(u, TM - 1, axis=0))
    x = u + (prev - u) * mup_ref[...] + (nxt - u) * mun_ref[...]
    r = x[:, 0:RWKV_WIDTH]
    k = x[:, RWKV_WIDTH:2 * RWKV_WIDTH]
    v = x[:, 2 * RWKV_WIDTH:]

    bd = _block_diag_ones()
    kkr = k * kk_ref[...]
    kk = kkr * lax.rsqrt(jnp.maximum(_head_sum(kkr * kkr, bd), 1e-24))
    lo = lora_ref[...]
    z = w0_ref[0] + _mm(jnp.tanh(lo[:, 0:2 * LORA_W]), w2_ref[0])
    lw_scr[...] = -DECAY_SCALE * jax.nn.sigmoid(z)
    a = jax.nn.sigmoid(a0_ref[0] + _mm(lo[:, 2 * LORA_W:4 * LORA_W], a2_ref[0]))
    kd = k * (1.0 + (a - 1.0) * ka_ref[...])
    bonus_ref[0] = _head_sum(r * kd * rk_ref[...], bd) * v
    r_scr[...] = r
    kd_scr[...] = kd
    v_scr[...] = v
    aa_scr[...] = -kk
    b_scr[...] = kk * a

    shift = CHUNK.bit_length() - 1
    rt = lax.broadcasted_iota(jnp.int32, (TM, TM), 0)
    ct = lax.broadcasted_iota(jnp.int32, (TM, TM), 1)
    cum_m = jnp.where(((rt >> shift) == (ct >> shift)) & (((ct - rt) * sgn) <= 0), 1.0, 0.0)
    cl_scr[...] = _mm_split3(cum_m.astype(BF16), lw_scr[...])

    ri = lax.broadcasted_iota(jnp.int32, (LANES, LANES), 0)
    ci = lax.broadcasted_iota(jnp.int32, (LANES, LANES), 1)
    same = (ri >> shift) == (ci >> shift)
    rel = (ci - ri) * sgn
    strict = same & (rel < 0)
    incl = same & (rel <= 0)
    eye = (ri == ci).astype(F32)
    levels = []
    for lv in range(shift):
        levels.append(((ri >> (lv + 1)) == (ci >> (lv + 1)))
                      & (((ri >> lv) & 1) == 1 - d) & (((ci >> lv) & 1) == d))
    head0 = lax.broadcasted_iota(jnp.int32, (CHUNK, LANES), 1) < HEAD_DIM
    dot = functools.partial(jnp.dot, preferred_element_type=F32)

    def stack(t):
        return jnp.concatenate([jnp.where(head0, t, 0.0), jnp.where(head0, 0.0, t)], axis=0)

    def prepare(chunks):
        ops = []
        for c in chunks:
            rows = slice(c * CHUNK, (c + 1) * CHUNK)
            lw = lw_scr[rows, :]
            cl = cl_scr[rows, :]
            tot = jnp.sum(lw, axis=0, keepdims=True)
            e_out = jnp.exp(-cl)
            e_end = jnp.exp(tot - cl)
            b_c = b_scr[rows, :]
            kd_c = kd_scr[rows, :]
            a_t = aa_scr[rows, :] * jnp.exp(cl - lw)
            r_t = r_scr[rows, :] * jnp.exp(cl)
            b_t = b_c * e_out
            k_t = kd_c * e_out
            b_h = b_c * e_end
            k_h = kd_c * e_end
            v_c = v_scr[rows, :]
            w_tot = jnp.exp(tot)
            for p in range(N_PAIR):
                i = c * N_PAIR + p
                sl = slice(p * LANES, (p + 1) * LANES)
                a_s = stack(a_t[:, sl]).astype(BF16)
                r_s = stack(r_t[:, sl]).astype(BF16)
                bk_s = jnp.concatenate([stack(b_t[:, sl]), stack(k_t[:, sl])], axis=0).astype(BF16)
                vs_scr[i] = stack(v_c[:, sl]).astype(BF16)
                upd_scr[i] = jnp.concatenate([jnp.transpose(stack(b_h[:, sl])),
                                              jnp.transpose(stack(k_h[:, sl]))], axis=1).astype(BF16)
                wc_scr[i] = jnp.transpose(jnp.broadcast_to(w_tot[:, sl], (LANES, LANES)))
                ops.append((i, a_s, r_s, bk_s))
        a_abs, t_invs = [], []
        for i, a_s, r_s, bk_s in ops:
            ab = _mm_nt(jnp.concatenate([a_s, r_s], axis=0), bk_s)
            a_ab = jnp.where(strict, ab[0:LANES, 0:LANES], 0.0)
            a_ak = jnp.where(strict, ab[0:LANES, LANES:], 0.0)
            a_rb = jnp.where(incl, ab[LANES:, 0:LANES], 0.0)
            a_rk = jnp.where(incl, ab[LANES:, LANES:], 0.0)
            rhs_scr[i] = jnp.concatenate([a_s, a_ak.astype(BF16)], axis=1)
            out_scr[i] = jnp.concatenate([r_s, a_rb.astype(BF16), a_rk.astype(BF16)], axis=1)
            a_abs.append(a_ab)
            t_invs.append(eye + jnp.where(levels[0], a_ab, 0.0))
        for lvl in levels[1:]:
            ms = [_mm(jnp.where(lvl, a_ab, 0.0), t) for a_ab, t in zip(a_abs, t_invs)]
            t_invs = [t + _mm(t, m) for t, m in zip(t_invs, ms)]
        for (i, _, _, _), t in zip(ops, t_invs):
            th = t.astype(BF16)
            ti_scr[i, 0] = th
            ti_scr[i, 1] = (t - th.astype(F32)).astype(BF16)

    def scan_chunk(step):
        cidx = step + d * (N_CHUNK - 1 - 2 * step)
        rows = pl.ds(pl.multiple_of(cidx * CHUNK, CHUNK), CHUNK)
        base = cidx * N_PAIR
        pairs = range(N_PAIR)
        sts = [s_scr[p] for p in pairs]
        stb = [st.astype(BF16) for st in sts]
        vss = [vs_scr[base + p] for p in pairs]
        rhs = [dot(rhs_scr[base + p], jnp.concatenate([stb[p], vss[p]], axis=0)) for p in pairs]
        pmb = []
        for p in pairs:
            rh = rhs[p].astype(BF16)
            rl = (rhs[p] - rh.astype(F32)).astype(BF16)
            th = ti_scr[base + p, 0]
            pm = dot(th, rh) + dot(th, rl) + dot(ti_scr[base + p, 1], rh)
            pmb.append(pm.astype(BF16))
        for p in pairs:
            s_scr[p] = wc_scr[base + p] * sts[p] + dot(
                upd_scr[base + p], jnp.concatenate([pmb[p], vss[p]], axis=0))
        for p in pairs:
            y_s = dot(out_scr[base + p], jnp.concatenate([stb[p], pmb[p], vss[p]], axis=0))
            y_ref[0, rows, p * LANES:(p + 1) * LANES] = y_s[0:CHUNK] + y_s[CHUNK:]

    half = N_CHUNK // 2
    prepare(range(0, half))
    prepare(range(half, N_CHUNK))
    for step in range(N_CHUNK):
        scan_chunk(step)

    @pl.when(j == nblk - 1)
    def _():
        sout_ref[0, 0] = s_scr[...]


def _wkv(rkv, lora, s0, mup, mun, k_k, k_a, r_k, w0, w2p, a0, a2p, *, nseq, nblk):
    n = rkv.shape[0]
    hb = TM // HALO

    def tile(d, s, j):
        return s * nblk + j + d * (nblk - 1 - 2 * j)

    s0_seq = (lambda s: s) if s0.shape[1] > 1 else (lambda s: 0)
    const = lambda d, s, j: (0, 0)
    per_dir = lambda d, s, j: (d, 0, 0)
    kern = functools.partial(_wkv_kernel, nblk=nblk)
    tok = pltpu.VMEM((TM, RWKV_WIDTH), F32)
    n_prob = N_CHUNK * N_PAIR
    return pl.pallas_call(
        kern,
        grid=(2, nseq, nblk),
        in_specs=[
            pl.BlockSpec((TM, 3 * RWKV_WIDTH), lambda d, s, j: (tile(d, s, j), 0)),
            pl.BlockSpec((HALO, 3 * RWKV_WIDTH),
                         lambda d, s, j: (jnp.maximum(tile(d, s, j) * hb - 1, 0), 0)),
            pl.BlockSpec((HALO, 3 * RWKV_WIDTH),
                         lambda d, s, j: (jnp.minimum((tile(d, s, j) + 1) * hb, n // HALO - 1), 0)),
            pl.BlockSpec((TM, 4 * LORA_W), lambda d, s, j: (tile(d, s, j), 0)),
            pl.BlockSpec((1, 1, N_PAIR, LANES, LANES), lambda d, s, j: (d, s0_seq(s), 0, 0, 0)),
            pl.BlockSpec((1, 3 * RWKV_WIDTH), const),
            pl.BlockSpec((1, 3 * RWKV_WIDTH), const),
            pl.BlockSpec((1, RWKV_WIDTH), const),
            pl.BlockSpec((1, RWKV_WIDTH), const),
            pl.BlockSpec((1, RWKV_WIDTH), const),
            pl.BlockSpec((1, 1, RWKV_WIDTH), per_dir),
            pl.BlockSpec((1, 2 * LORA_W, RWKV_WIDTH), per_dir),
            pl.BlockSpec((1, 1, RWKV_WIDTH), per_dir),
            pl.BlockSpec((1, 2 * LORA_W, RWKV_WIDTH), per_dir),
        ],
        out_specs=[
            pl.BlockSpec((1, TM, RWKV_WIDTH), lambda d, s, j: (d, tile(d, s, j), 0)),
            pl.BlockSpec((1, TM, RWKV_WIDTH), lambda d, s, j: (d, tile(d, s, j), 0)),
            pl.BlockSpec((1, 1, N_PAIR, LANES, LANES), lambda d, s, j: (d, s, 0, 0, 0)),
        ],
        out_shape=[jax.ShapeDtypeStruct((2, n, RWKV_WIDTH), F32),
                   jax.ShapeDtypeStruct((2, n, RWKV_WIDTH), F32),
                   jax.ShapeDtypeStruct((2, nseq, N_PAIR, LANES, LANES), F32)],
        scratch_shapes=[pltpu.VMEM((N_PAIR, LANES, LANES), F32), tok, tok, tok, tok, tok, tok, tok,
                        pltpu.VMEM((n_prob, LANES, 2 * LANES), BF16),
                        pltpu.VMEM((n_prob, LANES, 3 * LANES), BF16),
                        pltpu.VMEM((n_prob, LANES, 2 * LANES), BF16),
                        pltpu.VMEM((n_prob, 2, LANES, LANES), BF16),
                        pltpu.VMEM((n_prob, LANES, LANES), BF16),
                        pltpu.VMEM((n_prob, LANES, LANES), F32)],
        compiler_params=pltpu.CompilerParams(
            dimension_semantics=("arbitrary", "arbitrary", "arbitrary"),
            vmem_limit_bytes=VMEM_LIMIT),
        name="wkv",
    )(rkv, rkv, rkv, lora, s0, mup, mun, k_k, k_a, r_k, w0, w2p, a0, a2p)


def _mid_kernel(y_ref, bn_ref, lg_ref, glu_ref, gates_ref, x_ref, mod_ref, g2w_ref, lnxg_ref,
                lnxb_ref, wor_ref, cw_ref, cb_ref, clg_ref, clb_ref, woc_ref, wo_ref, n2g_ref,
                x1_ref, xn2_ref, pad_scr, *, seq):
    mod = mod_ref[0]
    g1 = mod[:, 2 * D_MODEL:3 * D_MODEL]
    sh2 = mod[:, 3 * D_MODEL:4 * D_MODEL]
    sc2 = mod[:, 4 * D_MODEL:5 * D_MODEL]

    bd = _block_diag_ones()
    y = y_ref[0] + y_ref[1]
    yc = y - _head_sum(y, bd) * (1.0 / HEAD_DIM)
    var = _head_sum(yc * yc, bd) * (1.0 / HEAD_DIM)
    yn = yc * lax.rsqrt(var + GN_EPS) * lnxg_ref[...] + lnxb_ref[...]
    g = _mm(jax.nn.sigmoid(lg_ref[...]), g2w_ref[...])
    y_r = _mm((yn + bn_ref[0] + bn_ref[1]) * g, wor_ref[...])

    glu = glu_ref[...]
    uu = glu[:, 0:CONV_WIDTH] * jax.nn.sigmoid(glu[:, CONV_WIDTH:])
    nseq = TM // seq
    zpad = jnp.zeros((nseq, CONV_PAD, CONV_WIDTH), F32)
    pad_scr[0, :, 0:CONV_PAD, :] = zpad
    pad_scr[0, :, CONV_PAD + seq:, :] = zpad
    for s in range(nseq):
        pad_scr[0, s, CONV_PAD:CONV_PAD + seq, :] = uu[s * seq:(s + 1) * seq]
    live = seq + 2 * CONV_PAD - SUBLANES
    for m in range(1, SUBLANES):
        pad_scr[m, :, 0:live, :] = pad_scr[0, :, m:m + live, :]
    first = CONV_PAD - CONV_K // 2
    parts = []
    for s in range(nseq):
        for r0 in range(0, seq, CONV_ROWS):
            acc = jnp.broadcast_to(cb_ref[...], (CONV_ROWS, CONV_WIDTH))
            for t in range(CONV_K):
                m = (first + t) % SUBLANES
                lo = r0 + first + t - m
                acc = acc + pad_scr[m, s, lo:lo + CONV_ROWS, :] * cw_ref[t:t + 1, :]
            parts.append(acc)
    cv = jnp.concatenate(parts, axis=0)
    cm = cv - jnp.mean(cv, axis=-1, keepdims=True)
    cvar = jnp.mean(cm * cm, axis=-1, keepdims=True)
    un = cm * lax.rsqrt(cvar + LN_EPS) * clg_ref[...] + clb_ref[...]
    y_c = _mm(un * jax.nn.sigmoid(un), woc_ref[...])

    gates = gates_ref[...]
    merged = jax.nn.sigmoid(gates[:, 0:D_MODEL]) * y_r + jax.nn.sigmoid(gates[:, D_MODEL:]) * y_c
    x1 = x_ref[...] + g1 * _mm(merged, wo_ref[...])
    x1_ref[...] = x1
    xn2_ref[...] = (_rms(x1) * n2g_ref[...] * (1.0 + sc2) + sh2).astype(BF16)


def _mid(y2, bn2, lg, glu, gates, x, mod3, mod_row, g2w, lnxg, lnxb, wor, cw, cb, clg, clb, woc,
         wo, n2g, *, seq):
    n = x.shape[0]
    const = lambda i: (0, 0)
    tok = lambda i: (i, 0)
    both = lambda i: (0, i, 0)

    def full(a):
        return pl.BlockSpec(a.shape, const)

    return pl.pallas_call(
        functools.partial(_mid_kernel, seq=seq),
        grid=(n // TM,),
        in_specs=[pl.BlockSpec((2, TM, RWKV_WIDTH), both),
                  pl.BlockSpec((2, TM, RWKV_WIDTH), both),
                  pl.BlockSpec((TM, GATE_LORA), tok),
                  pl.BlockSpec((TM, 2 * CONV_WIDTH), tok),
                  pl.BlockSpec((TM, 2 * D_MODEL), tok),
                  pl.BlockSpec((TM, D_MODEL), tok),
                  pl.BlockSpec((1, 1, 6 * D_MODEL), lambda i: (mod_row(i), 0, 0)),
                  full(g2w), full(lnxg), full(lnxb), full(wor), full(cw), full(cb), full(clg),
                  full(clb), full(woc), full(wo), full(n2g)],
        out_specs=[pl.BlockSpec((TM, D_MODEL), tok), pl.BlockSpec((TM, D_MODEL), tok)],
        out_shape=[jax.ShapeDtypeStruct((n, D_MODEL), F32),
                   jax.ShapeDtypeStruct((n, D_MODEL), BF16)],
        scratch_shapes=[pltpu.VMEM((SUBLANES, TM // seq, seq + 2 * CONV_PAD, CONV_WIDTH), F32)],
        compiler_params=pltpu.CompilerParams(dimension_semantics=("arbitrary",),
                                             vmem_limit_bytes=VMEM_LIMIT),
        name="mid",
    )(y2, bn2, lg, glu, gates, x, mod3, g2w, lnxg, lnxb, wor, cw, cb, clg, clb, woc, wo, n2g)


def _mlp_kernel(x1_ref, xn2_ref, mod_ref, w1_ref, w2_ref, fg_ref, o_ref):
    g2 = mod_ref[0][:, 5 * D_MODEL:6 * D_MODEL]
    h = jnp.maximum(jnp.dot(xn2_ref[...], w1_ref[...], preferred_element_type=F32), 0.0)
    x2 = x1_ref[...] + g2 * _mm(h * h, w2_ref[...])
    o_ref[...] = _rms(x2) * fg_ref[...]


def _mlp(x1, xn2, mod3, mod_row, w1, w2, fg):
    n = x1.shape[0]
    const = lambda i: (0, 0)
    tok = lambda i: (i, 0)
    return pl.pallas_call(
        _mlp_kernel,
        grid=(n // TM,),
        in_specs=[pl.BlockSpec((TM, D_MODEL), tok),
                  pl.BlockSpec((TM, D_MODEL), tok),
                  pl.BlockSpec((1, 1, 6 * D_MODEL), lambda i: (mod_row(i), 0, 0)),
                  pl.BlockSpec(w1.shape, const),
                  pl.BlockSpec(w2.shape, const),
                  pl.BlockSpec((1, D_MODEL), const)],
        out_specs=pl.BlockSpec((TM, D_MODEL), tok),
        out_shape=jax.ShapeDtypeStruct((n, D_MODEL), F32),
        compiler_params=pltpu.CompilerParams(dimension_semantics=("arbitrary",),
                                             vmem_limit_bytes=VMEM_LIMIT),
        name="mlp",
    )(x1, xn2, mod3, w1, w2, fg)


def _state_to_blocks(s):
    lead = s.shape[:-3]
    st = jnp.swapaxes(s.astype(F32), -1, -2).reshape(*lead, N_PAIR, 2, HEAD_DIM, 1, HEAD_DIM)
    eye = jnp.eye(2, dtype=F32).reshape(2, 1, 2, 1)
    return (st * eye).reshape(*lead, N_PAIR, LANES, LANES)


def _blocks_to_state(sb):
    lead = sb.shape[:-3]
    s6 = sb.reshape(*lead, N_PAIR, 2, HEAD_DIM, 2, HEAD_DIM)
    st = jnp.sum(s6 * jnp.eye(2, dtype=F32).reshape(2, 1, 2, 1), axis=-2)
    return jnp.swapaxes(st, -1, -2).reshape(*lead, N_HEADS, HEAD_DIM, HEAD_DIM)


def _pad_dir(w):
    return (jnp.eye(2, dtype=w.dtype)[:, :, None, None] * w[:, None]).reshape(2, 2 * w.shape[1], w.shape[2])


def kernel(x_prompt, x_sample, state_fwd, state_bwd, c, c_ctx, ada_w, ada_b, norm1_g, norm2_g, w_in,
           mu_prev, mu_next, decay_w0, decay_w1, decay_w2, iclr_a0, iclr_a1, iclr_a2, gate_g1, gate_g2,
           k_k, k_a, r_k, lnx_g, lnx_b, w_out_rwkv, conv_w, conv_b, conv_ln_g, conv_ln_b, w_out_conv,
           w_o, mlp_w1, mlp_w2, final_g):
    n_ctx, seq_ctx, _ = x_prompt.shape
    n_lat, seq_lat, _ = x_sample.shape
    depth = ada_w.shape[0]
    assert seq_ctx == TM and seq_lat % TM == 0 and GRID_W == CHUNK and c.shape[0] + 1 <= 8
    lat_blk = seq_lat // TM
    row_ctx = lambda i: 0
    row_lat = lambda i: 1 + i // lat_blk

    xp = x_prompt.reshape(n_ctx * seq_ctx, D_MODEL)
    xs = x_sample.reshape(n_lat * seq_lat, D_MODEL)
    cond = jnp.concatenate([c_ctx[None, :], c, jnp.zeros((7 - c.shape[0], D_MODEL), F32)], axis=0)
    zero_state = jnp.zeros((2, 1, N_PAIR, LANES, LANES), F32)
    row = lambda a: a.reshape(1, -1)
    assert depth == 1
    lay = lambda a: a.reshape(a.shape[1:])
    bf = lambda a: lay(a).astype(BF16)

    mod3 = _ada(cond, lay(ada_w), lay(ada_b)).reshape(8, 1, 6 * D_MODEL)
    w_lora = jnp.concatenate([decay_w1[0, 0], decay_w1[0, 1], iclr_a1[0, 0], iclr_a1[0, 1], gate_g1[0]],
                             axis=1).astype(BF16)
    front_w = (norm1_g, bf(w_in), w_lora)
    wkv_w = (mu_prev, mu_next, k_k, k_a, row(r_k),
             decay_w0.reshape(2, 1, RWKV_WIDTH), _pad_dir(lay(decay_w2)).astype(BF16),
             iclr_a0.reshape(2, 1, RWKV_WIDTH), _pad_dir(lay(iclr_a2)).astype(BF16))
    mid_w = (bf(gate_g2), lnx_g, lnx_b, bf(w_out_rwkv), lay(conv_w), conv_b, conv_ln_g, conv_ln_b,
             bf(w_out_conv), bf(w_o), norm2_g)
    mlp_w = (bf(mlp_w1), bf(mlp_w2), row(final_g))
    s0_lat = _state_to_blocks(jnp.stack([state_fwd[:, 0], state_bwd[:, 0]]))

    def layer(x, mod_row, s0, nseq, nblk, conv_seq):
        rkv, glu, gates, lora, lg = _front(x, mod3, mod_row, *front_w)
        y2, bn2, s_out = _wkv(rkv, lora, s0, *wkv_w, nseq=nseq, nblk=nblk)
        x1, xn2 = _mid(y2, bn2, lg, glu, gates, x, mod3, mod_row, *mid_w, seq=conv_seq)
        return _mlp(x1, xn2, mod3, mod_row, *mlp_w), s_out

    yp, s_ctx = layer(xp, row_ctx, zero_state, n_ctx, 1, seq_ctx)
    ys, _ = layer(xs, row_lat, s0_lat, n_lat, lat_blk, GRID_W)
    states = _blocks_to_state(s_ctx).astype(x_prompt.dtype)[:, :, None]
    return (yp.reshape(x_prompt.shape), ys.reshape(x_sample.shape), states[0], states[1])
```

```python
import functools

import jax
import jax.numpy as jnp
from jax import lax
from jax.experimental import pallas as pl
from jax.experimental.pallas import tpu as pltpu

D_MODEL = 1024
RWKV_WIDTH = 512
HEAD_DIM = 64
N_HEADS = RWKV_WIDTH // HEAD_DIM
CONV_WIDTH = 512
CONV_K = 31
D_FF = 4 * D_MODEL
GRID_W = 64
LORA_W = 64
GATE_LORA = 128
RMS_EPS = 1e-6
LN_EPS = 1e-5
GN_EPS = 64e-5

LANES = 128
SUBLANES = 8
TM = 256
CHUNK = 64
N_CHUNK = TM // CHUNK
N_PAIR = RWKV_WIDTH // LANES
HALO = SUBLANES
CONV_PAD = 16
CONV_ROWS = 32
VMEM_LIMIT = 56 * 1024 * 1024

F32 = jnp.float32
BF16 = jnp.bfloat16
HIGHEST = lax.Precision.HIGHEST
DECAY_SCALE = 0.6065306597126334


def _mm(a, b):
    return jnp.dot(a.astype(BF16), b.astype(BF16), preferred_element_type=F32)


def _mm_nt(a, b):
    return lax.dot_general(a.astype(BF16), b.astype(BF16), (((1,), (1,)), ((), ())),
                           preferred_element_type=F32)


def _mm_split3(m, x):
    dot = functools.partial(jnp.dot, preferred_element_type=F32)
    x1 = x.astype(BF16)
    r1 = x - x1.astype(F32)
    x2 = r1.astype(BF16)
    x3 = (r1 - x2.astype(F32)).astype(BF16)
    return dot(m, x3) + dot(m, x2) + dot(m, x1)


def _mm_f32(a, b):
    return jnp.dot(a, b, precision=HIGHEST, preferred_element_type=F32)


def _head_sum(x, bd):
    return jnp.concatenate(
        [_mm_f32(x[:, p * LANES:(p + 1) * LANES], bd) for p in range(x.shape[1] // LANES)], axis=1)


def _block_diag_ones():
    ri = lax.broadcasted_iota(jnp.int32, (LANES, LANES), 0)
    ci = lax.broadcasted_iota(jnp.int32, (LANES, LANES), 1)
    shift = HEAD_DIM.bit_length() - 1
    return ((ri >> shift) == (ci >> shift)).astype(F32)


def _rms(x):
    return x * lax.rsqrt(jnp.mean(x * x, axis=-1, keepdims=True) + RMS_EPS)


def _ada_kernel(c_ref, w_ref, b_ref, o_ref):
    c = c_ref[...]
    o_ref[...] = _mm_f32(c * jax.nn.sigmoid(c), w_ref[...]) + b_ref[...]


def _ada(cond, w, b):
    n = w.shape[1]
    tn = 1024
    return pl.pallas_call(
        _ada_kernel,
        grid=(n // tn,),
        in_specs=[pl.BlockSpec((8, D_MODEL), lambda i: (0, 0)),
                  pl.BlockSpec((D_MODEL, tn), lambda i: (0, i)),
                  pl.BlockSpec((1, tn), lambda i: (0, i))],
        out_specs=pl.BlockSpec((8, tn), lambda i: (0, i)),
        out_shape=jax.ShapeDtypeStruct((8, n), F32),
        compiler_params=pltpu.CompilerParams(dimension_semantics=("arbitrary",),
                                             vmem_limit_bytes=VMEM_LIMIT),
        name="ada",
    )(cond, w, b.reshape(1, n))


_IN_SPLIT = (3 * RWKV_WIDTH, 2 * CONV_WIDTH, 2 * D_MODEL)
_LORA_SPLIT = (4 * LORA_W, GATE_LORA)


def _front_kernel(x_ref, mod_ref, g_ref, win_ref, wlo_ref, *out_refs):
    mod = mod_ref[0]
    sh1 = mod[:, 0:D_MODEL]
    sc1 = mod[:, D_MODEL:2 * D_MODEL]
    xn = (_rms(x_ref[...]) * g_ref[...] * (1.0 + sc1) + sh1).astype(BF16)
    outs = iter(out_refs)
    for w_ref, split in ((win_ref, _IN_SPLIT), (wlo_ref, _LORA_SPLIT)):
        lo = 0
        for width in split:
            next(outs)[...] = jnp.dot(xn, w_ref[:, lo:lo + width], preferred_element_type=F32)
            lo += width


def _front(x, mod3, mod_row, g, w_in, w_lora):
    n = x.shape[0]
    const = lambda i: (0, 0)
    tok = lambda i: (i, 0)
    widths = _IN_SPLIT + _LORA_SPLIT
    return pl.pallas_call(
        _front_kernel,
        grid=(n // TM,),
        in_specs=[pl.BlockSpec((TM, D_MODEL), tok),
                  pl.BlockSpec((1, 1, 6 * D_MODEL), lambda i: (mod_row(i), 0, 0)),
                  pl.BlockSpec((1, D_MODEL), const),
                  pl.BlockSpec(w_in.shape, const),
                  pl.BlockSpec(w_lora.shape, const)],
        out_specs=[pl.BlockSpec((TM, w), tok) for w in widths],
        out_shape=[jax.ShapeDtypeStruct((n, w), F32) for w in widths],
        compiler_params=pltpu.CompilerParams(dimension_semantics=("arbitrary",),
                                             vmem_limit_bytes=VMEM_LIMIT),
        name="front",
    )(x, mod3, g, w_in, w_lora)


def _wkv_kernel(rkv_ref, prev_ref, next_ref, lora_ref, s0_ref, mup_ref, mun_ref, kk_ref, ka_ref,
                rk_ref, w0_ref, w2_ref, a0_ref, a2_ref,
                y_ref, bonus_ref, sout_ref,
                s_scr, r_scr, kd_scr, v_scr, aa_scr, b_scr, lw_scr, cl_scr,
                rhs_scr, out_scr, upd_scr, ti_scr, vs_scr, wc_scr, *, nblk):
    d = pl.program_id(0)
    j = pl.program_id(2)
    sgn = 1 - 2 * d
    blk = j + d * (nblk - 1 - 2 * j)

    @pl.when(j == 0)
    def _():
        zero = jnp.zeros((HEAD_DIM, HEAD_DIM), F32)
        for p in range(N_PAIR):
            top = jnp.concatenate([s0_ref[0, 0, 2 * p], zero], axis=1)
            bot = jnp.concatenate([zero, s0_ref[0, 0, 2 * p + 1]], axis=1)
            s_scr[p] = jnp.transpose(jnp.concatenate([top, bot], axis=0))

    u = rkv_ref[...]
    row = lax.broadcasted_iota(jnp.int32, (TM, 1), 0)
    prow = jnp.where(blk > 0, prev_ref[HALO - 1:HALO, :], 0.0)
    nrow = jnp.where(blk < nblk - 1, next_ref[0:1, :], 0.0)
    prev = jnp.where(row == 0, prow, pltpu.roll(u, 1, axis=0))
    nxt = jnp.where(row == TM - 1, nrow, pltpu.roll(u, TM - 1, axis=0))
    x = u + (prev - u) * mup_ref[...] + (nxt - u) * mun_ref[...]
    r = x[:, 0:RWKV_WIDTH]
    k = x[:, RWKV_WIDTH:2 * RWKV_WIDTH]
    v = x[:, 2 * RWKV_WIDTH:]

    bd = _block_diag_ones()
    kkr = k * kk_ref[...]
    kk = kkr * lax.rsqrt(jnp.maximum(_head_sum(kkr * kkr, bd), 1e-24))
    lo = lora_ref[...]
    z = w0_ref[0] + _mm(jnp.tanh(lo[:, 0:2 * LORA_W]), w2_ref[0])
    lw_scr[...] = -DECAY_SCALE * jax.nn.sigmoid(z)
    a = jax.nn.sigmoid(a0_ref[0] + _mm(lo[:, 2 * LORA_W:4 * LORA_W], a2_ref[0]))
    kd = k * (1.0 + (a - 1.0) * ka_ref[...])
    bonus_ref[0] = _head_sum(r * kd * rk_ref[...], bd) * v
    r_scr[...] = r
    kd_scr[...] = kd
    v_scr[...] = v
    aa_scr[...] = -kk
    b_scr[...] = kk * a

    shift = CHUNK.bit_length() - 1
    rt = lax.broadcasted_iota(jnp.int32, (TM, TM), 0)
    ct = lax.broadcasted_iota(jnp.int32, (TM, TM), 1)
    cum_m = jnp.where(((rt >> shift) == (ct >> shift)) & (((ct - rt) * sgn) <= 0), 1.0, 0.0)
    cl_scr[...] = _mm_split3(cum_m.astype(BF16), lw_scr[...])

    ri = lax.broadcasted_iota(jnp.int32, (LANES, LANES), 0)
    ci = lax.broadcasted_iota(jnp.int32, (LANES, LANES), 1)
    same = (ri >> shift) == (ci >> shift)
    rel = (ci - ri) * sgn
    strict = same & (rel < 0)
    incl = same & (rel <= 0)
    eye = (ri == ci).astype(F32)
    levels = []
    for lv in range(shift):
        levels.append(((ri >> (lv + 1)) == (ci >> (lv + 1)))
                      & (((ri >> lv) & 1) == 1 - d) & (((ci >> lv) & 1) == d))
    head0 = lax.broadcasted_iota(jnp.int32, (CHUNK, LANES), 1) < HEAD_DIM
    dot = functools.partial(jnp.dot, preferred_element_type=F32)

    def stack(t):
        return jnp.concatenate([jnp.where(head0, t, 0.0), jnp.where(head0, 0.0, t)], axis=0)

    def prepare(chunks):
        ops = []
        for c in chunks:
            rows = slice(c * CHUNK, (c + 1) * CHUNK)
            lw = lw_scr[rows, :]
            cl = cl_scr[rows, :]
            tot = jnp.sum(lw, axis=0, keepdims=True)
            e_out = jnp.exp(-cl)
            e_end = jnp.exp(tot - cl)
            b_c = b_scr[rows, :]
            kd_c = kd_scr[rows, :]
            a_t = aa_scr[rows, :] * jnp.exp(cl - lw)
            r_t = r_scr[rows, :] * jnp.exp(cl)
            b_t = b_c * e_out
            k_t = kd_c * e_out
            b_h = b_c * e_end
            k_h = kd_c * e_end
            v_c = v_scr[rows, :]
            w_tot = jnp.exp(tot)
            for p in range(N_PAIR):
                i = c * N_PAIR + p
                sl = slice(p * LANES, (p + 1) * LANES)
                a_s = stack(a_t[:, sl]).astype(BF16)
                r_s = stack(r_t[:, sl]).astype(BF16)
                bk_s = jnp.concatenate([stack(b_t[:, sl]), stack(k_t[:, sl])], axis=0).astype(BF16)
                vs_scr[i] = stack(v_c[:, sl]).astype(BF16)
                upd_scr[i] = jnp.concatenate([jnp.transpose(stack(b_h[:, sl])),
                                              jnp.transpose(stack(k_h[:, sl]))], axis=1).astype(BF16)
                wc_scr[i] = jnp.transpose(jnp.broadcast_to(w_tot[:, sl], (LANES, LANES)))
                ops.append((i, a_s, r_s, bk_s))
        a_abs, t_invs = [], []
        for i, a_s, r_s, bk_s in ops:
            ab = _mm_nt(jnp.concatenate([a_s, r_s], axis=0), bk_s)
            a_ab = jnp.where(strict, ab[0:LANES, 0:LANES], 0.0)
            a_ak = jnp.where(strict, ab[0:LANES, LANES:], 0.0)
            a_rb = jnp.where(incl, ab[LANES:, 0:LANES], 0.0)
            a_rk = jnp.where(incl, ab[LANES:, LANES:], 0.0)
            rhs_scr[i] = jnp.concatenate([a_s, a_ak.astype(BF16)], axis=1)
            out_scr[i] = jnp.concatenate([r_s, a_rb.astype(BF16), a_rk.astype(BF16)], axis=1)
            a_abs.append(a_ab)
            t_invs.append(eye + jnp.where(levels[0], a_ab, 0.0))
        for lvl in levels[1:]:
            ms = [_mm(jnp.where(lvl, a_ab, 0.0), t) for a_ab, t in zip(a_abs, t_invs)]
            t_invs = [t + _mm(t, m) for t, m in zip(t_invs, ms)]
        for (i, _, _, _), t in zip(ops, t_invs):
            th = t.astype(BF16)
            ti_scr[i, 0] = th
            ti_scr[i, 1] = (t - th.astype(F32)).astype(BF16)

    def scan_chunk(step):
        cidx = step + d * (N_CHUNK - 1 - 2 * step)
        rows = pl.ds(pl.multiple_of(cidx * CHUNK, CHUNK), CHUNK)
        base = cidx * N_PAIR
        pairs = range(N_PAIR)
        sts = [s_scr[p] for p in pairs]
        stb = [st.astype(BF16) for st in sts]
        vss = [vs_scr[base + p] for p in pairs]
        rhs = [dot(rhs_scr[base + p], jnp.concatenate([stb[p], vss[p]], axis=0)) for p in pairs]
        pmb = []
        for p in pairs:
            rh = rhs[p].astype(BF16)
            rl = (rhs[p] - rh.astype(F32)).astype(BF16)
            th = ti_scr[base + p, 0]
            pm = dot(th, rh) + dot(th, rl) + dot(ti_scr[base + p, 1], rh)
            pmb.append(pm.astype(BF16))
        for p in pairs:
            s_scr[p] = wc_scr[base + p] * sts[p] + dot(
                upd_scr[base + p], jnp.concatenate([pmb[p], vss[p]], axis=0))
        for p in pairs:
            y_s = dot(out_scr[base + p], jnp.concatenate([stb[p], pmb[p], vss[p]], axis=0))
            y_ref[0, rows, p * LANES:(p + 1) * LANES] = y_s[0:CHUNK] + y_s[CHUNK:]

    half = N_CHUNK // 2
    prepare(range(0, half))
    prepare(range(half, N_CHUNK))
    for step in range(N_CHUNK):
        scan_chunk(step)

    @pl.when(j == nblk - 1)
    def _():
        for p in range(N_PAIR):
            s_vk = jnp.transpose(s_scr[p])
            sout_ref[0, 0, 2 * p] = s_vk[0:HEAD_DIM, 0:HEAD_DIM]
            sout_ref[0, 0, 2 * p + 1] = s_vk[HEAD_DIM:, HEAD_DIM:]


def _wkv(rkv, lora, s0, mup, mun, k_k, k_a, r_k, w0, w2p, a0, a2p, *, nseq, nblk):
    n = rkv.shape[0]
    hb = TM // HALO

    def tile(d, s, j):
        return s * nblk + j + d * (nblk - 1 - 2 * j)

    s0_seq = (lambda s: s) if s0.shape[1] > 1 else (lambda s: 0)
    const = lambda d, s, j: (0, 0)
    per_dir = lambda d, s, j: (d, 0, 0)
    kern = functools.partial(_wkv_kernel, nblk=nblk)
    tok = pltpu.VMEM((TM, RWKV_WIDTH), F32)
    n_prob = N_CHUNK * N_PAIR
    return pl.pallas_call(
        kern,
        grid=(2, nseq, nblk),
        in_specs=[
            pl.BlockSpec((TM, 3 * RWKV_WIDTH), lambda d, s, j: (tile(d, s, j), 0)),
            pl.BlockSpec((HALO, 3 * RWKV_WIDTH),
                         lambda d, s, j: (jnp.maximum(tile(d, s, j) * hb - 1, 0), 0)),
            pl.BlockSpec((HALO, 3 * RWKV_WIDTH),
                         lambda d, s, j: (jnp.minimum((tile(d, s, j) + 1) * hb, n // HALO - 1), 0)),
            pl.BlockSpec((TM, 4 * LORA_W), lambda d, s, j: (tile(d, s, j), 0)),
            pl.BlockSpec((1, 1, N_HEADS, HEAD_DIM, HEAD_DIM), lambda d, s, j: (d, s0_seq(s), 0, 0, 0)),
            pl.BlockSpec((1, 3 * RWKV_WIDTH), const),
            pl.BlockSpec((1, 3 * RWKV_WIDTH), const),
            pl.BlockSpec((1, RWKV_WIDTH), const),
            pl.BlockSpec((1, RWKV_WIDTH), const),
            pl.BlockSpec((1, RWKV_WIDTH), const),
            pl.BlockSpec((1, 1, RWKV_WIDTH), per_dir),
            pl.BlockSpec((1, 2 * LORA_W, RWKV_WIDTH), per_dir),
            pl.BlockSpec((1, 1, RWKV_WIDTH), per_dir),
            pl.BlockSpec((1, 2 * LORA_W, RWKV_WIDTH), per_dir),
        ],
        out_specs=[
            pl.BlockSpec((1, TM, RWKV_WIDTH), lambda d, s, j: (d, tile(d, s, j), 0)),
            pl.BlockSpec((1, TM, RWKV_WIDTH), lambda d, s, j: (d, tile(d, s, j), 0)),
            pl.BlockSpec((1, 1, N_HEADS, HEAD_DIM, HEAD_DIM), lambda d, s, j: (d, s, 0, 0, 0)),
        ],
        out_shape=[jax.ShapeDtypeStruct((2, n, RWKV_WIDTH), F32),
                   jax.ShapeDtypeStruct((2, n, RWKV_WIDTH), F32),
                   jax.ShapeDtypeStruct((2, nseq, N_HEADS, HEAD_DIM, HEAD_DIM), F32)],
        scratch_shapes=[pltpu.VMEM((N_PAIR, LANES, LANES), F32), tok, tok, tok, tok, tok, tok, tok,
                        pltpu.VMEM((n_prob, LANES, 2 * LANES), BF16),
                        pltpu.VMEM((n_prob, LANES, 3 * LANES), BF16),
                        pltpu.VMEM((n_prob, LANES, 2 * LANES), BF16),
                        pltpu.VMEM((n_prob, 2, LANES, LANES), BF16),
                        pltpu.VMEM((n_prob, LANES, LANES), BF16),
                        pltpu.VMEM((n_prob, LANES, LANES), F32)],
        compiler_params=pltpu.CompilerParams(
            dimension_semantics=("arbitrary", "arbitrary", "arbitrary"),
            vmem_limit_bytes=VMEM_LIMIT),
        name="wkv",
    )(rkv, rkv, rkv, lora, s0, mup, mun, k_k, k_a, r_k, w0, w2p, a0, a2p)


def _mid_kernel(y_ref, bn_ref, lg_ref, glu_ref, gates_ref, x_ref, mod_ref, g2w_ref, lnxg_ref,
                lnxb_ref, wor_ref, cw_ref, cb_ref, clg_ref, clb_ref, woc_ref, wo_ref, n2g_ref,
                x1_ref, xn2_ref, pad_scr, *, seq):
    mod = mod_ref[0]
    g1 = mod[:, 2 * D_MODEL:3 * D_MODEL]
    sh2 = mod[:, 3 * D_MODEL:4 * D_MODEL]
    sc2 = mod[:, 4 * D_MODEL:5 * D_MODEL]

    bd = _block_diag_ones()
    y = y_ref[0] + y_ref[1]
    yc = y - _head_sum(y, bd) * (1.0 / HEAD_DIM)
    var = _head_sum(yc * yc, bd) * (1.0 / HEAD_DIM)
    yn = yc * lax.rsqrt(var + GN_EPS) * lnxg_ref[...] + lnxb_ref[...]
    g = _mm(jax.nn.sigmoid(lg_ref[...]), g2w_ref[...])
    y_r = _mm((yn + bn_ref[0] + bn_ref[1]) * g, wor_ref[...])

    glu = glu_ref[...]
    uu = glu[:, 0:CONV_WIDTH] * jax.nn.sigmoid(glu[:, CONV_WIDTH:])
    nseq = TM // seq
    zpad = jnp.zeros((nseq, CONV_PAD, CONV_WIDTH), F32)
    pad_scr[0, :, 0:CONV_PAD, :] = zpad
    pad_scr[0, :, CONV_PAD + seq:, :] = zpad
    for s in range(nseq):
        pad_scr[0, s, CONV_PAD:CONV_PAD + seq, :] = uu[s * seq:(s + 1) * seq]
    live = seq + 2 * CONV_PAD - SUBLANES
    for m in range(1, SUBLANES):
        pad_scr[m, :, 0:live, :] = pad_scr[0, :, m:m + live, :]
    first = CONV_PAD - CONV_K // 2
    parts = []
    for s in range(nseq):
        for r0 in range(0, seq, CONV_ROWS):
            acc = jnp.broadcast_to(cb_ref[...], (CONV_ROWS, CONV_WIDTH))
            for t in range(CONV_K):
                m = (first + t) % SUBLANES
                lo = r0 + first + t - m
                acc = acc + pad_scr[m, s, lo:lo + CONV_ROWS, :] * cw_ref[t:t + 1, :]
            parts.append(acc)
    cv = jnp.concatenate(parts, axis=0)
    cm = cv - jnp.mean(cv, axis=-1, keepdims=True)
    cvar = jnp.mean(cm * cm, axis=-1, keepdims=True)
    un = cm * lax.rsqrt(cvar + LN_EPS) * clg_ref[...] + clb_ref[...]
    y_c = _mm(un * jax.nn.sigmoid(un), woc_ref[...])

    gates = gates_ref[...]
    merged = jax.nn.sigmoid(gates[:, 0:D_MODEL]) * y_r + jax.nn.sigmoid(gates[:, D_MODEL:]) * y_c
    x1 = x_ref[...] + g1 * _mm(merged, wo_ref[...])
    x1_ref[...] = x1
    xn2_ref[...] = (_rms(x1) * n2g_ref[...] * (1.0 + sc2) + sh2).astype(BF16)


def _mid(y2, bn2, lg, glu, gates, x, mod3, mod_row, g2w, lnxg, lnxb, wor, cw, cb, clg, clb, woc,
         wo, n2g, *, seq):
    n = x.shape[0]
    const = lambda i: (0, 0)
    tok = lambda i: (i, 0)
    both = lambda i: (0, i, 0)

    def full(a):
        return pl.BlockSpec(a.shape, const)

    return pl.pallas_call(
        functools.partial(_mid_kernel, seq=seq),
        grid=(n // TM,),
        in_specs=[pl.BlockSpec((2, TM, RWKV_WIDTH), both),
                  pl.BlockSpec((2, TM, RWKV_WIDTH), both),
                  pl.BlockSpec((TM, GATE_LORA), tok),
                  pl.BlockSpec((TM, 2 * CONV_WIDTH), tok),
                  pl.BlockSpec((TM, 2 * D_MODEL), tok),
                  pl.BlockSpec((TM, D_MODEL), tok),
                  pl.BlockSpec((1, 1, 6 * D_MODEL), lambda i: (mod_row(i), 0, 0)),
                  full(g2w), full(lnxg), full(lnxb), full(wor), full(cw), full(cb), full(clg),
                  full(clb), full(woc), full(wo), full(n2g)],
        out_specs=[pl.BlockSpec((TM, D_MODEL), tok), pl.BlockSpec((TM, D_MODEL), tok)],
        out_shape=[jax.ShapeDtypeStruct((n, D_MODEL), F32),
                   jax.ShapeDtypeStruct((n, D_MODEL), BF16)],
        scratch_shapes=[pltpu.VMEM((SUBLANES, TM // seq, seq + 2 * CONV_PAD, CONV_WIDTH), F32)],
        compiler_params=pltpu.CompilerParams(dimension_semantics=("arbitrary",),
                                             vmem_limit_bytes=VMEM_LIMIT),
        name="mid",
    )(y2, bn2, lg, glu, gates, x, mod3, g2w, lnxg, lnxb, wor, cw, cb, clg, clb, woc, wo, n2g)


def _mlp_kernel(x1_ref, xn2_ref, mod_ref, w1_ref, w2_ref, fg_ref, o_ref):
    g2 = mod_ref[0][:, 5 * D_MODEL:6 * D_MODEL]
    h = jnp.maximum(jnp.dot(xn2_ref[...], w1_ref[...], preferred_element_type=F32), 0.0)
    x2 = x1_ref[...] + g2 * _mm(h * h, w2_ref[...])
    o_ref[...] = _rms(x2) * fg_ref[...]


def _mlp(x1, xn2, mod3, mod_row, w1, w2, fg):
    n = x1.shape[0]
    const = lambda i: (0, 0)
    tok = lambda i: (i, 0)
    return pl.pallas_call(
        _mlp_kernel,
        grid=(n // TM,),
        in_specs=[pl.BlockSpec((TM, D_MODEL), tok),
                  pl.BlockSpec((TM, D_MODEL), tok),
                  pl.BlockSpec((1, 1, 6 * D_MODEL), lambda i: (mod_row(i), 0, 0)),
                  pl.BlockSpec(w1.shape, const),
                  pl.BlockSpec(w2.shape, const),
                  pl.BlockSpec((1, D_MODEL), const)],
        out_specs=pl.BlockSpec((TM, D_MODEL), tok),
        out_shape=jax.ShapeDtypeStruct((n, D_MODEL), F32),
        compiler_params=pltpu.CompilerParams(dimension_semantics=("arbitrary",),
                                             vmem_limit_bytes=VMEM_LIMIT),
        name="mlp",
    )(x1, xn2, mod3, w1, w2, fg)


def _pad_dir(w):
    return (jnp.eye(2, dtype=w.dtype)[:, :, None, None] * w[:, None]).reshape(2, 2 * w.shape[1], w.shape[2])


def kernel(x_prompt, x_sample, state_fwd, state_bwd, c, c_ctx, ada_w, ada_b, norm1_g, norm2_g, w_in,
           mu_prev, mu_next, decay_w0, decay_w1, decay_w2, iclr_a0, iclr_a1, iclr_a2, gate_g1, gate_g2,
           k_k, k_a, r_k, lnx_g, lnx_b, w_out_rwkv, conv_w, conv_b, conv_ln_g, conv_ln_b, w_out_conv,
           w_o, mlp_w1, mlp_w2, final_g):
    n_ctx, seq_ctx, _ = x_prompt.shape
    n_lat, seq_lat, _ = x_sample.shape
    depth = ada_w.shape[0]
    assert seq_ctx == TM and seq_lat % TM == 0 and GRID_W == CHUNK and c.shape[0] + 1 <= 8
    lat_blk = seq_lat // TM
    row_ctx = lambda i: 0
    row_lat = lambda i: 1 + i // lat_blk

    xp = x_prompt.reshape(n_ctx * seq_ctx, D_MODEL)
    xs = x_sample.reshape(n_lat * seq_lat, D_MODEL)
    cond = jnp.concatenate([c_ctx[None, :], c, jnp.zeros((7 - c.shape[0], D_MODEL), F32)], axis=0)
    zero_state = jnp.zeros((2, 1, N_HEADS, HEAD_DIM, HEAD_DIM), F32)
    row = lambda a: a.reshape(1, -1)
    assert depth == 1
    lay = lambda a: a.reshape(a.shape[1:])
    bf = lambda a: lay(a).astype(BF16)

    mod3 = _ada(cond, lay(ada_w), lay(ada_b)).reshape(8, 1, 6 * D_MODEL)
    w_lora = jnp.concatenate([decay_w1[0, 0], decay_w1[0, 1], iclr_a1[0, 0], iclr_a1[0, 1], gate_g1[0]],
                             axis=1).astype(BF16)
    front_w = (norm1_g, bf(w_in), w_lora)
    wkv_w = (mu_prev, mu_next, k_k, k_a, row(r_k),
             decay_w0.reshape(2, 1, RWKV_WIDTH), _pad_dir(lay(decay_w2)).astype(BF16),
             iclr_a0.reshape(2, 1, RWKV_WIDTH), _pad_dir(lay(iclr_a2)).astype(BF16))
    mid_w = (bf(gate_g2), lnx_g, lnx_b, bf(w_out_rwkv), lay(conv_w), conv_b, conv_ln_g, conv_ln_b,
             bf(w_out_conv), bf(w_o), norm2_g)
    mlp_w = (bf(mlp_w1), bf(mlp_w2), row(final_g))
    s0_lat = jnp.stack([state_fwd[:, 0], state_bwd[:, 0]]).astype(F32)

    def layer(x, mod_row, s0, nseq, nblk, conv_seq):
        rkv, glu, gates, lora, lg = _front(x, mod3, mod_row, *front_w)
        y2, bn2, s_out = _wkv(rkv, lora, s0, *wkv_w, nseq=nseq, nblk=nblk)
        x1, xn2 = _mid(y2, bn2, lg, glu, gates, x, mod3, mod_row, *mid_w, seq=conv_seq)
        return _mlp(x1, xn2, mod3, mod_row, *mlp_w), s_out

    yp, s_ctx = layer(xp, row_ctx, zero_state, n_ctx, 1, seq_ctx)
    ys, _ = layer(xs, row_lat, s0_lat, n_lat, lat_blk, GRID_W)
    states = s_ctx.astype(x_prompt.dtype)[:, :, None]
    return (yp.reshape(x_prompt.shape), ys.reshape(x_sample.shape), states[0], states[1])
```

```python
import functools

import jax
import jax.numpy as jnp
from jax import lax
from jax.experimental import pallas as pl
from jax.experimental.pallas import tpu as pltpu

D_MODEL = 1024
RWKV_WIDTH = 512
HEAD_DIM = 64
N_HEADS = RWKV_WIDTH // HEAD_DIM
CONV_WIDTH = 512
CONV_K = 31
D_FF = 4 * D_MODEL
GRID_W = 64
LORA_W = 64
GATE_LORA = 128
RMS_EPS = 1e-6
LN_EPS = 1e-5
GN_EPS = 64e-5

LANES = 128
SUBLANES = 8
TM = 256
CHUNK = 64
N_CHUNK = TM // CHUNK
N_PAIR = RWKV_WIDTH // LANES
HALO = SUBLANES
CONV_PAD = 16
CONV_ROWS = 32
VMEM_LIMIT = 56 * 1024 * 1024

F32 = jnp.float32
BF16 = jnp.bfloat16
HIGHEST = lax.Precision.HIGHEST
DECAY_SCALE = 0.6065306597126334


def _mm(a, b):
    return jnp.dot(a.astype(BF16), b.astype(BF16), preferred_element_type=F32)


def _mm_nt(a, b):
    return lax.dot_general(a.astype(BF16), b.astype(BF16), (((1,), (1,)), ((), ())),
                           preferred_element_type=F32)


def _split3(x):
    x1 = x.astype(BF16)
    r1 = x - x1.astype(F32)
    x2 = r1.astype(BF16)
    x3 = (r1 - x2.astype(F32)).astype(BF16)
    return x3, x2, x1


def _mm_split3(m, x):
    x3, x2, x1 = (jnp.dot(m, t, preferred_element_type=F32) for t in _split3(x))
    return x3 + x2 + x1


def _mm_f32(a, b):
    return jnp.dot(a, b, precision=HIGHEST, preferred_element_type=F32)


def _head_sum(x, bd):
    rows = x.shape[0]
    n_grp = x.shape[1] // LANES
    terms = [t for p in range(n_grp) for t in _split3(x[:, p * LANES:(p + 1) * LANES])]
    prod = jnp.dot(jnp.concatenate(terms, axis=0), bd, preferred_element_type=F32)
    sums = [prod[(3 * p) * rows:(3 * p + 1) * rows] + prod[(3 * p + 1) * rows:(3 * p + 2) * rows]
            + prod[(3 * p + 2) * rows:(3 * p + 3) * rows] for p in range(n_grp)]
    return jnp.concatenate(sums, axis=1)


def _block_diag_ones():
    ri = lax.broadcasted_iota(jnp.int32, (LANES, LANES), 0)
    ci = lax.broadcasted_iota(jnp.int32, (LANES, LANES), 1)
    shift = HEAD_DIM.bit_length() - 1
    return jnp.where((ri >> shift) == (ci >> shift), 1.0, 0.0).astype(BF16)


def _rms(x):
    return x * lax.rsqrt(jnp.mean(x * x, axis=-1, keepdims=True) + RMS_EPS)


def _ada_kernel(c_ref, w_ref, b_ref, o_ref):
    c = c_ref[...]
    o_ref[...] = _mm_f32(c * jax.nn.sigmoid(c), w_ref[...]) + b_ref[...]


def _ada(cond, w, b):
    n = w.shape[1]
    tn = 1024
    return pl.pallas_call(
        _ada_kernel,
        grid=(n // tn,),
        in_specs=[pl.BlockSpec((8, D_MODEL), lambda i: (0, 0)),
                  pl.BlockSpec((D_MODEL, tn), lambda i: (0, i)),
                  pl.BlockSpec((1, tn), lambda i: (0, i))],
        out_specs=pl.BlockSpec((8, tn), lambda i: (0, i)),
        out_shape=jax.ShapeDtypeStruct((8, n), F32),
        compiler_params=pltpu.CompilerParams(dimension_semantics=("arbitrary",),
                                             vmem_limit_bytes=VMEM_LIMIT),
        name="ada",
    )(cond, w, b.reshape(1, n))


_IN_SPLIT = (3 * RWKV_WIDTH, 2 * CONV_WIDTH, 2 * D_MODEL)
_LORA_SPLIT = (4 * LORA_W, GATE_LORA)


def _front_kernel(x_ref, mod_ref, g_ref, win_ref, wlo_ref, *out_refs):
    mod = mod_ref[0]
    sh1 = mod[:, 0:D_MODEL]
    sc1 = mod[:, D_MODEL:2 * D_MODEL]
    xn = (_rms(x_ref[...]) * g_ref[...] * (1.0 + sc1) + sh1).astype(BF16)
    outs = iter(out_refs)
    for w_ref, split in ((win_ref, _IN_SPLIT), (wlo_ref, _LORA_SPLIT)):
        lo = 0
        for width in split:
            next(outs)[...] = jnp.dot(xn, w_ref[:, lo:lo + width], preferred_element_type=F32)
            lo += width


def _front(x, mod3, mod_row, g, w_in, w_lora):
    n = x.shape[0]
    const = lambda i: (0, 0)
    tok = lambda i: (i, 0)
    widths = _IN_SPLIT + _LORA_SPLIT
    return pl.pallas_call(
        _front_kernel,
        grid=(n // TM,),
        in_specs=[pl.BlockSpec((TM, D_MODEL), tok),
                  pl.BlockSpec((1, 1, 6 * D_MODEL), lambda i: (mod_row(i), 0, 0)),
                  pl.BlockSpec((1, D_MODEL), const),
                  pl.BlockSpec(w_in.shape, const),
                  pl.BlockSpec(w_lora.shape, const)],
        out_specs=[pl.BlockSpec((TM, w), tok) for w in widths],
        out_shape=[jax.ShapeDtypeStruct((n, w), F32) for w in widths],
        compiler_params=pltpu.CompilerParams(dimension_semantics=("arbitrary",),
                                             vmem_limit_bytes=VMEM_LIMIT),
        name="front",
    )(x, mod3, g, w_in, w_lora)


def _wkv_kernel(rkv_ref, prev_ref, next_ref, lora_ref, s0_ref, mup_ref, mun_ref, kk_ref, ka_ref,
                rk_ref, w0_ref, w2_ref, a0_ref, a2_ref,
                y_ref, bonus_ref, sout_ref,
                s_scr, r_scr, kd_scr, v_scr, aa_scr, b_scr, lw_scr, cl_scr,
                rhs_scr, out_scr, upd_scr, ti_scr, vs_scr, wc_scr, aab_scr, t_scr, *, nblk):
    d = pl.program_id(0)
    j = pl.program_id(2)
    sgn = 1 - 2 * d
    blk = j + d * (nblk - 1 - 2 * j)

    @pl.when(j == 0)
    def _():
        zero = jnp.zeros((HEAD_DIM, HEAD_DIM), F32)
        for p in range(N_PAIR):
            top = jnp.concatenate([s0_ref[0, 0, 2 * p], zero], axis=1)
            bot = jnp.concatenate([zero, s0_ref[0, 0, 2 * p + 1]], axis=1)
            s_scr[p] = jnp.transpose(jnp.concatenate([top, bot], axis=0))

    u = rkv_ref[...]
    row = lax.broadcasted_iota(jnp.int32, (TM, 1), 0)
    prow = jnp.where(blk > 0, prev_ref[HALO - 1:HALO, :], 0.0)
    nrow = jnp.where(blk < nblk - 1, next_ref[0:1, :], 0.0)
    prev = jnp.where(row == 0, prow, pltpu.roll(u, 1, axis=0))
    nxt = jnp.where(row == TM - 1, nrow, pltpu.roll(u, TM - 1, axis=0))
    x = u + (prev - u) * mup_ref[...] + (nxt - u) * mun_ref[...]
    r = x[:, 0:RWKV_WIDTH]
    k = x[:, RWKV_WIDTH:2 * RWKV_WIDTH]
    v = x[:, 2 * RWKV_WIDTH:]

    bd = _block_diag_ones()
    kkr = k * kk_ref[...]
    kk = kkr * lax.rsqrt(jnp.maximum(_head_sum(kkr * kkr, bd), 1e-24))
    lo = lora_ref[...]
    z = w0_ref[0] + _mm(jnp.tanh(lo[:, 0:2 * LORA_W]), w2_ref[0])
    lw_scr[...] = -DECAY_SCALE * jax.nn.sigmoid(z)
    a = jax.nn.sigmoid(a0_ref[0] + _mm(lo[:, 2 * LORA_W:4 * LORA_W], a2_ref[0]))
    kd = k * (1.0 + (a - 1.0) * ka_ref[...])
    bonus_ref[0] = _head_sum(r * kd * rk_ref[...], bd) * v
    r_scr[...] = r
    kd_scr[...] = kd
    v_scr[...] = v
    aa_scr[...] = -kk
    b_scr[...] = kk * a

    shift = CHUNK.bit_length() - 1
    rt = lax.broadcasted_iota(jnp.int32, (TM, TM), 0)
    ct = lax.broadcasted_iota(jnp.int32, (TM, TM), 1)
    cum_m = jnp.where(((rt >> shift) == (ct >> shift)) & (((ct - rt) * sgn) <= 0), 1.0, 0.0)
    cl_scr[...] = _mm_split3(cum_m.astype(BF16), lw_scr[...])

    ri = lax.broadcasted_iota(jnp.int32, (LANES, LANES), 0)
    ci = lax.broadcasted_iota(jnp.int32, (LANES, LANES), 1)
    same = (ri >> shift) == (ci >> shift)
    rel = (ci - ri) * sgn
    strict = same & (rel < 0)
    incl = same & (rel <= 0)
    eye = (ri == ci).astype(F32)
    levels = []
    for lv in range(shift):
        levels.append(((ri >> (lv + 1)) == (ci >> (lv + 1)))
                      & (((ri >> lv) & 1) == 1 - d) & (((ci >> lv) & 1) == d))
    head0 = lax.broadcasted_iota(jnp.int32, (CHUNK, LANES), 1) < HEAD_DIM
    dot = functools.partial(jnp.dot, preferred_element_type=F32)

    def stack(t):
        return jnp.concatenate([jnp.where(head0, t, 0.0), jnp.where(head0, 0.0, t)], axis=0)

    def chunk_rows(step):
        cidx = step + d * (N_CHUNK - 1 - 2 * step)
        return pl.ds(pl.multiple_of(cidx * CHUNK, CHUNK), CHUNK)

    def operands(step):
        rows = chunk_rows(step)
        for p in range(N_PAIR):
            i = step * N_PAIR + p
            sl = slice(p * LANES, (p + 1) * LANES)
            lw = lw_scr[rows, sl]
            cl = cl_scr[rows, sl]
            tot = jnp.sum(lw, axis=0, keepdims=True)
            e_out = jnp.exp(-cl)
            e_end = jnp.exp(tot - cl)
            b_c = b_scr[rows, sl]
            kd_c = kd_scr[rows, sl]
            a_s = stack(aa_scr[rows, sl] * jnp.exp(cl - lw)).astype(BF16)
            r_s = stack(r_scr[rows, sl] * jnp.exp(cl)).astype(BF16)
            bk_s = jnp.concatenate([stack(b_c * e_out), stack(kd_c * e_out)], axis=0).astype(BF16)
            vs_scr[i] = stack(v_scr[rows, sl]).astype(BF16)
            upd_scr[i] = jnp.concatenate([jnp.transpose(stack(b_c * e_end)),
                                          jnp.transpose(stack(kd_c * e_end))], axis=1).astype(BF16)
            wc_scr[i] = jnp.transpose(jnp.broadcast_to(jnp.exp(tot), (LANES, LANES)))
            ab = _mm_nt(jnp.concatenate([a_s, r_s], axis=0), bk_s)
            a_ab = jnp.where(strict, ab[0:LANES, 0:LANES], 0.0)
            a_ak = jnp.where(strict, ab[0:LANES, LANES:], 0.0)
            a_rb = jnp.where(incl, ab[LANES:, 0:LANES], 0.0)
            a_rk = jnp.where(incl, ab[LANES:, LANES:], 0.0)
            rhs_scr[i] = jnp.concatenate([a_s, a_ak.astype(BF16)], axis=1)
            out_scr[i] = jnp.concatenate([r_s, a_rb.astype(BF16), a_rk.astype(BF16)], axis=1)
            aab_scr[i] = a_ab
            t_scr[i] = eye + jnp.where(levels[0], a_ab, 0.0)
            yield

    def invert(step):
        idx = [step * N_PAIR + p for p in range(N_PAIR)]
        for lvl in levels[1:]:
            tbs = [t_scr[i].astype(BF16) for i in idx]
            ms = [dot(jnp.where(lvl, aab_scr[i], 0.0).astype(BF16), tb) for i, tb in zip(idx, tbs)]
            yield
            for i, tb, m in zip(idx, tbs, ms):
                t_scr[i] = t_scr[i] + dot(tb, m.astype(BF16))
            yield
        for i in idx:
            t = t_scr[i]
            th = t.astype(BF16)
            ti_scr[i, 0] = th
            ti_scr[i, 1] = (t - th.astype(F32)).astype(BF16)

    def scan(step):
        rows = chunk_rows(step)
        base = step * N_PAIR
        pairs = range(N_PAIR)
        sts = [s_scr[p] for p in pairs]
        stb = [st.astype(BF16) for st in sts]
        vss = [vs_scr[base + p] for p in pairs]
        rhs = [dot(rhs_scr[base + p], jnp.concatenate([stb[p], vss[p]], axis=0)) for p in pairs]
        yield
        pmb = []
        for p in pairs:
            rh = rhs[p].astype(BF16)
            rl = (rhs[p] - rh.astype(F32)).astype(BF16)
            th = ti_scr[base + p, 0]
            pm = dot(th, rh) + dot(th, rl) + dot(ti_scr[base + p, 1], rh)
            pmb.append(pm.astype(BF16))
        yield
        for p in pairs:
            s_scr[p] = wc_scr[base + p] * sts[p] + dot(
                upd_scr[base + p], jnp.concatenate([pmb[p], vss[p]], axis=0))
        yield
        for p in pairs:
            y_s = dot(out_scr[base + p], jnp.concatenate([stb[p], pmb[p], vss[p]], axis=0))
            y_ref[0, rows, p * LANES:(p + 1) * LANES] = y_s[0:CHUNK] + y_s[CHUNK:]

    for slot in range(N_CHUNK + 2):
        streams = []
        if 0 <= slot - 1 < N_CHUNK:
            streams.append(invert(slot - 1))
        if slot < N_CHUNK:
            streams.append(operands(slot))
        if 0 <= slot - 2 < N_CHUNK:
            streams.append(scan(slot - 2))
        while streams:
            streams = [g for g in streams if next(g, True) is None]

    @pl.when(j == nblk - 1)
    def _():
        for p in range(N_PAIR):
            s_vk = jnp.transpose(s_scr[p])
            sout_ref[0, 0, 2 * p] = s_vk[0:HEAD_DIM, 0:HEAD_DIM]
            sout_ref[0, 0, 2 * p + 1] = s_vk[HEAD_DIM:, HEAD_DIM:]


def _wkv(rkv, lora, s0, mup, mun, k_k, k_a, r_k, w0, w2p, a0, a2p, *, nseq, nblk):
    n = rkv.shape[0]
    hb = TM // HALO

    def tile(d, s, j):
        return s * nblk + j + d * (nblk - 1 - 2 * j)

    s0_seq = (lambda s: s) if s0.shape[1] > 1 else (lambda s: 0)
    const = lambda d, s, j: (0, 0)
    per_dir = lambda d, s, j: (d, 0, 0)
    kern = functools.partial(_wkv_kernel, nblk=nblk)
    tok = pltpu.VMEM((TM, RWKV_WIDTH), F32)
    n_prob = N_CHUNK * N_PAIR
    return pl.pallas_call(
        kern,
        grid=(2, nseq, nblk),
        in_specs=[
            pl.BlockSpec((TM, 3 * RWKV_WIDTH), lambda d, s, j: (tile(d, s, j), 0)),
            pl.BlockSpec((HALO, 3 * RWKV_WIDTH),
                         lambda d, s, j: (jnp.maximum(tile(d, s, j) * hb - 1, 0), 0)),
            pl.BlockSpec((HALO, 3 * RWKV_WIDTH),
                         lambda d, s, j: (jnp.minimum((tile(d, s, j) + 1) * hb, n // HALO - 1), 0)),
            pl.BlockSpec((TM, 4 * LORA_W), lambda d, s, j: (tile(d, s, j), 0)),
            pl.BlockSpec((1, 1, N_HEADS, HEAD_DIM, HEAD_DIM), lambda d, s, j: (d, s0_seq(s), 0, 0, 0)),
            pl.BlockSpec((1, 3 * RWKV_WIDTH), const),
            pl.BlockSpec((1, 3 * RWKV_WIDTH), const),
            pl.BlockSpec((1, RWKV_WIDTH), const),
            pl.BlockSpec((1, RWKV_WIDTH), const),
            pl.BlockSpec((1, RWKV_WIDTH), const),
            pl.BlockSpec((1, 1, RWKV_WIDTH), per_dir),
            pl.BlockSpec((1, 2 * LORA_W, RWKV_WIDTH), per_dir),
            pl.BlockSpec((1, 1, RWKV_WIDTH), per_dir),
            pl.BlockSpec((1, 2 * LORA_W, RWKV_WIDTH), per_dir),
        ],
        out_specs=[
            pl.BlockSpec((1, TM, RWKV_WIDTH), lambda d, s, j: (d, tile(d, s, j), 0)),
            pl.BlockSpec((1, TM, RWKV_WIDTH), lambda d, s, j: (d, tile(d, s, j), 0)),
            pl.BlockSpec((1, 1, N_HEADS, HEAD_DIM, HEAD_DIM), lambda d, s, j: (d, s, 0, 0, 0)),
        ],
        out_shape=[jax.ShapeDtypeStruct((2, n, RWKV_WIDTH), F32),
                   jax.ShapeDtypeStruct((2, n, RWKV_WIDTH), F32),
                   jax.ShapeDtypeStruct((2, nseq, N_HEADS, HEAD_DIM, HEAD_DIM), F32)],
        scratch_shapes=[pltpu.VMEM((N_PAIR, LANES, LANES), F32), tok, tok, tok, tok, tok, tok, tok,
                        pltpu.VMEM((n_prob, LANES, 2 * LANES), BF16),
                        pltpu.VMEM((n_prob, LANES, 3 * LANES), BF16),
                        pltpu.VMEM((n_prob, LANES, 2 * LANES), BF16),
                        pltpu.VMEM((n_prob, 2, LANES, LANES), BF16),
                        pltpu.VMEM((n_prob, LANES, LANES), BF16),
                        pltpu.VMEM((n_prob, LANES, LANES), F32),
                        pltpu.VMEM((n_prob, LANES, LANES), F32),
                        pltpu.VMEM((n_prob, LANES, LANES), F32)],
        compiler_params=pltpu.CompilerParams(
            dimension_semantics=("arbitrary", "arbitrary", "arbitrary"),
            vmem_limit_bytes=VMEM_LIMIT),
        name="wkv",
    )(rkv, rkv, rkv, lora, s0, mup, mun, k_k, k_a, r_k, w0, w2p, a0, a2p)


def _mid_kernel(y_ref, bn_ref, lg_ref, glu_ref, gates_ref, x_ref, mod_ref, g2w_ref, lnxg_ref,
                lnxb_ref, wor_ref, cw_ref, cb_ref, clg_ref, clb_ref, woc_ref, wo_ref, n2g_ref,
                x1_ref, xn2_ref, pad_scr, *, seq):
    mod = mod_ref[0]
    g1 = mod[:, 2 * D_MODEL:3 * D_MODEL]
    sh2 = mod[:, 3 * D_MODEL:4 * D_MODEL]
    sc2 = mod[:, 4 * D_MODEL:5 * D_MODEL]

    bd = _block_diag_ones()
    y = y_ref[0] + y_ref[1]
    yc = y - _head_sum(y, bd) * (1.0 / HEAD_DIM)
    var = _head_sum(yc * yc, bd) * (1.0 / HEAD_DIM)
    yn = yc * lax.rsqrt(var + GN_EPS) * lnxg_ref[...] + lnxb_ref[...]
    g = _mm(jax.nn.sigmoid(lg_ref[...]), g2w_ref[...])
    y_r = _mm((yn + bn_ref[0] + bn_ref[1]) * g, wor_ref[...])

    glu = glu_ref[...]
    uu = glu[:, 0:CONV_WIDTH] * jax.nn.sigmoid(glu[:, CONV_WIDTH:])
    nseq = TM // seq
    zpad = jnp.zeros((nseq, CONV_PAD, CONV_WIDTH), F32)
    pad_scr[0, :, 0:CONV_PAD, :] = zpad
    pad_scr[0, :, CONV_PAD + seq:, :] = zpad
    for s in range(nseq):
        pad_scr[0, s, CONV_PAD:CONV_PAD + seq, :] = uu[s * seq:(s + 1) * seq]
    live = seq + 2 * CONV_PAD - SUBLANES
    for m in range(1, SUBLANES):
        pad_scr[m, :, 0:live, :] = pad_scr[0, :, m:m + live, :]
    first = CONV_PAD - CONV_K // 2
    parts = []
    for s in range(nseq):
        for r0 in range(0, seq, CONV_ROWS):
            acc = jnp.broadcast_to(cb_ref[...], (CONV_ROWS, CONV_WIDTH))
            for t in range(CONV_K):
                m = (first + t) % SUBLANES
                lo = r0 + first + t - m
                acc = acc + pad_scr[m, s, lo:lo + CONV_ROWS, :] * cw_ref[t:t + 1, :]
            parts.append(acc)
    cv = jnp.concatenate(parts, axis=0)
    cm = cv - jnp.mean(cv, axis=-1, keepdims=True)
    cvar = jnp.mean(cm * cm, axis=-1, keepdims=True)
    un = cm * lax.rsqrt(cvar + LN_EPS) * clg_ref[...] + clb_ref[...]
    y_c = _mm(un * jax.nn.sigmoid(un), woc_ref[...])

    gates = gates_ref[...]
    merged = jax.nn.sigmoid(gates[:, 0:D_MODEL]) * y_r + jax.nn.sigmoid(gates[:, D_MODEL:]) * y_c
    x1 = x_ref[...] + g1 * _mm(merged, wo_ref[...])
    x1_ref[...] = x1
    xn2_ref[...] = (_rms(x1) * n2g_ref[...] * (1.0 + sc2) + sh2).astype(BF16)


def _mid(y2, bn2, lg, glu, gates, x, mod3, mod_row, g2w, lnxg, lnxb, wor, cw, cb, clg, clb, woc,
         wo, n2g, *, seq):
    n = x.shape[0]
    const = lambda i: (0, 0)
    tok = lambda i: (i, 0)
    both = lambda i: (0, i, 0)

    def full(a):
        return pl.BlockSpec(a.shape, const)

    return pl.pallas_call(
        functools.partial(_mid_kernel, seq=seq),
        grid=(n // TM,),
        in_specs=[pl.BlockSpec((2, TM, RWKV_WIDTH), both),
                  pl.BlockSpec((2, TM, RWKV_WIDTH), both),
                  pl.BlockSpec((TM, GATE_LORA), tok),
                  pl.BlockSpec((TM, 2 * CONV_WIDTH), tok),
                  pl.BlockSpec((TM, 2 * D_MODEL), tok),
                  pl.BlockSpec((TM, D_MODEL), tok),
                  pl.BlockSpec((1, 1, 6 * D_MODEL), lambda i: (mod_row(i), 0, 0)),
                  full(g2w), full(lnxg), full(lnxb), full(wor), full(cw), full(cb), full(clg),
                  full(clb), full(woc), full(wo), full(n2g)],
        out_specs=[pl.BlockSpec((TM, D_MODEL), tok), pl.BlockSpec((TM, D_MODEL), tok)],
        out_shape=[jax.ShapeDtypeStruct((n, D_MODEL), F32),
                   jax.ShapeDtypeStruct((n, D_MODEL), BF16)],
        scratch_shapes=[pltpu.VMEM((SUBLANES, TM // seq, seq + 2 * CONV_PAD, CONV_WIDTH), F32)],
        compiler_params=pltpu.CompilerParams(dimension_semantics=("arbitrary",),
                                             vmem_limit_bytes=VMEM_LIMIT),
        name="mid",
    )(y2, bn2, lg, glu, gates, x, mod3, g2w, lnxg, lnxb, wor, cw, cb, clg, clb, woc, wo, n2g)


def _mlp_kernel(x1_ref, xn2_ref, mod_ref, w1_ref, w2_ref, fg_ref, o_ref):
    g2 = mod_ref[0][:, 5 * D_MODEL:6 * D_MODEL]
    h = jnp.maximum(jnp.dot(xn2_ref[...], w1_ref[...], preferred_element_type=F32), 0.0)
    x2 = x1_ref[...] + g2 * _mm(h * h, w2_ref[...])
    o_ref[...] = _rms(x2) * fg_ref[...]


def _mlp(x1, xn2, mod3, mod_row, w1, w2, fg):
    n = x1.shape[0]
    const = lambda i: (0, 0)
    tok = lambda i: (i, 0)
    return pl.pallas_call(
        _mlp_kernel,
        grid=(n // TM,),
        in_specs=[pl.BlockSpec((TM, D_MODEL), tok),
                  pl.BlockSpec((TM, D_MODEL), tok),
                  pl.BlockSpec((1, 1, 6 * D_MODEL), lambda i: (mod_row(i), 0, 0)),
                  pl.BlockSpec(w1.shape, const),
                  pl.BlockSpec(w2.shape, const),
                  pl.BlockSpec((1, D_MODEL), const)],
        out_specs=pl.BlockSpec((TM, D_MODEL), tok),
        out_shape=jax.ShapeDtypeStruct((n, D_MODEL), F32),
        compiler_params=pltpu.CompilerParams(dimension_semantics=("arbitrary",),
                                             vmem_limit_bytes=VMEM_LIMIT),
        name="mlp",
    )(x1, xn2, mod3, w1, w2, fg)


def _pad_dir(w):
    return (jnp.eye(2, dtype=w.dtype)[:, :, None, None] * w[:, None]).reshape(2, 2 * w.shape[1], w.shape[2])


def kernel(x_prompt, x_sample, state_fwd, state_bwd, c, c_ctx, ada_w, ada_b, norm1_g, norm2_g, w_in,
           mu_prev, mu_next, decay_w0, decay_w1, decay_w2, iclr_a0, iclr_a1, iclr_a2, gate_g1, gate_g2,
           k_k, k_a, r_k, lnx_g, lnx_b, w_out_rwkv, conv_w, conv_b, conv_ln_g, conv_ln_b, w_out_conv,
           w_o, mlp_w1, mlp_w2, final_g):
    n_ctx, seq_ctx, _ = x_prompt.shape
    n_lat, seq_lat, _ = x_sample.shape
    depth = ada_w.shape[0]
    assert seq_ctx == TM and seq_lat % TM == 0 and GRID_W == CHUNK and c.shape[0] + 1 <= 8
    lat_blk = seq_lat // TM
    row_ctx = lambda i: 0
    row_lat = lambda i: 1 + i // lat_blk

    xp = x_prompt.reshape(n_ctx * seq_ctx, D_MODEL)
    xs = x_sample.reshape(n_lat * seq_lat, D_MODEL)
    cond = jnp.concatenate([c_ctx[None, :], c, jnp.zeros((7 - c.shape[0], D_MODEL), F32)], axis=0)
    zero_state = jnp.zeros((2, 1, N_HEADS, HEAD_DIM, HEAD_DIM), F32)
    row = lambda a: a.reshape(1, -1)
    assert depth == 1
    lay = lambda a: a.reshape(a.shape[1:])
    bf = lambda a: lay(a).astype(BF16)

    mod3 = _ada(cond, lay(ada_w), lay(ada_b)).reshape(8, 1, 6 * D_MODEL)
    w_lora = jnp.concatenate([decay_w1[0, 0], decay_w1[0, 1], iclr_a1[0, 0], iclr_a1[0, 1], gate_g1[0]],
                             axis=1).astype(BF16)
    front_w = (norm1_g, bf(w_in), w_lora)
    wkv_w = (mu_prev, mu_next, k_k, k_a, row(r_k),
             decay_w0.reshape(2, 1, RWKV_WIDTH), _pad_dir(lay(decay_w2)).astype(BF16),
             iclr_a0.reshape(2, 1, RWKV_WIDTH), _pad_dir(lay(iclr_a2)).astype(BF16))
    mid_w = (bf(gate_g2), lnx_g, lnx_b, bf(w_out_rwkv), lay(conv_w), conv_b, conv_ln_g, conv_ln_b,
             bf(w_out_conv), bf(w_o), norm2_g)
    mlp_w = (bf(mlp_w1), bf(mlp_w2), row(final_g))
    s0_lat = jnp.stack([state_fwd[:, 0], state_bwd[:, 0]]).astype(F32)

    def layer(x, mod_row, s0, nseq, nblk, conv_seq):
        rkv, glu, gates, lora, lg = _front(x, mod3, mod_row, *front_w)
        y2, bn2, s_out = _wkv(rkv, lora, s0, *wkv_w, nseq=nseq, nblk=nblk)
        x1, xn2 = _mid(y2, bn2, lg, glu, gates, x, mod3, mod_row, *mid_w, seq=conv_seq)
        return _mlp(x1, xn2, mod3, mod_row, *mlp_w), s_out

    yp, s_ctx = layer(xp, row_ctx, zero_state, n_ctx, 1, seq_ctx)
    ys, _ = layer(xs, row_lat, s0_lat, n_lat, lat_blk, GRID_W)
    states = s_ctx.astype(x_prompt.dtype)[:, :, None]
    return (yp.reshape(x_prompt.shape), ys.reshape(x_sample.shape), states[0], states[1])
```

```python
import functools
import itertools

import jax
import jax.numpy as jnp
from jax import lax
from jax.experimental import pallas as pl
from jax.experimental.pallas import tpu as pltpu

D_MODEL = 1024
RWKV_WIDTH = 512
HEAD_DIM = 64
N_HEADS = RWKV_WIDTH // HEAD_DIM
CONV_WIDTH = 512
CONV_K = 31
D_FF = 4 * D_MODEL
GRID_W = 64
LORA_W = 64
GATE_LORA = 128
RMS_EPS = 1e-6
LN_EPS = 1e-5
GN_EPS = 64e-5

LANES = 128
SUBLANES = 8
TM = 256
CHUNK = 64
N_CHUNK = TM // CHUNK
N_PAIR = RWKV_WIDTH // LANES
HALO = SUBLANES
CONV_PAD = 16
CONV_ROWS = 32
VMEM_LIMIT = 56 * 1024 * 1024

F32 = jnp.float32
BF16 = jnp.bfloat16
HIGHEST = lax.Precision.HIGHEST
DECAY_SCALE = 0.6065306597126334


def _mm(a, b):
    return jnp.dot(a.astype(BF16), b.astype(BF16), preferred_element_type=F32)


def _mm_nt(a, b):
    return lax.dot_general(a.astype(BF16), b.astype(BF16), (((1,), (1,)), ((), ())),
                           preferred_element_type=F32)


def _split3(x):
    x1 = x.astype(BF16)
    r1 = x - x1.astype(F32)
    x2 = r1.astype(BF16)
    x3 = (r1 - x2.astype(F32)).astype(BF16)
    return x3, x2, x1


def _mm_split3(m, x):
    x3, x2, x1 = (jnp.dot(m, t, preferred_element_type=F32) for t in _split3(x))
    return x3 + x2 + x1


def _mm_f32(a, b):
    return jnp.dot(a, b, precision=HIGHEST, preferred_element_type=F32)


def _head_sum(x, bd):
    rows = x.shape[0]
    n_grp = x.shape[1] // LANES
    terms = [t for p in range(n_grp) for t in _split3(x[:, p * LANES:(p + 1) * LANES])]
    prod = jnp.dot(jnp.concatenate(terms, axis=0), bd, preferred_element_type=F32)
    sums = [prod[(3 * p) * rows:(3 * p + 1) * rows] + prod[(3 * p + 1) * rows:(3 * p + 2) * rows]
            + prod[(3 * p + 2) * rows:(3 * p + 3) * rows] for p in range(n_grp)]
    return jnp.concatenate(sums, axis=1)


def _block_diag_ones():
    ri = lax.broadcasted_iota(jnp.int32, (LANES, LANES), 0)
    ci = lax.broadcasted_iota(jnp.int32, (LANES, LANES), 1)
    shift = HEAD_DIM.bit_length() - 1
    return jnp.where((ri >> shift) == (ci >> shift), 1.0, 0.0).astype(BF16)


def _rms(x):
    return x * lax.rsqrt(jnp.mean(x * x, axis=-1, keepdims=True) + RMS_EPS)


def _ada_kernel(c_ref, w_ref, b_ref, o_ref):
    c = c_ref[...]
    o_ref[...] = _mm_f32(c * jax.nn.sigmoid(c), w_ref[...]) + b_ref[...]


def _ada(cond, w, b):
    n = w.shape[1]
    tn = 1024
    return pl.pallas_call(
        _ada_kernel,
        grid=(n // tn,),
        in_specs=[pl.BlockSpec((8, D_MODEL), lambda i: (0, 0)),
                  pl.BlockSpec((D_MODEL, tn), lambda i: (0, i)),
                  pl.BlockSpec((1, tn), lambda i: (0, i))],
        out_specs=pl.BlockSpec((8, tn), lambda i: (0, i)),
        out_shape=jax.ShapeDtypeStruct((8, n), F32),
        compiler_params=pltpu.CompilerParams(dimension_semantics=("arbitrary",),
                                             vmem_limit_bytes=VMEM_LIMIT),
        name="ada",
    )(cond, w, b.reshape(1, n))


_IN_SPLIT = (3 * RWKV_WIDTH, 2 * CONV_WIDTH, 2 * D_MODEL)
_LORA_SPLIT = (4 * LORA_W, GATE_LORA)


def _front_kernel(x_ref, mod_ref, g_ref, win_ref, wlo_ref, *out_refs):
    mod = mod_ref[0]
    sh1 = mod[:, 0:D_MODEL]
    sc1 = mod[:, D_MODEL:2 * D_MODEL]
    xn = (_rms(x_ref[...]) * g_ref[...] * (1.0 + sc1) + sh1).astype(BF16)
    outs = iter(out_refs)
    for w_ref, split in ((win_ref, _IN_SPLIT), (wlo_ref, _LORA_SPLIT)):
        lo = 0
        for width in split:
            next(outs)[...] = jnp.dot(xn, w_ref[:, lo:lo + width], preferred_element_type=F32)
            lo += width


def _front(x, mod3, mod_row, g, w_in, w_lora):
    n = x.shape[0]
    const = lambda i: (0, 0)
    tok = lambda i: (i, 0)
    widths = _IN_SPLIT + _LORA_SPLIT
    return pl.pallas_call(
        _front_kernel,
        grid=(n // TM,),
        in_specs=[pl.BlockSpec((TM, D_MODEL), tok),
                  pl.BlockSpec((1, 1, 6 * D_MODEL), lambda i: (mod_row(i), 0, 0)),
                  pl.BlockSpec((1, D_MODEL), const),
                  pl.BlockSpec(w_in.shape, const),
                  pl.BlockSpec(w_lora.shape, const)],
        out_specs=[pl.BlockSpec((TM, w), tok) for w in widths],
        out_shape=[jax.ShapeDtypeStruct((n, w), F32) for w in widths],
        compiler_params=pltpu.CompilerParams(dimension_semantics=("arbitrary",),
                                             vmem_limit_bytes=VMEM_LIMIT),
        name="front",
    )(x, mod3, g, w_in, w_lora)


def _wkv_kernel(rkv_ref, prev_ref, next_ref, lora_ref, s0_ref, mup_ref, mun_ref, kk_ref, ka_ref,
                rk_ref, w0_ref, w2_ref, a0_ref, a2_ref,
                y_ref, bonus_ref, sout_ref,
                s_scr, r_scr, kd_scr, v_scr, aa_scr, b_scr, lw_scr, cl_scr,
                rhs_scr, out_scr, upd_scr, ti_scr, vs_scr, wc_scr, aab_scr, t_scr, *, nblk):
    d = pl.program_id(0)
    j = pl.program_id(2)
    sgn = 1 - 2 * d
    blk = j + d * (nblk - 1 - 2 * j)

    @pl.when(j == 0)
    def _():
        zero = jnp.zeros((HEAD_DIM, HEAD_DIM), F32)
        for p in range(N_PAIR):
            top = jnp.concatenate([s0_ref[0, 0, 2 * p], zero], axis=1)
            bot = jnp.concatenate([zero, s0_ref[0, 0, 2 * p + 1]], axis=1)
            s_scr[p] = jnp.transpose(jnp.concatenate([top, bot], axis=0))

    u = rkv_ref[...]
    row = lax.broadcasted_iota(jnp.int32, (TM, 1), 0)
    prow = jnp.where(blk > 0, prev_ref[HALO - 1:HALO, :], 0.0)
    nrow = jnp.where(blk < nblk - 1, next_ref[0:1, :], 0.0)
    prev = jnp.where(row == 0, prow, pltpu.roll(u, 1, axis=0))
    nxt = jnp.where(row == TM - 1, nrow, pltpu.roll(u, TM - 1, axis=0))
    x = u + (prev - u) * mup_ref[...] + (nxt - u) * mun_ref[...]
    r = x[:, 0:RWKV_WIDTH]
    k = x[:, RWKV_WIDTH:2 * RWKV_WIDTH]
    v = x[:, 2 * RWKV_WIDTH:]

    bd = _block_diag_ones()
    kkr = k * kk_ref[...]
    kk = kkr * lax.rsqrt(jnp.maximum(_head_sum(kkr * kkr, bd), 1e-24))
    lo = lora_ref[...]
    z = w0_ref[0] + _mm(jnp.tanh(lo[:, 0:2 * LORA_W]), w2_ref[0])
    lw_scr[0:TM, :] = -DECAY_SCALE * jax.nn.sigmoid(z)
    a = jax.nn.sigmoid(a0_ref[0] + _mm(lo[:, 2 * LORA_W:4 * LORA_W], a2_ref[0]))
    kd = k * (1.0 + (a - 1.0) * ka_ref[...])
    bonus_ref[0] = _head_sum(r * kd * rk_ref[...], bd) * v
    r_scr[...] = r
    kd_scr[...] = kd
    v_scr[...] = v
    aa_scr[...] = -kk
    b_scr[...] = kk * a

    shift = CHUNK.bit_length() - 1
    rt = lax.broadcasted_iota(jnp.int32, (TM, TM), 0)
    ct = lax.broadcasted_iota(jnp.int32, (TM, TM), 1)
    cum_m = jnp.where(((rt >> shift) == (ct >> shift)) & (((ct - rt) * sgn) <= 0), 1.0, 0.0)
    cl_scr[...] = _mm_split3(cum_m.astype(BF16), lw_scr[0:TM, :])

    ri = lax.broadcasted_iota(jnp.int32, (LANES, LANES), 0)
    ci = lax.broadcasted_iota(jnp.int32, (LANES, LANES), 1)
    same = (ri >> shift) == (ci >> shift)
    rel = (ci - ri) * sgn
    strict = same & (rel < 0)
    incl = same & (rel <= 0)
    eye = (ri == ci).astype(F32)
    levels = []
    for lv in range(shift):
        levels.append(((ri >> (lv + 1)) == (ci >> (lv + 1)))
                      & (((ri >> lv) & 1) == 1 - d) & (((ci >> lv) & 1) == d))
    head0 = lax.broadcasted_iota(jnp.int32, (CHUNK, LANES), 1) < HEAD_DIM
    dot = functools.partial(jnp.dot, preferred_element_type=F32)

    def stack(t):
        return jnp.concatenate([jnp.where(head0, t, 0.0), jnp.where(head0, 0.0, t)], axis=0)

    def chunk_rows(step):
        cidx = step + d * (N_CHUNK - 1 - 2 * step)
        return pl.ds(pl.multiple_of(cidx * CHUNK, CHUNK), CHUNK)

    def operands(step):
        rows = chunk_rows(step)
        for p in range(N_PAIR):
            i = step * N_PAIR + p
            sl = slice(p * LANES, (p + 1) * LANES)
            lw = lw_scr[rows, sl]
            cl = cl_scr[rows, sl]
            tot = jnp.sum(lw, axis=0, keepdims=True)
            e_out = jnp.exp(-cl)
            e_end = jnp.exp(tot - cl)
            b_c = b_scr[rows, sl]
            kd_c = kd_scr[rows, sl]
            a_s = stack(aa_scr[rows, sl] * jnp.exp(cl - lw)).astype(BF16)
            r_s = stack(r_scr[rows, sl] * jnp.exp(cl)).astype(BF16)
            bk_s = jnp.concatenate([stack(b_c * e_out), stack(kd_c * e_out)], axis=0).astype(BF16)
            vs_scr[i] = stack(v_scr[rows, sl]).astype(BF16)
            upd_scr[i] = jnp.concatenate([jnp.transpose(stack(b_c * e_end)),
                                          jnp.transpose(stack(kd_c * e_end))], axis=1).astype(BF16)
            wc_scr[i] = jnp.transpose(jnp.broadcast_to(jnp.exp(tot), (LANES, LANES)))
            ab = _mm_nt(jnp.concatenate([a_s, r_s], axis=0), bk_s)
            a_ab = jnp.where(strict, ab[0:LANES, 0:LANES], 0.0)
            a_ak = jnp.where(strict, ab[0:LANES, LANES:], 0.0)
            a_rb = jnp.where(incl, ab[LANES:, 0:LANES], 0.0)
            a_rk = jnp.where(incl, ab[LANES:, LANES:], 0.0)
            rhs_scr[i] = jnp.concatenate([a_s, a_ak.astype(BF16)], axis=1)
            out_scr[i] = jnp.concatenate([r_s, a_rb.astype(BF16), a_rk.astype(BF16)], axis=1)
            aab_scr[i] = a_ab
            t_scr[i] = eye + jnp.where(levels[0], a_ab, 0.0)
            yield

    def invert(steps):
        idx = [step * N_PAIR + p for step in steps for p in range(N_PAIR)]
        for lvl in levels[1:]:
            tbs = [t_scr[i].astype(BF16) for i in idx]
            ms = [dot(jnp.where(lvl, aab_scr[i], 0.0).astype(BF16), tb) for i, tb in zip(idx, tbs)]
            lw_scr[TM:, 0:LANES] = ms[-1][0:SUBLANES, :]
            yield
            for i, tb, m in zip(idx, tbs, ms):
                t_scr[i] = t_scr[i] + dot(tb, m.astype(BF16))
            lw_scr[TM:, 0:LANES] = ms[0][0:SUBLANES, :]
            yield
        for i in idx:
            t = t_scr[i]
            th = t.astype(BF16)
            ti_scr[i, 0] = th
            ti_scr[i, 1] = (t - th.astype(F32)).astype(BF16)

    def scan(step):
        rows = chunk_rows(step)
        base = step * N_PAIR
        pairs = range(N_PAIR)
        sts = [s_scr[p] for p in pairs]
        stb = [st.astype(BF16) for st in sts]
        vss = [vs_scr[base + p] for p in pairs]
        rhs = [dot(rhs_scr[base + p], jnp.concatenate([stb[p], vss[p]], axis=0)) for p in pairs]
        yield
        pmb = []
        for p in pairs:
            rh = rhs[p].astype(BF16)
            rl = (rhs[p] - rh.astype(F32)).astype(BF16)
            th = ti_scr[base + p, 0]
            pm = dot(th, rh) + dot(th, rl) + dot(ti_scr[base + p, 1], rh)
            pmb.append(pm.astype(BF16))
        yield
        for p in pairs:
            s_scr[p] = wc_scr[base + p] * sts[p] + dot(
                upd_scr[base + p], jnp.concatenate([pmb[p], vss[p]], axis=0))
        yield
        for p in pairs:
            y_s = dot(out_scr[base + p], jnp.concatenate([stb[p], pmb[p], vss[p]], axis=0))
            y_ref[0, rows, p * LANES:(p + 1) * LANES] = y_s[0:CHUNK] + y_s[CHUNK:]

    def run(*streams):
        streams = list(streams)
        while streams:
            streams = [g for g in streams if next(g, True) is None]

    half = N_CHUNK // 2
    first, second = range(0, half), range(half, N_CHUNK)
    run(itertools.chain(*(operands(s) for s in first)))
    run(invert(first), itertools.chain(*(operands(s) for s in second)))
    run(invert(second), itertools.chain(*(scan(s) for s in first)))
    run(itertools.chain(*(scan(s) for s in second)))

    @pl.when(j == nblk - 1)
    def _():
        for p in range(N_PAIR):
            s_vk = jnp.transpose(s_scr[p])
            sout_ref[0, 0, 2 * p] = s_vk[0:HEAD_DIM, 0:HEAD_DIM]
            sout_ref[0, 0, 2 * p + 1] = s_vk[HEAD_DIM:, HEAD_DIM:]


def _wkv(rkv, lora, s0, mup, mun, k_k, k_a, r_k, w0, w2p, a0, a2p, *, nseq, nblk):
    n = rkv.shape[0]
    hb = TM // HALO

    def tile(d, s, j):
        return s * nblk + j + d * (nblk - 1 - 2 * j)

    s0_seq = (lambda s: s) if s0.shape[1] > 1 else (lambda s: 0)
    const = lambda d, s, j: (0, 0)
    per_dir = lambda d, s, j: (d, 0, 0)
    kern = functools.partial(_wkv_kernel, nblk=nblk)
    tok = pltpu.VMEM((TM, RWKV_WIDTH), F32)
    n_prob = N_CHUNK * N_PAIR
    return pl.pallas_call(
        kern,
        grid=(2, nseq, nblk),
        in_specs=[
            pl.BlockSpec((TM, 3 * RWKV_WIDTH), lambda d, s, j: (tile(d, s, j), 0)),
            pl.BlockSpec((HALO, 3 * RWKV_WIDTH),
                         lambda d, s, j: (jnp.maximum(tile(d, s, j) * hb - 1, 0), 0)),
            pl.BlockSpec((HALO, 3 * RWKV_WIDTH),
                         lambda d, s, j: (jnp.minimum((tile(d, s, j) + 1) * hb, n // HALO - 1), 0)),
            pl.BlockSpec((TM, 4 * LORA_W), lambda d, s, j: (tile(d, s, j), 0)),
            pl.BlockSpec((1, 1, N_HEADS, HEAD_DIM, HEAD_DIM), lambda d, s, j: (d, s0_seq(s), 0, 0, 0)),
            pl.BlockSpec((1, 3 * RWKV_WIDTH), const),
            pl.BlockSpec((1, 3 * RWKV_WIDTH), const),
            pl.BlockSpec((1, RWKV_WIDTH), const),
            pl.BlockSpec((1, RWKV_WIDTH), const),
            pl.BlockSpec((1, RWKV_WIDTH), const),
            pl.BlockSpec((1, 1, RWKV_WIDTH), per_dir),
            pl.BlockSpec((1, 2 * LORA_W, RWKV_WIDTH), per_dir),
            pl.BlockSpec((1, 1, RWKV_WIDTH), per_dir),
            pl.BlockSpec((1, 2 * LORA_W, RWKV_WIDTH), per_dir),
        ],
        out_specs=[
            pl.BlockSpec((1, TM, RWKV_WIDTH), lambda d, s, j: (d, tile(d, s, j), 0)),
            pl.BlockSpec((1, TM, RWKV_WIDTH), lambda d, s, j: (d, tile(d, s, j), 0)),
            pl.BlockSpec((1, 1, N_HEADS, HEAD_DIM, HEAD_DIM), lambda d, s, j: (d, s, 0, 0, 0)),
        ],
        out_shape=[jax.ShapeDtypeStruct((2, n, RWKV_WIDTH), F32),
                   jax.ShapeDtypeStruct((2, n, RWKV_WIDTH), F32),
                   jax.ShapeDtypeStruct((2, nseq, N_HEADS, HEAD_DIM, HEAD_DIM), F32)],
        scratch_shapes=[pltpu.VMEM((N_PAIR, LANES, LANES), F32), tok, tok, tok, tok, tok,
                        pltpu.VMEM((TM + SUBLANES, RWKV_WIDTH), F32),
                        tok,
                        pltpu.VMEM((n_prob, LANES, 2 * LANES), BF16),
                        pltpu.VMEM((n_prob, LANES, 3 * LANES), BF16),
                        pltpu.VMEM((n_prob, LANES, 2 * LANES), BF16),
                        pltpu.VMEM((n_prob, 2, LANES, LANES), BF16),
                        pltpu.VMEM((n_prob, LANES, LANES), BF16),
                        pltpu.VMEM((n_prob, LANES, LANES), F32),
                        pltpu.VMEM((n_prob, LANES, LANES), F32),
                        pltpu.VMEM((n_prob, LANES, LANES), F32)],
        compiler_params=pltpu.CompilerParams(
            dimension_semantics=("arbitrary", "arbitrary", "arbitrary"),
            vmem_limit_bytes=VMEM_LIMIT),
        name="wkv",
    )(rkv, rkv, rkv, lora, s0, mup, mun, k_k, k_a, r_k, w0, w2p, a0, a2p)


def _mid_kernel(y_ref, bn_ref, lg_ref, glu_ref, gates_ref, x_ref, mod_ref, g2w_ref, lnxg_ref,
                lnxb_ref, wor_ref, cw_ref, cb_ref, clg_ref, clb_ref, woc_ref, wo_ref, n2g_ref,
                x1_ref, xn2_ref, pad_scr, *, seq):
    mod = mod_ref[0]
    g1 = mod[:, 2 * D_MODEL:3 * D_MODEL]
    sh2 = mod[:, 3 * D_MODEL:4 * D_MODEL]
    sc2 = mod[:, 4 * D_MODEL:5 * D_MODEL]

    bd = _block_diag_ones()
    y = y_ref[0] + y_ref[1]
    yc = y - _head_sum(y, bd) * (1.0 / HEAD_DIM)
    var = _head_sum(yc * yc, bd) * (1.0 / HEAD_DIM)
    yn = yc * lax.rsqrt(var + GN_EPS) * lnxg_ref[...] + lnxb_ref[...]
    g = _mm(jax.nn.sigmoid(lg_ref[...]), g2w_ref[...])
    y_r = _mm((yn + bn_ref[0] + bn_ref[1]) * g, wor_ref[...])

    glu = glu_ref[...]
    uu = glu[:, 0:CONV_WIDTH] * jax.nn.sigmoid(glu[:, CONV_WIDTH:])
    nseq = TM // seq
    zpad = jnp.zeros((nseq, CONV_PAD, CONV_WIDTH), F32)
    pad_scr[0, :, 0:CONV_PAD, :] = zpad
    pad_scr[0, :, CONV_PAD + seq:, :] = zpad
    for s in range(nseq):
        pad_scr[0, s, CONV_PAD:CONV_PAD + seq, :] = uu[s * seq:(s + 1) * seq]
    live = seq + 2 * CONV_PAD - SUBLANES
    for m in range(1, SUBLANES):
        pad_scr[m, :, 0:live, :] = pad_scr[0, :, m:m + live, :]
    first = CONV_PAD - CONV_K // 2
    parts = []
    for s in range(nseq):
        for r0 in range(0, seq, CONV_ROWS):
            acc = jnp.broadcast_to(cb_ref[...], (CONV_ROWS, CONV_WIDTH))
            for t in range(CONV_K):
                m = (first + t) % SUBLANES
                lo = r0 + first + t - m
                acc = acc + pad_scr[m, s, lo:lo + CONV_ROWS, :] * cw_ref[t:t + 1, :]
            parts.append(acc)
    cv = jnp.concatenate(parts, axis=0)
    cm = cv - jnp.mean(cv, axis=-1, keepdims=True)
    cvar = jnp.mean(cm * cm, axis=-1, keepdims=True)
    un = cm * lax.rsqrt(cvar + LN_EPS) * clg_ref[...] + clb_ref[...]
    y_c = _mm(un * jax.nn.sigmoid(un), woc_ref[...])

    gates = gates_ref[...]
    merged = jax.nn.sigmoid(gates[:, 0:D_MODEL]) * y_r + jax.nn.sigmoid(gates[:, D_MODEL:]) * y_c
    x1 = x_ref[...] + g1 * _mm(merged, wo_ref[...])
    x1_ref[...] = x1
    xn2_ref[...] = (_rms(x1) * n2g_ref[...] * (1.0 + sc2) + sh2).astype(BF16)


def _mid(y2, bn2, lg, glu, gates, x, mod3, mod_row, g2w, lnxg, lnxb, wor, cw, cb, clg, clb, woc,
         wo, n2g, *, seq):
    n = x.shape[0]
    const = lambda i: (0, 0)
    tok = lambda i: (i, 0)
    both = lambda i: (0, i, 0)

    def full(a):
        return pl.BlockSpec(a.shape, const)

    return pl.pallas_call(
        functools.partial(_mid_kernel, seq=seq),
        grid=(n // TM,),
        in_specs=[pl.BlockSpec((2, TM, RWKV_WIDTH), both),
                  pl.BlockSpec((2, TM, RWKV_WIDTH), both),
                  pl.BlockSpec((TM, GATE_LORA), tok),
                  pl.BlockSpec((TM, 2 * CONV_WIDTH), tok),
                  pl.BlockSpec((TM, 2 * D_MODEL), tok),
                  pl.BlockSpec((TM, D_MODEL), tok),
                  pl.BlockSpec((1, 1, 6 * D_MODEL), lambda i: (mod_row(i), 0, 0)),
                  full(g2w), full(lnxg), full(lnxb), full(wor), full(cw), full(cb), full(clg),
                  full(clb), full(woc), full(wo), full(n2g)],
        out_specs=[pl.BlockSpec((TM, D_MODEL), tok), pl.BlockSpec((TM, D_MODEL), tok)],
        out_shape=[jax.ShapeDtypeStruct((n, D_MODEL), F32),
                   jax.ShapeDtypeStruct((n, D_MODEL), BF16)],
        scratch_shapes=[pltpu.VMEM((SUBLANES, TM // seq, seq + 2 * CONV_PAD, CONV_WIDTH), F32)],
        compiler_params=pltpu.CompilerParams(dimension_semantics=("arbitrary",),
                                             vmem_limit_bytes=VMEM_LIMIT),
        name="mid",
    )(y2, bn2, lg, glu, gates, x, mod3, g2w, lnxg, lnxb, wor, cw, cb, clg, clb, woc, wo, n2g)


def _mlp_kernel(x1_ref, xn2_ref, mod_ref, w1_ref, w2_ref, fg_ref, o_ref):
    g2 = mod_ref[0][:, 5 * D_MODEL:6 * D_MODEL]
    h = jnp.maximum(jnp.dot(xn2_ref[...], w1_ref[...], preferred_element_type=F32), 0.0)
    x2 = x1_ref[...] + g2 * _mm(h * h, w2_ref[...])
    o_ref[...] = _rms(x2) * fg_ref[...]


def _mlp(x1, xn2, mod3, mod_row, w1, w2, fg):
    n = x1.shape[0]
    const = lambda i: (0, 0)
    tok = lambda i: (i, 0)
    return pl.pallas_call(
        _mlp_kernel,
        grid=(n // TM,),
        in_specs=[pl.BlockSpec((TM, D_MODEL), tok),
                  pl.BlockSpec((TM, D_MODEL), tok),
                  pl.BlockSpec((1, 1, 6 * D_MODEL), lambda i: (mod_row(i), 0, 0)),
                  pl.BlockSpec(w1.shape, const),
                  pl.BlockSpec(w2.shape, const),
                  pl.BlockSpec((1, D_MODEL), const)],
        out_specs=pl.BlockSpec((TM, D_MODEL), tok),
        out_shape=jax.ShapeDtypeStruct((n, D_MODEL), F32),
        compiler_params=pltpu.CompilerParams(dimension_semantics=("arbitrary",),
                                             vmem_limit_bytes=VMEM_LIMIT),
        name="mlp",
    )(x1, xn2, mod3, w1, w2, fg)


def _pad_dir(w):
    return (jnp.eye(2, dtype=w.dtype)[:, :, None, None] * w[:, None]).reshape(2, 2 * w.shape[1], w.shape[2])


def kernel(x_prompt, x_sample, state_fwd, state_bwd, c, c_ctx, ada_w, ada_b, norm1_g, norm2_g, w_in,
           mu_prev, mu_next, decay_w0, decay_w1, decay_w2, iclr_a0, iclr_a1, iclr_a2, gate_g1, gate_g2,
           k_k, k_a, r_k, lnx_g, lnx_b, w_out_rwkv, conv_w, conv_b, conv_ln_g, conv_ln_b, w_out_conv,
           w_o, mlp_w1, mlp_w2, final_g):
    n_ctx, seq_ctx, _ = x_prompt.shape
    n_lat, seq_lat, _ = x_sample.shape
    depth = ada_w.shape[0]
    assert seq_ctx == TM and seq_lat % TM == 0 and GRID_W == CHUNK and c.shape[0] + 1 <= 8
    lat_blk = seq_lat // TM
    row_ctx = lambda i: 0
    row_lat = lambda i: 1 + i // lat_blk

    xp = x_prompt.reshape(n_ctx * seq_ctx, D_MODEL)
    xs = x_sample.reshape(n_lat * seq_lat, D_MODEL)
    cond = jnp.concatenate([c_ctx[None, :], c, jnp.zeros((7 - c.shape[0], D_MODEL), F32)], axis=0)
    zero_state = jnp.zeros((2, 1, N_HEADS, HEAD_DIM, HEAD_DIM), F32)
    row = lambda a: a.reshape(1, -1)
    assert depth == 1
    lay = lambda a: a.reshape(a.shape[1:])
    bf = lambda a: lay(a).astype(BF16)

    mod3 = _ada(cond, lay(ada_w), lay(ada_b)).reshape(8, 1, 6 * D_MODEL)
    w_lora = jnp.concatenate([decay_w1[0, 0], decay_w1[0, 1], iclr_a1[0, 0], iclr_a1[0, 1], gate_g1[0]],
                             axis=1).astype(BF16)
    front_w = (norm1_g, bf(w_in), w_lora)
    wkv_w = (mu_prev, mu_next, k_k, k_a, row(r_k),
             decay_w0.reshape(2, 1, RWKV_WIDTH), _pad_dir(lay(decay_w2)).astype(BF16),
             iclr_a0.reshape(2, 1, RWKV_WIDTH), _pad_dir(lay(iclr_a2)).astype(BF16))
    mid_w = (bf(gate_g2), lnx_g, lnx_b, bf(w_out_rwkv), lay(conv_w), conv_b, conv_ln_g, conv_ln_b,
             bf(w_out_conv), bf(w_o), norm2_g)
    mlp_w = (bf(mlp_w1), bf(mlp_w2), row(final_g))
    s0_lat = jnp.stack([state_fwd[:, 0], state_bwd[:, 0]]).astype(F32)

    def layer(x, mod_row, s0, nseq, nblk, conv_seq):
        rkv, glu, gates, lora, lg = _front(x, mod3, mod_row, *front_w)
        y2, bn2, s_out = _wkv(rkv, lora, s0, *wkv_w, nseq=nseq, nblk=nblk)
        x1, xn2 = _mid(y2, bn2, lg, glu, gates, x, mod3, mod_row, *mid_w, seq=conv_seq)
        return _mlp(x1, xn2, mod3, mod_row, *mlp_w), s_out

    yp, s_ctx = layer(xp, row_ctx, zero_state, n_ctx, 1, seq_ctx)
    ys, _ = layer(xs, row_lat, s0_lat, n_lat, lat_blk, GRID_W)
    states = s_ctx.astype(x_prompt.dtype)[:, :, None]
    return (yp.reshape(x_prompt.shape), ys.reshape(x_sample.shape), states[0], states[1])
```

```python
import functools
import itertools

import jax
import jax.numpy as jnp
from jax import lax
from jax.experimental import pallas as pl
from jax.experimental.pallas import tpu as pltpu

D_MODEL = 1024
RWKV_WIDTH = 512
HEAD_DIM = 64
N_HEADS = RWKV_WIDTH // HEAD_DIM
CONV_WIDTH = 512
CONV_K = 31
D_FF = 4 * D_MODEL
GRID_W = 64
LORA_W = 64
GATE_LORA = 128
RMS_EPS = 1e-6
LN_EPS = 1e-5
GN_EPS = 64e-5

LANES = 128
SUBLANES = 8
TM = 256
CHUNK = 64
N_CHUNK = TM // CHUNK
N_PAIR = RWKV_WIDTH // LANES
HALO = SUBLANES
CONV_PAD = 16
CONV_ROWS = 32
VMEM_LIMIT = 56 * 1024 * 1024

F32 = jnp.float32
BF16 = jnp.bfloat16
HIGHEST = lax.Precision.HIGHEST
DECAY_SCALE = 0.6065306597126334


def _mm(a, b):
    return jnp.dot(a.astype(BF16), b.astype(BF16), preferred_element_type=F32)


def _mm_nt(a, b):
    return lax.dot_general(a.astype(BF16), b.astype(BF16), (((1,), (1,)), ((), ())),
                           preferred_element_type=F32)


def _split3(x):
    x1 = x.astype(BF16)
    r1 = x - x1.astype(F32)
    x2 = r1.astype(BF16)
    x3 = (r1 - x2.astype(F32)).astype(BF16)
    return x3, x2, x1


def _mm_split3(m, x):
    x3, x2, x1 = (jnp.dot(m, t, preferred_element_type=F32) for t in _split3(x))
    return x3 + x2 + x1


def _mm_f32(a, b):
    return jnp.dot(a, b, precision=HIGHEST, preferred_element_type=F32)


def _head_sum(x, bd):
    rows = x.shape[0]
    n_grp = x.shape[1] // LANES
    terms = [t for p in range(n_grp) for t in _split3(x[:, p * LANES:(p + 1) * LANES])]
    prod = jnp.dot(jnp.concatenate(terms, axis=0), bd, preferred_element_type=F32)
    sums = [prod[(3 * p) * rows:(3 * p + 1) * rows] + prod[(3 * p + 1) * rows:(3 * p + 2) * rows]
            + prod[(3 * p + 2) * rows:(3 * p + 3) * rows] for p in range(n_grp)]
    return jnp.concatenate(sums, axis=1)


def _block_diag_ones():
    ri = lax.broadcasted_iota(jnp.int32, (LANES, LANES), 0)
    ci = lax.broadcasted_iota(jnp.int32, (LANES, LANES), 1)
    shift = HEAD_DIM.bit_length() - 1
    return jnp.where((ri >> shift) == (ci >> shift), 1.0, 0.0).astype(BF16)


def _rms(x):
    return x * lax.rsqrt(jnp.mean(x * x, axis=-1, keepdims=True) + RMS_EPS)


def _ada_kernel(c_ref, w_ref, b_ref, o_ref):
    c = c_ref[...]
    o_ref[...] = _mm_f32(c * jax.nn.sigmoid(c), w_ref[...]) + b_ref[...]


def _ada(cond, w, b):
    n = w.shape[1]
    tn = 1024
    return pl.pallas_call(
        _ada_kernel,
        grid=(n // tn,),
        in_specs=[pl.BlockSpec((8, D_MODEL), lambda i: (0, 0)),
                  pl.BlockSpec((D_MODEL, tn), lambda i: (0, i)),
                  pl.BlockSpec((1, tn), lambda i: (0, i))],
        out_specs=pl.BlockSpec((8, tn), lambda i: (0, i)),
        out_shape=jax.ShapeDtypeStruct((8, n), F32),
        compiler_params=pltpu.CompilerParams(dimension_semantics=("arbitrary",),
                                             vmem_limit_bytes=VMEM_LIMIT),
        name="ada",
    )(cond, w, b.reshape(1, n))


_IN_SPLIT = (3 * RWKV_WIDTH, 2 * CONV_WIDTH, 2 * D_MODEL)
_LORA_SPLIT = (4 * LORA_W, GATE_LORA)


def _front_kernel(x_ref, mod_ref, g_ref, win_ref, wlo_ref, *out_refs):
    mod = mod_ref[0]
    sh1 = mod[:, 0:D_MODEL]
    sc1 = mod[:, D_MODEL:2 * D_MODEL]
    xn = (_rms(x_ref[...]) * g_ref[...] * (1.0 + sc1) + sh1).astype(BF16)
    outs = iter(out_refs)
    for w_ref, split in ((win_ref, _IN_SPLIT), (wlo_ref, _LORA_SPLIT)):
        lo = 0
        for width in split:
            next(outs)[...] = jnp.dot(xn, w_ref[:, lo:lo + width], preferred_element_type=F32)
            lo += width


def _front(x, mod3, mod_row, g, w_in, w_lora):
    n = x.shape[0]
    const = lambda i: (0, 0)
    tok = lambda i: (i, 0)
    widths = _IN_SPLIT + _LORA_SPLIT
    return pl.pallas_call(
        _front_kernel,
        grid=(n // TM,),
        in_specs=[pl.BlockSpec((TM, D_MODEL), tok),
                  pl.BlockSpec((1, 1, 6 * D_MODEL), lambda i: (mod_row(i), 0, 0)),
                  pl.BlockSpec((1, D_MODEL), const),
                  pl.BlockSpec(w_in.shape, const),
                  pl.BlockSpec(w_lora.shape, const)],
        out_specs=[pl.BlockSpec((TM, w), tok) for w in widths],
        out_shape=[jax.ShapeDtypeStruct((n, w), F32) for w in widths],
        compiler_params=pltpu.CompilerParams(dimension_semantics=("arbitrary",),
                                             vmem_limit_bytes=VMEM_LIMIT),
        name="front",
    )(x, mod3, g, w_in, w_lora)


def _tile_coords(t, nseq, nblk):
    per_dir = nseq * nblk
    d = t // per_dir
    rem = t - d * per_dir
    s = rem // nblk
    return d, s, rem - s * nblk


def _wkv_masks(d):
    shift = CHUNK.bit_length() - 1
    ri = lax.broadcasted_iota(jnp.int32, (LANES, LANES), 0)
    ci = lax.broadcasted_iota(jnp.int32, (LANES, LANES), 1)
    same = (ri >> shift) == (ci >> shift)
    rel = (ci - ri) * (1 - 2 * d)
    levels = [((ri >> (lv + 1)) == (ci >> (lv + 1))) & (((ri >> lv) & 1) == 1 - d) & (((ci >> lv) & 1) == d)
              for lv in range(shift)]
    return same & (rel < 0), same & (rel <= 0), levels


def _run(*streams):
    streams = list(streams)
    while streams:
        streams = [g for g in streams if next(g, True) is None]


def _wkv_kernel(rkv_ref, prev_ref, next_ref, lora_ref, s0_ref, mup_ref, mun_ref, kk_ref, ka_ref,
                rk_ref, w0_ref, w2_ref, a0_ref, a2_ref,
                y_ref, bonus_ref, sout_ref,
                s_scr, r_scr, kd_scr, v_scr, aa_scr, b_scr, lw_scr, cl_scr, bn_scr,
                rhs_scr, out_scr, upd_scr, ti_scr, vs_scr, wc_scr, aab_scr, t_scr, *, nseq, nblk):
    g = pl.program_id(0)
    n_tile = 2 * nseq * nblk
    n_prob = N_CHUNK * N_PAIR
    d_a, _, j_a = _tile_coords(jnp.minimum(g, n_tile - 1), nseq, nblk)
    d_b, _, j_b = _tile_coords(jnp.maximum(g - 1, 0), nseq, nblk)
    masks_a = _wkv_masks(d_a)
    masks_b = _wkv_masks(d_b)
    shift = CHUNK.bit_length() - 1
    ri = lax.broadcasted_iota(jnp.int32, (LANES, LANES), 0)
    ci = lax.broadcasted_iota(jnp.int32, (LANES, LANES), 1)
    eye = (ri == ci).astype(F32)
    head0 = lax.broadcasted_iota(jnp.int32, (CHUNK, LANES), 1) < HEAD_DIM
    dot = functools.partial(jnp.dot, preferred_element_type=F32)

    def stack(t):
        return jnp.concatenate([jnp.where(head0, t, 0.0), jnp.where(head0, 0.0, t)], axis=0)

    def twice(t):
        return jnp.concatenate([t, t], axis=0)

    def chunk_rows(step, d):
        cidx = step + d * (N_CHUNK - 1 - 2 * step)
        return pl.ds(pl.multiple_of(cidx * CHUNK, CHUNK), CHUNK)

    def prepare(slot, d, j):
        blk = j + d * (nblk - 1 - 2 * j)
        u = rkv_ref[...]
        row = lax.broadcasted_iota(jnp.int32, (TM, 1), 0)
        prow = jnp.where(blk > 0, prev_ref[HALO - 1:HALO, :], 0.0)
        nrow = jnp.where(blk < nblk - 1, next_ref[0:1, :], 0.0)
        prev = jnp.where(row == 0, prow, pltpu.roll(u, 1, axis=0))
        nxt = jnp.where(row == TM - 1, nrow, pltpu.roll(u, TM - 1, axis=0))
        x = u + (prev - u) * mup_ref[...] + (nxt - u) * mun_ref[...]
        r = x[:, 0:RWKV_WIDTH]
        k = x[:, RWKV_WIDTH:2 * RWKV_WIDTH]
        v = x[:, 2 * RWKV_WIDTH:]
        r_scr[slot] = r
        v_scr[slot] = v
        yield
        bd = _block_diag_ones()
        kkr = k * kk_ref[...]
        kk = kkr * lax.rsqrt(jnp.maximum(_head_sum(kkr * kkr, bd), 1e-24))
        aa_scr[slot] = -kk
        yield
        lo = lora_ref[...]
        z = w0_ref[0] + _mm(jnp.tanh(lo[:, 0:2 * LORA_W]), w2_ref[0])
        lw = -DECAY_SCALE * jax.nn.sigmoid(z)
        lw_scr[slot, 0:TM, :] = lw
        a = jax.nn.sigmoid(a0_ref[0] + _mm(lo[:, 2 * LORA_W:4 * LORA_W], a2_ref[0]))
        kd = k * (1.0 + (a - 1.0) * ka_ref[...])
        kd_scr[slot] = kd
        b_scr[slot] = kk * a
        yield
        bn_scr[slot] = _head_sum(r * kd * rk_ref[...], bd) * v
        yield
        rt = lax.broadcasted_iota(jnp.int32, (TM, TM), 0)
        ct = lax.broadcasted_iota(jnp.int32, (TM, TM), 1)
        cum_m = jnp.where(((rt >> shift) == (ct >> shift)) & (((ct - rt) * (1 - 2 * d)) <= 0), 1.0, 0.0)
        cl_scr[slot] = _mm_split3(cum_m.astype(BF16), lw)
        yield

    def operands(step, slot, d, masks):
        strict, incl, levels = masks
        rows = chunk_rows(step, d)
        for p in range(N_PAIR):
            i = slot * n_prob + step * N_PAIR + p
            sl = slice(p * LANES, (p + 1) * LANES)
            lw = lw_scr[slot, rows, sl]
            cl = cl_scr[slot, rows, sl]
            tot = jnp.sum(lw, axis=0, keepdims=True)
            e_out = jnp.exp(-cl)
            e_end = jnp.exp(tot - cl)
            b_c = b_scr[slot, rows, sl]
            kd_c = kd_scr[slot, rows, sl]
            a_s = stack(aa_scr[slot, rows, sl] * jnp.exp(cl - lw)).astype(BF16)
            r_s = stack(r_scr[slot, rows, sl] * jnp.exp(cl)).astype(BF16)
            bk_s = jnp.concatenate([twice(b_c * e_out), twice(kd_c * e_out)], axis=0).astype(BF16)
            vs_scr[i] = stack(v_scr[slot, rows, sl]).astype(BF16)
            upd_scr[i] = jnp.concatenate([jnp.transpose(stack(b_c * e_end)),
                                          jnp.transpose(stack(kd_c * e_end))], axis=1).astype(BF16)
            wc_scr[i] = jnp.transpose(jnp.broadcast_to(jnp.exp(tot), (LANES, LANES)))
            ab = _mm_nt(jnp.concatenate([a_s, r_s], axis=0), bk_s)
            a_ab = jnp.where(strict, ab[0:LANES, 0:LANES], 0.0)
            a_ak = jnp.where(strict, ab[0:LANES, LANES:], 0.0)
            a_rb = jnp.where(incl, ab[LANES:, 0:LANES], 0.0)
            a_rk = jnp.where(incl, ab[LANES:, LANES:], 0.0)
            rhs_scr[i] = jnp.concatenate([a_s, a_ak.astype(BF16)], axis=1)
            out_scr[i] = jnp.concatenate([r_s, a_rb.astype(BF16), a_rk.astype(BF16)], axis=1)
            aab_scr[i] = a_ab
            t_scr[i] = eye + jnp.where(levels[0], a_ab, 0.0)
            yield

    def invert(steps, slot, masks):
        levels = masks[2]
        idx = [slot * n_prob + step * N_PAIR + p for step in steps for p in range(N_PAIR)]
        for lvl in levels[1:]:
            tbs = [t_scr[i].astype(BF16) for i in idx]
            ms = [dot(jnp.where(lvl, aab_scr[i], 0.0).astype(BF16), tb) for i, tb in zip(idx, tbs)]
            lw_scr[slot, TM:, 0:LANES] = ms[-1][0:SUBLANES, :]
            yield
            for i, tb, m in zip(idx, tbs, ms):
                t_scr[i] = t_scr[i] + dot(tb, m.astype(BF16))
            lw_scr[slot, TM:, 0:LANES] = ms[0][0:SUBLANES, :]
            yield
        for i in idx:
            t = t_scr[i]
            th = t.astype(BF16)
            ti_scr[i, 0] = th
            ti_scr[i, 1] = (t - th.astype(F32)).astype(BF16)

    def scan(step, slot, d):
        rows = chunk_rows(step, d)
        base = slot * n_prob + step * N_PAIR
        pairs = range(N_PAIR)
        sts = [s_scr[p] for p in pairs]
        stb = [st.astype(BF16) for st in sts]
        vss = [vs_scr[base + p] for p in pairs]
        rhs = [dot(rhs_scr[base + p], jnp.concatenate([stb[p], vss[p]], axis=0)) for p in pairs]
        yield
        pmb = []
        for p in pairs:
            rh = rhs[p].astype(BF16)
            rl = (rhs[p] - rh.astype(F32)).astype(BF16)
            th = ti_scr[base + p, 0]
            pm = dot(th, rh) + dot(th, rl) + dot(ti_scr[base + p, 1], rh)
            pmb.append(pm.astype(BF16))
        yield
        for p in pairs:
            s_scr[p] = wc_scr[base + p] * sts[p] + dot(
                upd_scr[base + p], jnp.concatenate([pmb[p], vss[p]], axis=0))
        yield
        for p in pairs:
            y_s = dot(out_scr[base + p], jnp.concatenate([stb[p], pmb[p], vss[p]], axis=0))
            y_ref[0, rows, p * LANES:(p + 1) * LANES] = y_s[0:CHUNK] + y_s[CHUNK:]
        yield

    first, second = range(0, N_CHUNK // 2), range(N_CHUNK // 2, N_CHUNK)
    chain = itertools.chain.from_iterable

    def stage_a(slot):
        return (prepare(slot, d_a, j_a),
                chain(operands(s, slot, d_a, masks_a) for s in first))

    def stage_b(slot, with_inverse=(), with_scan=()):
        @pl.when(j_b == 0)
        def _():
            zero = jnp.zeros((HEAD_DIM, HEAD_DIM), F32)
            for p in range(N_PAIR):
                top = jnp.concatenate([s0_ref[0, 0, 2 * p], zero], axis=1)
                bot = jnp.concatenate([zero, s0_ref[0, 0, 2 * p + 1]], axis=1)
                s_scr[p] = jnp.transpose(jnp.concatenate([top, bot], axis=0))

        bonus_ref[0] = bn_scr[slot]
        _run(invert(first, slot, masks_b), chain(operands(s, slot, d_b, masks_b) for s in second))
        _run(invert(second, slot, masks_b), chain(scan(s, slot, d_b) for s in first), *with_inverse)
        _run(chain(scan(s, slot, d_b) for s in second), *with_scan)

        @pl.when(j_b == nblk - 1)
        def _():
            for p in range(N_PAIR):
                s_vk = jnp.transpose(s_scr[p])
                sout_ref[0, 0, 2 * p] = s_vk[0:HEAD_DIM, 0:HEAD_DIM]
                sout_ref[0, 0, 2 * p + 1] = s_vk[HEAD_DIM:, HEAD_DIM:]

    @pl.when(g == 0)
    def _():
        prep, ops = stage_a(0)
        _run(prep)
        _run(ops)

    for parity in range(2):
        @pl.when((g > 0) & (g < n_tile) & (g % 2 == parity))
        def _():
            prep, ops = stage_a(parity)
            stage_b(1 - parity, with_inverse=(prep,), with_scan=(ops,))

    @pl.when(g == n_tile)
    def _():
        stage_b((n_tile - 1) % 2)


def _wkv(rkv, lora, s0, mup, mun, k_k, k_a, r_k, w0, w2p, a0, a2p, *, nseq, nblk):
    n = rkv.shape[0]
    hb = TM // HALO
    n_tile = 2 * nseq * nblk

    def coords(t):
        d, s, j = _tile_coords(t, nseq, nblk)
        return d, s, s * nblk + j + d * (nblk - 1 - 2 * j)

    ahead = lambda g: coords(jnp.minimum(g, n_tile - 1))
    behind = lambda g: coords(jnp.maximum(g - 1, 0))
    s0_seq = (lambda s: s) if s0.shape[1] > 1 else (lambda s: 0)
    const = lambda g: (0, 0)
    per_dir = lambda g: (ahead(g)[0], 0, 0)
    out_tile = lambda g: (behind(g)[0], behind(g)[2], 0)
    kern = functools.partial(_wkv_kernel, nseq=nseq, nblk=nblk)
    tok = pltpu.VMEM((2, TM, RWKV_WIDTH), F32)
    n_prob = 2 * N_CHUNK * N_PAIR
    return pl.pallas_call(
        kern,
        grid=(n_tile + 1,),
        in_specs=[
            pl.BlockSpec((TM, 3 * RWKV_WIDTH), lambda g: (ahead(g)[2], 0)),
            pl.BlockSpec((HALO, 3 * RWKV_WIDTH), lambda g: (jnp.maximum(ahead(g)[2] * hb - 1, 0), 0)),
            pl.BlockSpec((HALO, 3 * RWKV_WIDTH),
                         lambda g: (jnp.minimum((ahead(g)[2] + 1) * hb, n // HALO - 1), 0)),
            pl.BlockSpec((TM, 4 * LORA_W), lambda g: (ahead(g)[2], 0)),
            pl.BlockSpec((1, 1, N_HEADS, HEAD_DIM, HEAD_DIM),
                         lambda g: (behind(g)[0], s0_seq(behind(g)[1]), 0, 0, 0)),
            pl.BlockSpec((1, 3 * RWKV_WIDTH), const),
            pl.BlockSpec((1, 3 * RWKV_WIDTH), const),
            pl.BlockSpec((1, RWKV_WIDTH), const),
            pl.BlockSpec((1, RWKV_WIDTH), const),
            pl.BlockSpec((1, RWKV_WIDTH), const),
            pl.BlockSpec((1, 1, RWKV_WIDTH), per_dir),
            pl.BlockSpec((1, 2 * LORA_W, RWKV_WIDTH), per_dir),
            pl.BlockSpec((1, 1, RWKV_WIDTH), per_dir),
            pl.BlockSpec((1, 2 * LORA_W, RWKV_WIDTH), per_dir),
        ],
        out_specs=[
            pl.BlockSpec((1, TM, RWKV_WIDTH), out_tile),
            pl.BlockSpec((1, TM, RWKV_WIDTH), out_tile),
            pl.BlockSpec((1, 1, N_HEADS, HEAD_DIM, HEAD_DIM),
                         lambda g: (behind(g)[0], behind(g)[1], 0, 0, 0)),
        ],
        out_shape=[jax.ShapeDtypeStruct((2, n, RWKV_WIDTH), F32),
                   jax.ShapeDtypeStruct((2, n, RWKV_WIDTH), F32),
                   jax.ShapeDtypeStruct((2, nseq, N_HEADS, HEAD_DIM, HEAD_DIM), F32)],
        scratch_shapes=[pltpu.VMEM((N_PAIR, LANES, LANES), F32),
                        tok, tok, tok, tok, tok,
                        pltpu.VMEM((2, TM + SUBLANES, RWKV_WIDTH), F32),
                        tok, tok,
                        pltpu.VMEM((n_prob, LANES, 2 * LANES), BF16),
                        pltpu.VMEM((n_prob, LANES, 3 * LANES), BF16),
                        pltpu.VMEM((n_prob, LANES, 2 * LANES), BF16),
                        pltpu.VMEM((n_prob, 2, LANES, LANES), BF16),
                        pltpu.VMEM((n_prob, LANES, LANES), BF16),
                        pltpu.VMEM((n_prob, LANES, LANES), F32),
                        pltpu.VMEM((n_prob, LANES, LANES), F32),
                        pltpu.VMEM((n_prob, LANES, LANES), F32)],
        compiler_params=pltpu.CompilerParams(dimension_semantics=("arbitrary",),
                                             vmem_limit_bytes=VMEM_LIMIT),
        name="wkv",
    )(rkv, rkv, rkv, lora, s0, mup, mun, k_k, k_a, r_k, w0, w2p, a0, a2p)


def _mid_kernel(y_ref, bn_ref, lg_ref, glu_ref, gates_ref, x_ref, mod_ref, g2w_ref, lnxg_ref,
                lnxb_ref, wor_ref, cw_ref, cb_ref, clg_ref, clb_ref, woc_ref, wo_ref, n2g_ref,
                x1_ref, xn2_ref, pad_scr, *, seq):
    mod = mod_ref[0]
    g1 = mod[:, 2 * D_MODEL:3 * D_MODEL]
    sh2 = mod[:, 3 * D_MODEL:4 * D_MODEL]
    sc2 = mod[:, 4 * D_MODEL:5 * D_MODEL]

    bd = _block_diag_ones()
    y = y_ref[0] + y_ref[1]
    yc = y - _head_sum(y, bd) * (1.0 / HEAD_DIM)
    var = _head_sum(yc * yc, bd) * (1.0 / HEAD_DIM)
    yn = yc * lax.rsqrt(var + GN_EPS) * lnxg_ref[...] + lnxb_ref[...]
    g = _mm(jax.nn.sigmoid(lg_ref[...]), g2w_ref[...])
    y_r = _mm((yn + bn_ref[0] + bn_ref[1]) * g, wor_ref[...])

    glu = glu_ref[...]
    uu = glu[:, 0:CONV_WIDTH] * jax.nn.sigmoid(glu[:, CONV_WIDTH:])
    nseq = TM // seq
    zpad = jnp.zeros((nseq, CONV_PAD, CONV_WIDTH), F32)
    pad_scr[0, :, 0:CONV_PAD, :] = zpad
    pad_scr[0, :, CONV_PAD + seq:, :] = zpad
    for s in range(nseq):
        pad_scr[0, s, CONV_PAD:CONV_PAD + seq, :] = uu[s * seq:(s + 1) * seq]
    live = seq + 2 * CONV_PAD - SUBLANES
    for m in range(1, SUBLANES):
        pad_scr[m, :, 0:live, :] = pad_scr[0, :, m:m + live, :]
    first = CONV_PAD - CONV_K // 2
    parts = []
    for s in range(nseq):
        for r0 in range(0, seq, CONV_ROWS):
            acc = jnp.broadcast_to(cb_ref[...], (CONV_ROWS, CONV_WIDTH))
            for t in range(CONV_K):
                m = (first + t) % SUBLANES
                lo = r0 + first + t - m
                acc = acc + pad_scr[m, s, lo:lo + CONV_ROWS, :] * cw_ref[t:t + 1, :]
            parts.append(acc)
    cv = jnp.concatenate(parts, axis=0)
    cm = cv - jnp.mean(cv, axis=-1, keepdims=True)
    cvar = jnp.mean(cm * cm, axis=-1, keepdims=True)
    un = cm * lax.rsqrt(cvar + LN_EPS) * clg_ref[...] + clb_ref[...]
    y_c = _mm(un * jax.nn.sigmoid(un), woc_ref[...])

    gates = gates_ref[...]
    merged = jax.nn.sigmoid(gates[:, 0:D_MODEL]) * y_r + jax.nn.sigmoid(gates[:, D_MODEL:]) * y_c
    x1 = x_ref[...] + g1 * _mm(merged, wo_ref[...])
    x1_ref[...] = x1
    xn2_ref[...] = (_rms(x1) * n2g_ref[...] * (1.0 + sc2) + sh2).astype(BF16)


def _mid(y2, bn2, lg, glu, gates, x, mod3, mod_row, g2w, lnxg, lnxb, wor, cw, cb, clg, clb, woc,
         wo, n2g, *, seq):
    n = x.shape[0]
    const = lambda i: (0, 0)
    tok = lambda i: (i, 0)
    both = lambda i: (0, i, 0)

    def full(a):
        return pl.BlockSpec(a.shape, const)

    return pl.pallas_call(
        functools.partial(_mid_kernel, seq=seq),
        grid=(n // TM,),
        in_specs=[pl.BlockSpec((2, TM, RWKV_WIDTH), both),
                  pl.BlockSpec((2, TM, RWKV_WIDTH), both),
                  pl.BlockSpec((TM, GATE_LORA), tok),
                  pl.BlockSpec((TM, 2 * CONV_WIDTH), tok),
                  pl.BlockSpec((TM, 2 * D_MODEL), tok),
                  pl.BlockSpec((TM, D_MODEL), tok),
                  pl.BlockSpec((1, 1, 6 * D_MODEL), lambda i: (mod_row(i), 0, 0)),
                  full(g2w), full(lnxg), full(lnxb), full(wor), full(cw), full(cb), full(clg),
                  full(clb), full(woc), full(wo), full(n2g)],
        out_specs=[pl.BlockSpec((TM, D_MODEL), tok), pl.BlockSpec((TM, D_MODEL), tok)],
        out_shape=[jax.ShapeDtypeStruct((n, D_MODEL), F32),
                   jax.ShapeDtypeStruct((n, D_MODEL), BF16)],
        scratch_shapes=[pltpu.VMEM((SUBLANES, TM // seq, seq + 2 * CONV_PAD, CONV_WIDTH), F32)],
        compiler_params=pltpu.CompilerParams(dimension_semantics=("arbitrary",),
                                             vmem_limit_bytes=VMEM_LIMIT),
        name="mid",
    )(y2, bn2, lg, glu, gates, x, mod3, g2w, lnxg, lnxb, wor, cw, cb, clg, clb, woc, wo, n2g)


def _mlp_kernel(x1_ref, xn2_ref, mod_ref, w1_ref, w2_ref, fg_ref, o_ref):
    g2 = mod_ref[0][:, 5 * D_MODEL:6 * D_MODEL]
    h = jnp.maximum(jnp.dot(xn2_ref[...], w1_ref[...], preferred_element_type=F32), 0.0)
    x2 = x1_ref[...] + g2 * _mm(h * h, w2_ref[...])
    o_ref[...] = _rms(x2) * fg_ref[...]


def _mlp(x1, xn2, mod3, mod_row, w1, w2, fg):
    n = x1.shape[0]
    const = lambda i: (0, 0)
    tok = lambda i: (i, 0)
    return pl.pallas_call(
        _mlp_kernel,
        grid=(n // TM,),
        in_specs=[pl.BlockSpec((TM, D_MODEL), tok),
                  pl.BlockSpec((TM, D_MODEL), tok),
                  pl.BlockSpec((1, 1, 6 * D_MODEL), lambda i: (mod_row(i), 0, 0)),
                  pl.BlockSpec(w1.shape, const),
                  pl.BlockSpec(w2.shape, const),
                  pl.BlockSpec((1, D_MODEL), const)],
        out_specs=pl.BlockSpec((TM, D_MODEL), tok),
        out_shape=jax.ShapeDtypeStruct((n, D_MODEL), F32),
        compiler_params=pltpu.CompilerParams(dimension_semantics=("arbitrary",),
                                             vmem_limit_bytes=VMEM_LIMIT),
        name="mlp",
    )(x1, xn2, mod3, w1, w2, fg)


def _pad_dir(w):
    return (jnp.eye(2, dtype=w.dtype)[:, :, None, None] * w[:, None]).reshape(2, 2 * w.shape[1], w.shape[2])


def kernel(x_prompt, x_sample, state_fwd, state_bwd, c, c_ctx, ada_w, ada_b, norm1_g, norm2_g, w_in,
           mu_prev, mu_next, decay_w0, decay_w1, decay_w2, iclr_a0, iclr_a1, iclr_a2, gate_g1, gate_g2,
           k_k, k_a, r_k, lnx_g, lnx_b, w_out_rwkv, conv_w, conv_b, conv_ln_g, conv_ln_b, w_out_conv,
           w_o, mlp_w1, mlp_w2, final_g):
    n_ctx, seq_ctx, _ = x_prompt.shape
    n_lat, seq_lat, _ = x_sample.shape
    depth = ada_w.shape[0]
    assert seq_ctx == TM and seq_lat % TM == 0 and GRID_W == CHUNK and c.shape[0] + 1 <= 8
    lat_blk = seq_lat // TM
    row_ctx = lambda i: 0
    row_lat = lambda i: 1 + i // lat_blk

    xp = x_prompt.reshape(n_ctx * seq_ctx, D_MODEL)
    xs = x_sample.reshape(n_lat * seq_lat, D_MODEL)
    cond = jnp.concatenate([c_ctx[None, :], c, jnp.zeros((7 - c.shape[0], D_MODEL), F32)], axis=0)
    zero_state = jnp.zeros((2, 1, N_HEADS, HEAD_DIM, HEAD_DIM), F32)
    row = lambda a: a.reshape(1, -1)
    assert depth == 1
    lay = lambda a: a.reshape(a.shape[1:])
    bf = lambda a: lay(a).astype(BF16)

    mod3 = _ada(cond, lay(ada_w), lay(ada_b)).reshape(8, 1, 6 * D_MODEL)
    w_lora = jnp.concatenate([decay_w1[0, 0], decay_w1[0, 1], iclr_a1[0, 0], iclr_a1[0, 1], gate_g1[0]],
                             axis=1).astype(BF16)
    front_w = (norm1_g, bf(w_in), w_lora)
    wkv_w = (mu_prev, mu_next, k_k, k_a, row(r_k),
             decay_w0.reshape(2, 1, RWKV_WIDTH), _pad_dir(lay(decay_w2)).astype(BF16),
             iclr_a0.reshape(2, 1, RWKV_WIDTH), _pad_dir(lay(iclr_a2)).astype(BF16))
    mid_w = (bf(gate_g2), lnx_g, lnx_b, bf(w_out_rwkv), lay(conv_w), conv_b, conv_ln_g, conv_ln_b,
             bf(w_out_conv), bf(w_o), norm2_g)
    mlp_w = (bf(mlp_w1), bf(mlp_w2), row(final_g))
    s0_lat = jnp.stack([state_fwd[:, 0], state_bwd[:, 0]]).astype(F32)

    def layer(x, mod_row, s0, nseq, nblk, conv_seq):
        rkv, glu, gates, lora, lg = _front(x, mod3, mod_row, *front_w)
        y2, bn2, s_out = _wkv(rkv, lora, s0, *wkv_w, nseq=nseq, nblk=nblk)
        x1, xn2 = _mid(y2, bn2, lg, glu, gates, x, mod3, mod_row, *mid_w, seq=conv_seq)
        return _mlp(x1, xn2, mod3, mod_row, *mlp_w), s_out

    yp, s_ctx = layer(xp, row_ctx, zero_state, n_ctx, 1, seq_ctx)
    ys, _ = layer(xs, row_lat, s0_lat, n_lat, lat_blk, GRID_W)
    states = s_ctx.astype(x_prompt.dtype)[:, :, None]
    return (yp.reshape(x_prompt.shape), ys.reshape(x_sample.shape), states[0], states[1])
```

```python
import functools
import itertools

import jax
import jax.numpy as jnp
from jax import lax
from jax.experimental import pallas as pl
from jax.experimental.pallas import tpu as pltpu

D_MODEL = 1024
RWKV_WIDTH = 512
HEAD_DIM = 64
N_HEADS = RWKV_WIDTH // HEAD_DIM
CONV_WIDTH = 512
CONV_K = 31
D_FF = 4 * D_MODEL
GRID_W = 64
LORA_W = 64
GATE_LORA = 128
RMS_EPS = 1e-6
LN_EPS = 1e-5
GN_EPS = 64e-5

LANES = 128
SUBLANES = 8
TM = 256
CHUNK = 64
N_CHUNK = TM // CHUNK
N_PAIR = RWKV_WIDTH // LANES
HALO = SUBLANES
CONV_PAD = 16
CONV_ROWS = 32
VMEM_LIMIT = 56 * 1024 * 1024

F32 = jnp.float32
BF16 = jnp.bfloat16
DECAY_SCALE = 0.6065306597126334


def _mm(a, b):
    return jnp.dot(a.astype(BF16), b.astype(BF16), preferred_element_type=F32)


def _mm_nt(a, b):
    return lax.dot_general(a.astype(BF16), b.astype(BF16), (((1,), (1,)), ((), ())),
                           preferred_element_type=F32)


def _split3(x):
    x1 = x.astype(BF16)
    r1 = x - x1.astype(F32)
    x2 = r1.astype(BF16)
    x3 = (r1 - x2.astype(F32)).astype(BF16)
    return x3, x2, x1


def _mm_split3(m, x):
    x3, x2, x1 = (jnp.dot(m, t, preferred_element_type=F32) for t in _split3(x))
    return x3 + x2 + x1


def _head_sum(x, bd):
    rows = x.shape[0]
    n_grp = x.shape[1] // LANES
    terms = []
    for p in range(n_grp):
        xp = x[:, p * LANES:(p + 1) * LANES]
        head = xp.astype(BF16)
        terms += [head, (xp - head.astype(F32)).astype(BF16)]
    prod = jnp.dot(jnp.concatenate(terms, axis=0), bd, preferred_element_type=F32)
    sums = [prod[(2 * p) * rows:(2 * p + 1) * rows] + prod[(2 * p + 1) * rows:(2 * p + 2) * rows]
            for p in range(n_grp)]
    return jnp.concatenate(sums, axis=1)


def _block_diag_ones():
    ri = lax.broadcasted_iota(jnp.int32, (LANES, LANES), 0)
    ci = lax.broadcasted_iota(jnp.int32, (LANES, LANES), 1)
    shift = HEAD_DIM.bit_length() - 1
    return jnp.where((ri >> shift) == (ci >> shift), 1.0, 0.0).astype(BF16)


def _rms(x):
    return x * lax.rsqrt(jnp.mean(x * x, axis=-1, keepdims=True) + RMS_EPS)


def _ada_kernel(c_ref, w_ref, b_ref, o_ref):
    c = c_ref[...]
    w = w_ref[...].astype(BF16)
    s3, s2, s1 = (jnp.dot(t, w, preferred_element_type=F32) for t in _split3(c * jax.nn.sigmoid(c)))
    o_ref[...] = s3 + s2 + s1 + b_ref[...]


def _ada(cond, w, b):
    n = w.shape[1]
    tn = 2048
    return pl.pallas_call(
        _ada_kernel,
        grid=(n // tn,),
        in_specs=[pl.BlockSpec((8, D_MODEL), lambda i: (0, 0)),
                  pl.BlockSpec((D_MODEL, tn), lambda i: (0, i)),
                  pl.BlockSpec((1, tn), lambda i: (0, i))],
        out_specs=pl.BlockSpec((8, tn), lambda i: (0, i)),
        out_shape=jax.ShapeDtypeStruct((8, n), F32),
        compiler_params=pltpu.CompilerParams(dimension_semantics=("arbitrary",),
                                             vmem_limit_bytes=VMEM_LIMIT),
        name="ada",
    )(cond, w, b.reshape(1, n))


_IN_SPLIT = (3 * RWKV_WIDTH, 2 * CONV_WIDTH, 2 * D_MODEL)
_LORA_SPLIT = (4 * LORA_W, GATE_LORA)


def _front_kernel(x_ref, mod_ref, g_ref, win_ref, wlo_ref, rkv_ref, glu_ref, gates_ref, lora_ref, lg_ref):
    mod = mod_ref[0]
    sh1 = mod[:, 0:D_MODEL]
    sc1 = mod[:, D_MODEL:2 * D_MODEL]
    xn = (_rms(x_ref[...]) * g_ref[...] * (1.0 + sc1) + sh1).astype(BF16)
    c0, c1, c2 = itertools.accumulate(_IN_SPLIT)
    dot = functools.partial(jnp.dot, preferred_element_type=F32)
    rkv_ref[...] = dot(xn, win_ref[:, 0:c0])
    glu = dot(xn, win_ref[:, c0:c1])
    glu_ref[...] = glu[:, 0:CONV_WIDTH] * jax.nn.sigmoid(glu[:, CONV_WIDTH:])
    gates_ref[...] = jax.nn.sigmoid(dot(xn, win_ref[:, c1:c2]))
    lora_ref[...] = dot(xn, wlo_ref[:, 0:_LORA_SPLIT[0]])
    lg_ref[...] = dot(xn, wlo_ref[:, _LORA_SPLIT[0]:])


def _front(x, mod3, mod_row, g, w_in, w_lora):
    n = x.shape[0]
    const = lambda i: (0, 0)
    tok = lambda i: (i, 0)
    widths = (_IN_SPLIT[0], CONV_WIDTH, _IN_SPLIT[2]) + _LORA_SPLIT
    return pl.pallas_call(
        _front_kernel,
        grid=(n // TM,),
        in_specs=[pl.BlockSpec((TM, D_MODEL), tok),
                  pl.BlockSpec((1, 1, 6 * D_MODEL), lambda i: (mod_row(i), 0, 0)),
                  pl.BlockSpec((1, D_MODEL), const),
                  pl.BlockSpec(w_in.shape, const),
                  pl.BlockSpec(w_lora.shape, const)],
        out_specs=[pl.BlockSpec((TM, w), tok) for w in widths],
        out_shape=[jax.ShapeDtypeStruct((n, w), F32) for w in widths],
        compiler_params=pltpu.CompilerParams(dimension_semantics=("arbitrary",),
                                             vmem_limit_bytes=VMEM_LIMIT),
        name="front",
    )(x, mod3, g, w_in, w_lora)


def _tile_coords(t, nseq, nblk):
    per_dir = nseq * nblk
    d = t // per_dir
    rem = t - d * per_dir
    s = rem // nblk
    return d, s, rem - s * nblk


def _wkv_masks(d):
    shift = CHUNK.bit_length() - 1
    ri = lax.broadcasted_iota(jnp.int32, (LANES, LANES), 0)
    ci = lax.broadcasted_iota(jnp.int32, (LANES, LANES), 1)
    same = (ri >> shift) == (ci >> shift)
    rel = (ci - ri) * (1 - 2 * d)
    levels = [((ri >> (lv + 1)) == (ci >> (lv + 1))) & (((ri >> lv) & 1) == 1 - d) & (((ci >> lv) & 1) == d)
              for lv in range(shift)]
    return same & (rel < 0), same & (rel <= 0), levels


def _run(*streams):
    streams = list(streams)
    while streams:
        streams = [g for g in streams if next(g, True) is None]


def _wkv_kernel(rkv_ref, prev_ref, next_ref, lora_ref, s0_ref, mup_ref, mun_ref, kk_ref, ka_ref,
                rk_ref, w0_ref, w2_ref, a0_ref, a2_ref,
                y_ref, bonus_ref, sout_ref,
                s_scr, r_scr, kd_scr, v_scr, aa_scr, b_scr, lw_scr, cl_scr, bn_scr,
                rhs_scr, out_scr, upd_scr, ti_scr, vs_scr, wc_scr, aab_scr, t_scr, *, nseq, nblk):
    g = pl.program_id(0)
    n_tile = 2 * nseq * nblk
    n_prob = N_CHUNK * N_PAIR
    d_a, _, j_a = _tile_coords(jnp.minimum(g, n_tile - 1), nseq, nblk)
    d_b, _, j_b = _tile_coords(jnp.maximum(g - 1, 0), nseq, nblk)
    masks_a = _wkv_masks(d_a)
    masks_b = _wkv_masks(d_b)
    shift = CHUNK.bit_length() - 1
    ri = lax.broadcasted_iota(jnp.int32, (LANES, LANES), 0)
    ci = lax.broadcasted_iota(jnp.int32, (LANES, LANES), 1)
    eye = (ri == ci).astype(F32)
    head0 = lax.broadcasted_iota(jnp.int32, (CHUNK, LANES), 1) < HEAD_DIM
    dot = functools.partial(jnp.dot, preferred_element_type=F32)

    def stack(t):
        return jnp.concatenate([jnp.where(head0, t, 0.0), jnp.where(head0, 0.0, t)], axis=0)

    def twice(t):
        return jnp.concatenate([t, t], axis=0)

    def chunk_rows(step, d):
        cidx = step + d * (N_CHUNK - 1 - 2 * step)
        return pl.ds(pl.multiple_of(cidx * CHUNK, CHUNK), CHUNK)

    def prepare(slot, d, j):
        blk = j + d * (nblk - 1 - 2 * j)
        u = rkv_ref[...]
        row = lax.broadcasted_iota(jnp.int32, (TM, 1), 0)
        prow = jnp.where(blk > 0, prev_ref[HALO - 1:HALO, :], 0.0)
        nrow = jnp.where(blk < nblk - 1, next_ref[0:1, :], 0.0)
        prev = jnp.where(row == 0, prow, pltpu.roll(u, 1, axis=0))
        nxt = jnp.where(row == TM - 1, nrow, pltpu.roll(u, TM - 1, axis=0))
        x = u + (prev - u) * mup_ref[...] + (nxt - u) * mun_ref[...]
        r = x[:, 0:RWKV_WIDTH]
        k = x[:, RWKV_WIDTH:2 * RWKV_WIDTH]
        v = x[:, 2 * RWKV_WIDTH:]
        r_scr[slot] = r
        v_scr[slot] = v
        yield
        bd = _block_diag_ones()
        kkr = k * kk_ref[...]
        kk = kkr * lax.rsqrt(jnp.maximum(_head_sum(kkr * kkr, bd), 1e-24))
        aa_scr[slot] = -kk
        yield
        lo = lora_ref[...]
        z = w0_ref[0] + _mm(jnp.tanh(lo[:, 0:2 * LORA_W]), w2_ref[0])
        lw = -DECAY_SCALE * jax.nn.sigmoid(z)
        lw_scr[slot, 0:TM, :] = lw
        a = jax.nn.sigmoid(a0_ref[0] + _mm(lo[:, 2 * LORA_W:4 * LORA_W], a2_ref[0]))
        kd = k * (1.0 + (a - 1.0) * ka_ref[...])
        kd_scr[slot] = kd
        b_scr[slot] = kk * a
        yield
        bn_scr[slot] = _head_sum(r * kd * rk_ref[...], bd) * v
        yield
        rt = lax.broadcasted_iota(jnp.int32, (TM, TM), 0)
        ct = lax.broadcasted_iota(jnp.int32, (TM, TM), 1)
        cum_m = jnp.where(((rt >> shift) == (ct >> shift)) & (((ct - rt) * (1 - 2 * d)) <= 0), 1.0, 0.0)
        cl_scr[slot] = _mm_split3(cum_m.astype(BF16), lw)
        yield

    def operands(step, slot, d, masks):
        strict, incl, levels = masks
        rows = chunk_rows(step, d)
        for p in range(N_PAIR):
            i = slot * n_prob + step * N_PAIR + p
            sl = slice(p * LANES, (p + 1) * LANES)
            lw = lw_scr[slot, rows, sl]
            cl = cl_scr[slot, rows, sl]
            tot = jnp.sum(lw, axis=0, keepdims=True)
            e_out = jnp.exp(-cl)
            e_end = jnp.exp(tot - cl)
            b_c = b_scr[slot, rows, sl]
            kd_c = kd_scr[slot, rows, sl]
            a_s = stack(aa_scr[slot, rows, sl] * jnp.exp(cl - lw)).astype(BF16)
            r_s = stack(r_scr[slot, rows, sl] * jnp.exp(cl)).astype(BF16)
            bk_s = jnp.concatenate([twice((b_c * e_out).astype(BF16)),
                                    twice((kd_c * e_out).astype(BF16))], axis=0)
            vs_scr[i] = stack(v_scr[slot, rows, sl]).astype(BF16)
            upd_scr[i] = jnp.concatenate([jnp.transpose(stack(b_c * e_end)),
                                          jnp.transpose(stack(kd_c * e_end))], axis=1).astype(BF16)
            wc_scr[i] = jnp.transpose(jnp.broadcast_to(jnp.exp(tot), (LANES, LANES)))
            ab = _mm_nt(jnp.concatenate([a_s, r_s], axis=0), bk_s)
            a_ab = jnp.where(strict, ab[0:LANES, 0:LANES], 0.0)
            a_ak = jnp.where(strict, ab[0:LANES, LANES:], 0.0)
            a_rb = jnp.where(incl, ab[LANES:, 0:LANES], 0.0)
            a_rk = jnp.where(incl, ab[LANES:, LANES:], 0.0)
            rhs_scr[i] = jnp.concatenate([a_s, a_ak.astype(BF16)], axis=1)
            out_scr[i] = jnp.concatenate([r_s, a_rb.astype(BF16), a_rk.astype(BF16)], axis=1)
            aab_scr[i] = a_ab
            t_scr[i] = eye + jnp.where(levels[0], a_ab, 0.0)
            yield

    def invert(steps, slot, masks):
        levels = masks[2]
        idx = [slot * n_prob + step * N_PAIR + p for step in steps for p in range(N_PAIR)]
        for lvl in levels[1:]:
            tbs = [t_scr[i].astype(BF16) for i in idx]
            ms = [dot(jnp.where(lvl, aab_scr[i], 0.0).astype(BF16), tb) for i, tb in zip(idx, tbs)]
            lw_scr[slot, TM:, 0:LANES] = ms[-1][0:SUBLANES, :]
            yield
            for i, tb, m in zip(idx, tbs, ms):
                t_scr[i] = t_scr[i] + dot(tb, m.astype(BF16))
            lw_scr[slot, TM:, 0:LANES] = ms[0][0:SUBLANES, :]
            yield
        for i in idx:
            t = t_scr[i]
            th = t.astype(BF16)
            ti_scr[i, 0] = th
            ti_scr[i, 1] = (t - th.astype(F32)).astype(BF16)

    def scan(step, slot, d):
        rows = chunk_rows(step, d)
        base = slot * n_prob + step * N_PAIR
        pairs = range(N_PAIR)
        sts = [s_scr[p] for p in pairs]
        stb = [st.astype(BF16) for st in sts]
        vss = [vs_scr[base + p] for p in pairs]
        rhs = [dot(rhs_scr[base + p], jnp.concatenate([stb[p], vss[p]], axis=0)) for p in pairs]
        yield
        pmb = []
        for p in pairs:
            rh = rhs[p].astype(BF16)
            rl = (rhs[p] - rh.astype(F32)).astype(BF16)
            th = ti_scr[base + p, 0]
            pm = dot(th, rh) + dot(th, rl) + dot(ti_scr[base + p, 1], rh)
            pmb.append(pm.astype(BF16))
        yield
        for p in pairs:
            s_scr[p] = wc_scr[base + p] * sts[p] + dot(
                upd_scr[base + p], jnp.concatenate([pmb[p], vss[p]], axis=0))
        yield
        for p in pairs:
            y_s = dot(out_scr[base + p], jnp.concatenate([stb[p], pmb[p], vss[p]], axis=0))
            y_ref[0, rows, p * LANES:(p + 1) * LANES] = y_s[0:CHUNK] + y_s[CHUNK:]
        yield

    first, second = range(0, N_CHUNK // 2), range(N_CHUNK // 2, N_CHUNK)
    chain = itertools.chain.from_iterable

    def stage_a(slot):
        return (prepare(slot, d_a, j_a),
                chain(operands(s, slot, d_a, masks_a) for s in first))

    def stage_b(slot, with_inverse=(), with_scan=()):
        @pl.when(j_b == 0)
        def _():
            zero = jnp.zeros((HEAD_DIM, HEAD_DIM), F32)
            for p in range(N_PAIR):
                top = jnp.concatenate([s0_ref[0, 0, 2 * p], zero], axis=1)
                bot = jnp.concatenate([zero, s0_ref[0, 0, 2 * p + 1]], axis=1)
                s_scr[p] = jnp.transpose(jnp.concatenate([top, bot], axis=0))

        bonus_ref[0] = bn_scr[slot]
        _run(invert(first, slot, masks_b), chain(operands(s, slot, d_b, masks_b) for s in second))
        _run(invert(second, slot, masks_b), chain(scan(s, slot, d_b) for s in first), *with_inverse)
        _run(chain(scan(s, slot, d_b) for s in second), *with_scan)

        @pl.when(j_b == nblk - 1)
        def _():
            for p in range(N_PAIR):
                s_vk = jnp.transpose(s_scr[p])
                sout_ref[0, 0, 2 * p] = s_vk[0:HEAD_DIM, 0:HEAD_DIM]
                sout_ref[0, 0, 2 * p + 1] = s_vk[HEAD_DIM:, HEAD_DIM:]

    @pl.when(g == 0)
    def _():
        prep, ops = stage_a(0)
        _run(prep)
        _run(ops)

    for parity in range(2):
        @pl.when((g > 0) & (g < n_tile) & (g % 2 == parity))
        def _():
            prep, ops = stage_a(parity)
            stage_b(1 - parity, with_inverse=(prep,), with_scan=(ops,))

    @pl.when(g == n_tile)
    def _():
        stage_b((n_tile - 1) % 2)


def _wkv(rkv, lora, s0, mup, mun, k_k, k_a, r_k, w0, w2p, a0, a2p, *, nseq, nblk):
    n = rkv.shape[0]
    hb = TM // HALO
    n_tile = 2 * nseq * nblk

    def coords(t):
        d, s, j = _tile_coords(t, nseq, nblk)
        return d, s, s * nblk + j + d * (nblk - 1 - 2 * j)

    ahead = lambda g: coords(jnp.minimum(g, n_tile - 1))
    behind = lambda g: coords(jnp.maximum(g - 1, 0))
    s0_seq = (lambda s: s) if s0.shape[1] > 1 else (lambda s: 0)
    const = lambda g: (0, 0)
    per_dir = lambda g: (ahead(g)[0], 0, 0)
    out_tile = lambda g: (behind(g)[0], behind(g)[2], 0)
    kern = functools.partial(_wkv_kernel, nseq=nseq, nblk=nblk)
    tok = pltpu.VMEM((2, TM, RWKV_WIDTH), F32)
    n_prob = 2 * N_CHUNK * N_PAIR
    return pl.pallas_call(
        kern,
        grid=(n_tile + 1,),
        in_specs=[
            pl.BlockSpec((TM, 3 * RWKV_WIDTH), lambda g: (ahead(g)[2], 0)),
            pl.BlockSpec((HALO, 3 * RWKV_WIDTH), lambda g: (jnp.maximum(ahead(g)[2] * hb - 1, 0), 0)),
            pl.BlockSpec((HALO, 3 * RWKV_WIDTH),
                         lambda g: (jnp.minimum((ahead(g)[2] + 1) * hb, n // HALO - 1), 0)),
            pl.BlockSpec((TM, 4 * LORA_W), lambda g: (ahead(g)[2], 0)),
            pl.BlockSpec((1, 1, N_HEADS, HEAD_DIM, HEAD_DIM),
                         lambda g: (behind(g)[0], s0_seq(behind(g)[1]), 0, 0, 0)),
            pl.BlockSpec((1, 3 * RWKV_WIDTH), const),
            pl.BlockSpec((1, 3 * RWKV_WIDTH), const),
            pl.BlockSpec((1, RWKV_WIDTH), const),
            pl.BlockSpec((1, RWKV_WIDTH), const),
            pl.BlockSpec((1, RWKV_WIDTH), const),
            pl.BlockSpec((1, 1, RWKV_WIDTH), per_dir),
            pl.BlockSpec((1, 2 * LORA_W, RWKV_WIDTH), per_dir),
            pl.BlockSpec((1, 1, RWKV_WIDTH), per_dir),
            pl.BlockSpec((1, 2 * LORA_W, RWKV_WIDTH), per_dir),
        ],
        out_specs=[
            pl.BlockSpec((1, TM, RWKV_WIDTH), out_tile),
            pl.BlockSpec((1, TM, RWKV_WIDTH), out_tile),
            pl.BlockSpec((1, 1, N_HEADS, HEAD_DIM, HEAD_DIM),
                         lambda g: (behind(g)[0], behind(g)[1], 0, 0, 0)),
        ],
        out_shape=[jax.ShapeDtypeStruct((2, n, RWKV_WIDTH), F32),
                   jax.ShapeDtypeStruct((2, n, RWKV_WIDTH), F32),
                   jax.ShapeDtypeStruct((2, nseq, N_HEADS, HEAD_DIM, HEAD_DIM), F32)],
        scratch_shapes=[pltpu.VMEM((N_PAIR, LANES, LANES), F32),
                        tok, tok, tok, tok, tok,
                        pltpu.VMEM((2, TM + SUBLANES, RWKV_WIDTH), F32),
                        tok, tok,
                        pltpu.VMEM((n_prob, LANES, 2 * LANES), BF16),
                        pltpu.VMEM((n_prob, LANES, 3 * LANES), BF16),
                        pltpu.VMEM((n_prob, LANES, 2 * LANES), BF16),
                        pltpu.VMEM((n_prob, 2, LANES, LANES), BF16),
                        pltpu.VMEM((n_prob, LANES, LANES), BF16),
                        pltpu.VMEM((n_prob, LANES, LANES), F32),
                        pltpu.VMEM((n_prob, LANES, LANES), F32),
                        pltpu.VMEM((n_prob, LANES, LANES), F32)],
        compiler_params=pltpu.CompilerParams(dimension_semantics=("arbitrary",),
                                             vmem_limit_bytes=VMEM_LIMIT),
        name="wkv",
    )(rkv, rkv, rkv, lora, s0, mup, mun, k_k, k_a, r_k, w0, w2p, a0, a2p)


def _mid_kernel(y_ref, bn_ref, lg_ref, glu_ref, gates_ref, x_ref, mod_ref, g2w_ref, lnxg_ref,
                lnxb_ref, wor_ref, cw_ref, cb_ref, clg_ref, clb_ref, woc_ref, wo_ref, n2g_ref,
                x1_ref, xn2_ref, pad_scr, *, seq):
    mod = mod_ref[0]
    g1 = mod[:, 2 * D_MODEL:3 * D_MODEL]
    sh2 = mod[:, 3 * D_MODEL:4 * D_MODEL]
    sc2 = mod[:, 4 * D_MODEL:5 * D_MODEL]

    bd = _block_diag_ones()
    y = y_ref[0] + y_ref[1]
    yc = y - _head_sum(y, bd) * (1.0 / HEAD_DIM)
    var = _head_sum(yc * yc, bd) * (1.0 / HEAD_DIM)
    yn = yc * lax.rsqrt(var + GN_EPS) * lnxg_ref[...] + lnxb_ref[...]
    g = _mm(jax.nn.sigmoid(lg_ref[...]), g2w_ref[...])
    y_r = _mm((yn + bn_ref[0] + bn_ref[1]) * g, wor_ref[...])

    uu = glu_ref[...]
    nseq = TM // seq
    zpad = jnp.zeros((nseq, CONV_PAD, CONV_WIDTH), F32)
    pad_scr[0, :, 0:CONV_PAD, :] = zpad
    pad_scr[0, :, CONV_PAD + seq:, :] = zpad
    for s in range(nseq):
        pad_scr[0, s, CONV_PAD:CONV_PAD + seq, :] = uu[s * seq:(s + 1) * seq]
    live = seq + 2 * CONV_PAD - SUBLANES
    for m in range(1, SUBLANES):
        pad_scr[m, :, 0:live, :] = pad_scr[0, :, m:m + live, :]
    first = CONV_PAD - CONV_K // 2
    parts = []
    for s in range(nseq):
        for r0 in range(0, seq, CONV_ROWS):
            acc = jnp.broadcast_to(cb_ref[...], (CONV_ROWS, CONV_WIDTH))
            for t in range(CONV_K):
                m = (first + t) % SUBLANES
                lo = r0 + first + t - m
                acc = acc + pad_scr[m, s, lo:lo + CONV_ROWS, :] * cw_ref[t:t + 1, :]
            parts.append(acc)
    cv = jnp.concatenate(parts, axis=0)
    cm = cv - jnp.mean(cv, axis=-1, keepdims=True)
    cvar = jnp.mean(cm * cm, axis=-1, keepdims=True)
    un = cm * lax.rsqrt(cvar + LN_EPS) * clg_ref[...] + clb_ref[...]
    y_c = _mm(un * jax.nn.sigmoid(un), woc_ref[...])

    gates = gates_ref[...]
    merged = gates[:, 0:D_MODEL] * y_r + gates[:, D_MODEL:] * y_c
    x1 = x_ref[...] + g1 * _mm(merged, wo_ref[...])
    x1_ref[...] = x1
    xn2_ref[...] = (_rms(x1) * n2g_ref[...] * (1.0 + sc2) + sh2).astype(BF16)


def _mid(y2, bn2, lg, glu, gates, x, mod3, mod_row, g2w, lnxg, lnxb, wor, cw, cb, clg, clb, woc,
         wo, n2g, *, seq):
    n = x.shape[0]
    const = lambda i: (0, 0)
    tok = lambda i: (i, 0)
    both = lambda i: (0, i, 0)

    def full(a):
        return pl.BlockSpec(a.shape, const)

    return pl.pallas_call(
        functools.partial(_mid_kernel, seq=seq),
        grid=(n // TM,),
        in_specs=[pl.BlockSpec((2, TM, RWKV_WIDTH), both),
                  pl.BlockSpec((2, TM, RWKV_WIDTH), both),
                  pl.BlockSpec((TM, GATE_LORA), tok),
                  pl.BlockSpec((TM, CONV_WIDTH), tok),
                  pl.BlockSpec((TM, 2 * D_MODEL), tok),
                  pl.BlockSpec((TM, D_MODEL), tok),
                  pl.BlockSpec((1, 1, 6 * D_MODEL), lambda i: (mod_row(i), 0, 0)),
                  full(g2w), full(lnxg), full(lnxb), full(wor), full(cw), full(cb), full(clg),
                  full(clb), full(woc), full(wo), full(n2g)],
        out_specs=[pl.BlockSpec((TM, D_MODEL), tok), pl.BlockSpec((TM, D_MODEL), tok)],
        out_shape=[jax.ShapeDtypeStruct((n, D_MODEL), F32),
                   jax.ShapeDtypeStruct((n, D_MODEL), BF16)],
        scratch_shapes=[pltpu.VMEM((SUBLANES, TM // seq, seq + 2 * CONV_PAD, CONV_WIDTH), F32)],
        compiler_params=pltpu.CompilerParams(dimension_semantics=("arbitrary",),
                                             vmem_limit_bytes=VMEM_LIMIT),
        name="mid",
    )(y2, bn2, lg, glu, gates, x, mod3, g2w, lnxg, lnxb, wor, cw, cb, clg, clb, woc, wo, n2g)


def _mlp_kernel(x1_ref, xn2_ref, mod_ref, w1_ref, w2_ref, fg_ref, o_ref):
    g2 = mod_ref[0][:, 5 * D_MODEL:6 * D_MODEL]
    h = jnp.maximum(jnp.dot(xn2_ref[...], w1_ref[...], preferred_element_type=F32), 0.0)
    x2 = x1_ref[...] + g2 * _mm(h * h, w2_ref[...])
    o_ref[...] = _rms(x2) * fg_ref[...]


def _mlp(x1, xn2, mod3, mod_row, w1, w2, fg):
    n = x1.shape[0]
    const = lambda i: (0, 0)
    tok = lambda i: (i, 0)
    return pl.pallas_call(
        _mlp_kernel,
        grid=(n // TM,),
        in_specs=[pl.BlockSpec((TM, D_MODEL), tok),
                  pl.BlockSpec((TM, D_MODEL), tok),
                  pl.BlockSpec((1, 1, 6 * D_MODEL), lambda i: (mod_row(i), 0, 0)),
                  pl.BlockSpec(w1.shape, const),
                  pl.BlockSpec(w2.shape, const),
                  pl.BlockSpec((1, D_MODEL), const)],
        out_specs=pl.BlockSpec((TM, D_MODEL), tok),
        out_shape=jax.ShapeDtypeStruct((n, D_MODEL), F32),
        compiler_params=pltpu.CompilerParams(dimension_semantics=("arbitrary",),
                                             vmem_limit_bytes=VMEM_LIMIT),
        name="mlp",
    )(x1, xn2, mod3, w1, w2, fg)


def _pad_dir(w):
    return (jnp.eye(2, dtype=w.dtype)[:, :, None, None] * w[:, None]).reshape(2, 2 * w.shape[1], w.shape[2])


def kernel(x_prompt, x_sample, state_fwd, state_bwd, c, c_ctx, ada_w, ada_b, norm1_g, norm2_g, w_in,
           mu_prev, mu_next, decay_w0, decay_w1, decay_w2, iclr_a0, iclr_a1, iclr_a2, gate_g1, gate_g2,
           k_k, k_a, r_k, lnx_g, lnx_b, w_out_rwkv, conv_w, conv_b, conv_ln_g, conv_ln_b, w_out_conv,
           w_o, mlp_w1, mlp_w2, final_g):
    n_ctx, seq_ctx, _ = x_prompt.shape
    n_lat, seq_lat, _ = x_sample.shape
    depth = ada_w.shape[0]
    assert seq_ctx == TM and seq_lat % TM == 0 and GRID_W == CHUNK and c.shape[0] + 1 <= 8
    lat_blk = seq_lat // TM
    row_ctx = lambda i: 0
    row_lat = lambda i: 1 + i // lat_blk

    xp = x_prompt.reshape(n_ctx * seq_ctx, D_MODEL)
    xs = x_sample.reshape(n_lat * seq_lat, D_MODEL)
    cond = jnp.concatenate([c_ctx[None, :], c, jnp.zeros((7 - c.shape[0], D_MODEL), F32)], axis=0)
    zero_state = jnp.zeros((2, 1, N_HEADS, HEAD_DIM, HEAD_DIM), F32)
    row = lambda a: a.reshape(1, -1)
    assert depth == 1
    lay = lambda a: a.reshape(a.shape[1:])
    bf = lambda a: lay(a).astype(BF16)

    mod3 = _ada(cond, lay(ada_w), lay(ada_b)).reshape(8, 1, 6 * D_MODEL)
    w_lora = jnp.concatenate([decay_w1[0, 0], decay_w1[0, 1], iclr_a1[0, 0], iclr_a1[0, 1], gate_g1[0]],
                             axis=1).astype(BF16)
    front_w = (norm1_g, bf(w_in), w_lora)
    wkv_w = (mu_prev, mu_next, k_k, k_a, row(r_k),
             decay_w0.reshape(2, 1, RWKV_WIDTH), _pad_dir(lay(decay_w2)).astype(BF16),
             iclr_a0.reshape(2, 1, RWKV_WIDTH), _pad_dir(lay(iclr_a2)).astype(BF16))
    mid_w = (bf(gate_g2), lnx_g, lnx_b, bf(w_out_rwkv), lay(conv_w), conv_b, conv_ln_g, conv_ln_b,
             bf(w_out_conv), bf(w_o), norm2_g)
    mlp_w = (bf(mlp_w1), bf(mlp_w2), row(final_g))
    s0_lat = jnp.stack([state_fwd[:, 0], state_bwd[:, 0]]).astype(F32)

    def layer(x, mod_row, s0, nseq, nblk, conv_seq):
        rkv, glu, gates, lora, lg = _front(x, mod3, mod_row, *front_w)
        y2, bn2, s_out = _wkv(rkv, lora, s0, *wkv_w, nseq=nseq, nblk=nblk)
        x1, xn2 = _mid(y2, bn2, lg, glu, gates, x, mod3, mod_row, *mid_w, seq=conv_seq)
        return _mlp(x1, xn2, mod3, mod_row, *mlp_w), s_out

    yp, s_ctx = layer(xp, row_ctx, zero_state, n_ctx, 1, seq_ctx)
    ys, _ = layer(xs, row_lat, s0_lat, n_lat, lat_blk, GRID_W)
    states = s_ctx.astype(x_prompt.dtype)[:, :, None]
    return (yp.reshape(x_prompt.shape), ys.reshape(x_sample.shape), states[0], states[1])
```

```python
import functools
import itertools

import jax
import jax.numpy as jnp
from jax import lax
from jax.experimental import pallas as pl
from jax.experimental.pallas import tpu as pltpu

D_MODEL = 1024
RWKV_WIDTH = 512
HEAD_DIM = 64
N_HEADS = RWKV_WIDTH // HEAD_DIM
CONV_WIDTH = 512
CONV_K = 31
D_FF = 4 * D_MODEL
GRID_W = 64
LORA_W = 64
GATE_LORA = 128
RMS_EPS = 1e-6
LN_EPS = 1e-5
GN_EPS = 64e-5

LANES = 128
SUBLANES = 8
TM = 256
TM_MLP = 512
CHUNK = 64
N_CHUNK = TM // CHUNK
N_PAIR = RWKV_WIDTH // LANES
HALO = SUBLANES
CONV_PAD = 16
CONV_ROWS = 32
VMEM_LIMIT = 56 * 1024 * 1024

F32 = jnp.float32
BF16 = jnp.bfloat16
DECAY_SCALE = 0.6065306597126334


def _mm(a, b):
    return jnp.dot(a.astype(BF16), b.astype(BF16), preferred_element_type=F32)


def _mm_nt(a, b):
    return lax.dot_general(a.astype(BF16), b.astype(BF16), (((1,), (1,)), ((), ())),
                           preferred_element_type=F32)


def _split3(x):
    x1 = x.astype(BF16)
    r1 = x - x1.astype(F32)
    x2 = r1.astype(BF16)
    x3 = (r1 - x2.astype(F32)).astype(BF16)
    return x3, x2, x1


def _mm_split3(m, x):
    x3, x2, x1 = (jnp.dot(m, t, preferred_element_type=F32) for t in _split3(x))
    return x3 + x2 + x1


def _head_sum(x, bd):
    rows = x.shape[0]
    n_grp = x.shape[1] // LANES
    terms = []
    for p in range(n_grp):
        xp = x[:, p * LANES:(p + 1) * LANES]
        head = xp.astype(BF16)
        terms += [head, (xp - head.astype(F32)).astype(BF16)]
    prod = jnp.dot(jnp.concatenate(terms, axis=0), bd, preferred_element_type=F32)
    sums = [prod[(2 * p) * rows:(2 * p + 1) * rows] + prod[(2 * p + 1) * rows:(2 * p + 2) * rows]
            for p in range(n_grp)]
    return jnp.concatenate(sums, axis=1)


def _block_diag_ones():
    ri = lax.broadcasted_iota(jnp.int32, (LANES, LANES), 0)
    ci = lax.broadcasted_iota(jnp.int32, (LANES, LANES), 1)
    shift = HEAD_DIM.bit_length() - 1
    return jnp.where((ri >> shift) == (ci >> shift), 1.0, 0.0).astype(BF16)


def _rms(x):
    return x * lax.rsqrt(jnp.mean(x * x, axis=-1, keepdims=True) + RMS_EPS)


def _ada_kernel(c_ref, w_ref, b_ref, o_ref):
    c = c_ref[...]
    w = w_ref[...].astype(BF16)
    s3, s2, s1 = (jnp.dot(t, w, preferred_element_type=F32) for t in _split3(c * jax.nn.sigmoid(c)))
    o_ref[...] = s3 + s2 + s1 + b_ref[...]


def _ada(cond, w, b):
    n = w.shape[1]
    tn = 2048
    return pl.pallas_call(
        _ada_kernel,
        grid=(n // tn,),
        in_specs=[pl.BlockSpec((8, D_MODEL), lambda i: (0, 0)),
                  pl.BlockSpec((D_MODEL, tn), lambda i: (0, i)),
                  pl.BlockSpec((1, tn), lambda i: (0, i))],
        out_specs=pl.BlockSpec((8, tn), lambda i: (0, i)),
        out_shape=jax.ShapeDtypeStruct((8, n), F32),
        compiler_params=pltpu.CompilerParams(dimension_semantics=("arbitrary",),
                                             vmem_limit_bytes=VMEM_LIMIT),
        name="ada",
    )(cond, w, b.reshape(1, n))


_IN_SPLIT = (3 * RWKV_WIDTH, 2 * CONV_WIDTH, 2 * D_MODEL)
_LORA_SPLIT = (4 * LORA_W, GATE_LORA)


def _front_kernel(x_ref, mod_ref, g_ref, win_ref, wlo_ref, rkv_ref, glu_ref, gates_ref, lora_ref, lg_ref):
    mod = mod_ref[0]
    sh1 = mod[:, 0:D_MODEL]
    sc1 = mod[:, D_MODEL:2 * D_MODEL]
    xn = (_rms(x_ref[...]) * g_ref[...] * (1.0 + sc1) + sh1).astype(BF16)
    c0, c1, c2 = itertools.accumulate(_IN_SPLIT)
    dot = functools.partial(jnp.dot, preferred_element_type=F32)
    rkv_ref[...] = dot(xn, win_ref[:, 0:c0])
    glu = dot(xn, win_ref[:, c0:c1])
    glu_ref[...] = glu[:, 0:CONV_WIDTH] * jax.nn.sigmoid(glu[:, CONV_WIDTH:])
    gates_ref[...] = jax.nn.sigmoid(dot(xn, win_ref[:, c1:c2]))
    lora_ref[...] = dot(xn, wlo_ref[:, 0:_LORA_SPLIT[0]])
    lg_ref[...] = dot(xn, wlo_ref[:, _LORA_SPLIT[0]:])


def _front(x, mod3, mod_row, g, w_in, w_lora):
    n = x.shape[0]
    const = lambda i: (0, 0)
    tok = lambda i: (i, 0)
    widths = (_IN_SPLIT[0], CONV_WIDTH, _IN_SPLIT[2]) + _LORA_SPLIT
    return pl.pallas_call(
        _front_kernel,
        grid=(n // TM,),
        in_specs=[pl.BlockSpec((TM, D_MODEL), tok),
                  pl.BlockSpec((1, 1, 6 * D_MODEL), lambda i: (mod_row(i), 0, 0)),
                  pl.BlockSpec((1, D_MODEL), const),
                  pl.BlockSpec(w_in.shape, const),
                  pl.BlockSpec(w_lora.shape, const)],
        out_specs=[pl.BlockSpec((TM, w), tok) for w in widths],
        out_shape=[jax.ShapeDtypeStruct((n, w), F32) for w in widths],
        compiler_params=pltpu.CompilerParams(dimension_semantics=("arbitrary",),
                                             vmem_limit_bytes=VMEM_LIMIT),
        name="front",
    )(x, mod3, g, w_in, w_lora)


def _tile_coords(t, nseq, nblk):
    per_dir = nseq * nblk
    d = t // per_dir
    rem = t - d * per_dir
    s = rem // nblk
    return d, s, rem - s * nblk


def _wkv_masks(d):
    shift = CHUNK.bit_length() - 1
    ri = lax.broadcasted_iota(jnp.int32, (LANES, LANES), 0)
    ci = lax.broadcasted_iota(jnp.int32, (LANES, LANES), 1)
    same = (ri >> shift) == (ci >> shift)
    rel = (ci - ri) * (1 - 2 * d)
    levels = [((ri >> (lv + 1)) == (ci >> (lv + 1))) & (((ri >> lv) & 1) == 1 - d) & (((ci >> lv) & 1) == d)
              for lv in range(shift)]
    return same & (rel < 0), same & (rel <= 0), levels


def _run(*streams):
    streams = list(streams)
    while streams:
        streams = [g for g in streams if next(g, True) is None]


def _wkv_kernel(rkv_ref, prev_ref, next_ref, lora_ref, s0_ref, mup_ref, mun_ref, kk_ref, ka_ref,
                rk_ref, w0_ref, w2_ref, a0_ref, a2_ref,
                y_ref, bonus_ref, sout_ref,
                s_scr, r_scr, kd_scr, v_scr, aa_scr, b_scr, lw_scr, cl_scr, bn_scr,
                rhs_scr, out_scr, upd_scr, ti_scr, vs_scr, wc_scr, aab_scr, t_scr, *, nseq, nblk):
    g = pl.program_id(0)
    n_tile = 2 * nseq * nblk
    n_prob = N_CHUNK * N_PAIR
    d_a, _, j_a = _tile_coords(jnp.minimum(g, n_tile - 1), nseq, nblk)
    d_b, _, j_b = _tile_coords(jnp.maximum(g - 1, 0), nseq, nblk)
    masks_a = _wkv_masks(d_a)
    masks_b = _wkv_masks(d_b)
    shift = CHUNK.bit_length() - 1
    ri = lax.broadcasted_iota(jnp.int32, (LANES, LANES), 0)
    ci = lax.broadcasted_iota(jnp.int32, (LANES, LANES), 1)
    eye = (ri == ci).astype(F32)
    head0 = lax.broadcasted_iota(jnp.int32, (CHUNK, LANES), 1) < HEAD_DIM
    dot = functools.partial(jnp.dot, preferred_element_type=F32)

    def stack(t):
        return jnp.concatenate([jnp.where(head0, t, 0.0), jnp.where(head0, 0.0, t)], axis=0)

    def twice(t):
        return jnp.concatenate([t, t], axis=0)

    def chunk_rows(step, d):
        cidx = step + d * (N_CHUNK - 1 - 2 * step)
        return pl.ds(pl.multiple_of(cidx * CHUNK, CHUNK), CHUNK)

    def prepare(slot, d, j):
        blk = j + d * (nblk - 1 - 2 * j)
        u = rkv_ref[...]
        row = lax.broadcasted_iota(jnp.int32, (TM, 1), 0)
        prow = jnp.where(blk > 0, prev_ref[HALO - 1:HALO, :], 0.0)
        nrow = jnp.where(blk < nblk - 1, next_ref[0:1, :], 0.0)
        prev = jnp.where(row == 0, prow, pltpu.roll(u, 1, axis=0))
        nxt = jnp.where(row == TM - 1, nrow, pltpu.roll(u, TM - 1, axis=0))
        x = u + (prev - u) * mup_ref[...] + (nxt - u) * mun_ref[...]
        r = x[:, 0:RWKV_WIDTH]
        k = x[:, RWKV_WIDTH:2 * RWKV_WIDTH]
        v = x[:, 2 * RWKV_WIDTH:]
        r_scr[slot] = r
        v_scr[slot] = v
        yield
        bd = _block_diag_ones()
        kkr = k * kk_ref[...]
        kk = kkr * lax.rsqrt(jnp.maximum(_head_sum(kkr * kkr, bd), 1e-24))
        aa_scr[slot] = -kk
        yield
        lo = lora_ref[...]
        z = w0_ref[0] + _mm(jnp.tanh(lo[:, 0:2 * LORA_W]), w2_ref[0])
        lw = -DECAY_SCALE * jax.nn.sigmoid(z)
        lw_scr[slot, 0:TM, :] = lw
        a = jax.nn.sigmoid(a0_ref[0] + _mm(lo[:, 2 * LORA_W:4 * LORA_W], a2_ref[0]))
        kd = k * (1.0 + (a - 1.0) * ka_ref[...])
        kd_scr[slot] = kd
        b_scr[slot] = kk * a
        yield
        bn_scr[slot] = _head_sum(r * kd * rk_ref[...], bd) * v
        yield
        rt = lax.broadcasted_iota(jnp.int32, (TM, TM), 0)
        ct = lax.broadcasted_iota(jnp.int32, (TM, TM), 1)
        cum_m = jnp.where(((rt >> shift) == (ct >> shift)) & (((ct - rt) * (1 - 2 * d)) <= 0), 1.0, 0.0)
        cl_scr[slot] = _mm_split3(cum_m.astype(BF16), lw)
        yield

    def operands(step, slot, d, masks):
        strict, incl, levels = masks
        rows = chunk_rows(step, d)
        for p in range(N_PAIR):
            i = slot * n_prob + step * N_PAIR + p
            sl = slice(p * LANES, (p + 1) * LANES)
            lw = lw_scr[slot, rows, sl]
            cl = cl_scr[slot, rows, sl]
            tot = jnp.sum(lw, axis=0, keepdims=True)
            e_out = jnp.exp(-cl)
            e_end = jnp.exp(tot - cl)
            b_c = b_scr[slot, rows, sl]
            kd_c = kd_scr[slot, rows, sl]
            a_s = stack(aa_scr[slot, rows, sl] * jnp.exp(cl - lw)).astype(BF16)
            r_s = stack(r_scr[slot, rows, sl] * jnp.exp(cl)).astype(BF16)
            bk_s = jnp.concatenate([twice((b_c * e_out).astype(BF16)),
                                    twice((kd_c * e_out).astype(BF16))], axis=0)
            vs_scr[i] = stack(v_scr[slot, rows, sl]).astype(BF16)
            upd_scr[i] = jnp.concatenate([jnp.transpose(stack(b_c * e_end)),
                                          jnp.transpose(stack(kd_c * e_end))], axis=1).astype(BF16)
            wc_scr[i] = jnp.transpose(jnp.broadcast_to(jnp.exp(tot), (LANES, LANES)))
            ab = _mm_nt(jnp.concatenate([a_s, r_s], axis=0), bk_s)
            a_ab = jnp.where(strict, ab[0:LANES, 0:LANES], 0.0)
            a_ak = jnp.where(strict, ab[0:LANES, LANES:], 0.0)
            a_rb = jnp.where(incl, ab[LANES:, 0:LANES], 0.0)
            a_rk = jnp.where(incl, ab[LANES:, LANES:], 0.0)
            rhs_scr[i] = jnp.concatenate([a_s, a_ak.astype(BF16)], axis=1)
            out_scr[i] = jnp.concatenate([r_s, a_rb.astype(BF16), a_rk.astype(BF16)], axis=1)
            aab_scr[i] = a_ab
            t_scr[i] = eye + jnp.where(levels[0], a_ab, 0.0)
            yield

    def invert(steps, slot, masks):
        levels = masks[2]
        idx = [slot * n_prob + step * N_PAIR + p for step in steps for p in range(N_PAIR)]
        for lvl in levels[1:]:
            tbs = [t_scr[i].astype(BF16) for i in idx]
            ms = [dot(jnp.where(lvl, aab_scr[i], 0.0).astype(BF16), tb) for i, tb in zip(idx, tbs)]
            lw_scr[slot, TM:, 0:LANES] = ms[-1][0:SUBLANES, :]
            yield
            for i, tb, m in zip(idx, tbs, ms):
                t_scr[i] = t_scr[i] + dot(tb, m.astype(BF16))
            lw_scr[slot, TM:, 0:LANES] = ms[0][0:SUBLANES, :]
            yield
        for i in idx:
            ti_scr[i] = t_scr[i].astype(BF16)

    def scan(step, slot, d):
        rows = chunk_rows(step, d)
        base = slot * n_prob + step * N_PAIR
        pairs = range(N_PAIR)
        sts = [s_scr[p] for p in pairs]
        stb = [st.astype(BF16) for st in sts]
        vss = [vs_scr[base + p] for p in pairs]
        rhs = [dot(rhs_scr[base + p], jnp.concatenate([stb[p], vss[p]], axis=0)) for p in pairs]
        yield
        pmb = []
        for p in pairs:
            rh = rhs[p].astype(BF16)
            rl = (rhs[p] - rh.astype(F32)).astype(BF16)
            pm = dot(ti_scr[base + p], jnp.concatenate([rh, rl], axis=1))
            pmb.append((pm[:, 0:LANES] + pm[:, LANES:]).astype(BF16))
        yield
        for p in pairs:
            s_scr[p] = wc_scr[base + p] * sts[p] + dot(
                upd_scr[base + p], jnp.concatenate([pmb[p], vss[p]], axis=0))
        yield
        for p in pairs:
            y_s = dot(out_scr[base + p], jnp.concatenate([stb[p], pmb[p], vss[p]], axis=0))
            y_ref[0, rows, p * LANES:(p + 1) * LANES] = y_s[0:CHUNK] + y_s[CHUNK:]
        yield

    first, second = range(0, N_CHUNK // 2), range(N_CHUNK // 2, N_CHUNK)
    chain = itertools.chain.from_iterable

    def stage_a(slot):
        return (prepare(slot, d_a, j_a),
                chain(operands(s, slot, d_a, masks_a) for s in first))

    def stage_b(slot, with_inverse=(), with_scan=()):
        @pl.when(j_b == 0)
        def _():
            zero = jnp.zeros((HEAD_DIM, HEAD_DIM), F32)
            for p in range(N_PAIR):
                top = jnp.concatenate([s0_ref[0, 0, 2 * p], zero], axis=1)
                bot = jnp.concatenate([zero, s0_ref[0, 0, 2 * p + 1]], axis=1)
                s_scr[p] = jnp.transpose(jnp.concatenate([top, bot], axis=0))

        bonus_ref[0] = bn_scr[slot]
        _run(invert(first, slot, masks_b), chain(operands(s, slot, d_b, masks_b) for s in second))
        _run(invert(second, slot, masks_b), chain(scan(s, slot, d_b) for s in first), *with_inverse)
        _run(chain(scan(s, slot, d_b) for s in second), *with_scan)

        @pl.when(j_b == nblk - 1)
        def _():
            for p in range(N_PAIR):
                s_vk = jnp.transpose(s_scr[p])
                sout_ref[0, 0, 2 * p] = s_vk[0:HEAD_DIM, 0:HEAD_DIM]
                sout_ref[0, 0, 2 * p + 1] = s_vk[HEAD_DIM:, HEAD_DIM:]

    @pl.when(g == 0)
    def _():
        prep, ops = stage_a(0)
        _run(prep)
        _run(ops)

    for parity in range(2):
        @pl.when((g > 0) & (g < n_tile) & (g % 2 == parity))
        def _():
            prep, ops = stage_a(parity)
            stage_b(1 - parity, with_inverse=(prep,), with_scan=(ops,))

    @pl.when(g == n_tile)
    def _():
        stage_b((n_tile - 1) % 2)


def _wkv(rkv, lora, s0, mup, mun, k_k, k_a, r_k, w0, w2p, a0, a2p, *, nseq, nblk):
    n = rkv.shape[0]
    hb = TM // HALO
    n_tile = 2 * nseq * nblk

    def coords(t):
        d, s, j = _tile_coords(t, nseq, nblk)
        return d, s, s * nblk + j + d * (nblk - 1 - 2 * j)

    ahead = lambda g: coords(jnp.minimum(g, n_tile - 1))
    behind = lambda g: coords(jnp.maximum(g - 1, 0))
    s0_seq = (lambda s: s) if s0.shape[1] > 1 else (lambda s: 0)
    const = lambda g: (0, 0)
    per_dir = lambda g: (ahead(g)[0], 0, 0)
    out_tile = lambda g: (behind(g)[0], behind(g)[2], 0)
    kern = functools.partial(_wkv_kernel, nseq=nseq, nblk=nblk)
    tok = pltpu.VMEM((2, TM, RWKV_WIDTH), F32)
    n_prob = 2 * N_CHUNK * N_PAIR
    return pl.pallas_call(
        kern,
        grid=(n_tile + 1,),
        in_specs=[
            pl.BlockSpec((TM, 3 * RWKV_WIDTH), lambda g: (ahead(g)[2], 0)),
            pl.BlockSpec((HALO, 3 * RWKV_WIDTH), lambda g: (jnp.maximum(ahead(g)[2] * hb - 1, 0), 0)),
            pl.BlockSpec((HALO, 3 * RWKV_WIDTH),
                         lambda g: (jnp.minimum((ahead(g)[2] + 1) * hb, n // HALO - 1), 0)),
            pl.BlockSpec((TM, 4 * LORA_W), lambda g: (ahead(g)[2], 0)),
            pl.BlockSpec((1, 1, N_HEADS, HEAD_DIM, HEAD_DIM),
                         lambda g: (behind(g)[0], s0_seq(behind(g)[1]), 0, 0, 0)),
            pl.BlockSpec((1, 3 * RWKV_WIDTH), const),
            pl.BlockSpec((1, 3 * RWKV_WIDTH), const),
            pl.BlockSpec((1, RWKV_WIDTH), const),
            pl.BlockSpec((1, RWKV_WIDTH), const),
            pl.BlockSpec((1, RWKV_WIDTH), const),
            pl.BlockSpec((1, 1, RWKV_WIDTH), per_dir),
            pl.BlockSpec((1, 2 * LORA_W, RWKV_WIDTH), per_dir),
            pl.BlockSpec((1, 1, RWKV_WIDTH), per_dir),
            pl.BlockSpec((1, 2 * LORA_W, RWKV_WIDTH), per_dir),
        ],
        out_specs=[
            pl.BlockSpec((1, TM, RWKV_WIDTH), out_tile),
            pl.BlockSpec((1, TM, RWKV_WIDTH), out_tile),
            pl.BlockSpec((1, 1, N_HEADS, HEAD_DIM, HEAD_DIM),
                         lambda g: (behind(g)[0], behind(g)[1], 0, 0, 0)),
        ],
        out_shape=[jax.ShapeDtypeStruct((2, n, RWKV_WIDTH), F32),
                   jax.ShapeDtypeStruct((2, n, RWKV_WIDTH), F32),
                   jax.ShapeDtypeStruct((2, nseq, N_HEADS, HEAD_DIM, HEAD_DIM), F32)],
        scratch_shapes=[pltpu.VMEM((N_PAIR, LANES, LANES), F32),
                        tok, tok, tok, tok, tok,
                        pltpu.VMEM((2, TM + SUBLANES, RWKV_WIDTH), F32),
                        tok, tok,
                        pltpu.VMEM((n_prob, LANES, 2 * LANES), BF16),
                        pltpu.VMEM((n_prob, LANES, 3 * LANES), BF16),
                        pltpu.VMEM((n_prob, LANES, 2 * LANES), BF16),
                        pltpu.VMEM((n_prob, LANES, LANES), BF16),
                        pltpu.VMEM((n_prob, LANES, LANES), BF16),
                        pltpu.VMEM((n_prob, LANES, LANES), F32),
                        pltpu.VMEM((n_prob, LANES, LANES), F32),
                        pltpu.VMEM((n_prob, LANES, LANES), F32)],
        compiler_params=pltpu.CompilerParams(dimension_semantics=("arbitrary",),
                                             vmem_limit_bytes=VMEM_LIMIT),
        name="wkv",
    )(rkv, rkv, rkv, lora, s0, mup, mun, k_k, k_a, r_k, w0, w2p, a0, a2p)


def _mid_kernel(y_ref, bn_ref, lg_ref, glu_ref, gates_ref, x_ref, mod_ref, g2w_ref, lnxg_ref,
                lnxb_ref, wor_ref, cw_ref, cb_ref, clg_ref, clb_ref, woc_ref, wo_ref, n2g_ref,
                x1_ref, xn2_ref, pad_scr, *, seq):
    mod = mod_ref[0]
    g1 = mod[:, 2 * D_MODEL:3 * D_MODEL]
    sh2 = mod[:, 3 * D_MODEL:4 * D_MODEL]
    sc2 = mod[:, 4 * D_MODEL:5 * D_MODEL]

    bd = _block_diag_ones()
    y = y_ref[0] + y_ref[1]
    yc = y - _head_sum(y, bd) * (1.0 / HEAD_DIM)
    var = _head_sum(yc * yc, bd) * (1.0 / HEAD_DIM)
    yn = yc * lax.rsqrt(var + GN_EPS) * lnxg_ref[...] + lnxb_ref[...]
    g = _mm(jax.nn.sigmoid(lg_ref[...]), g2w_ref[...])
    y_r = _mm((yn + bn_ref[0] + bn_ref[1]) * g, wor_ref[...])

    uu = glu_ref[...]
    nseq = TM // seq
    zpad = jnp.zeros((nseq, CONV_PAD, CONV_WIDTH), F32)
    pad_scr[0, :, 0:CONV_PAD, :] = zpad
    pad_scr[0, :, CONV_PAD + seq:, :] = zpad
    for s in range(nseq):
        pad_scr[0, s, CONV_PAD:CONV_PAD + seq, :] = uu[s * seq:(s + 1) * seq]
    live = seq + 2 * CONV_PAD - SUBLANES
    for m in range(1, SUBLANES):
        pad_scr[m, :, 0:live, :] = pad_scr[0, :, m:m + live, :]
    first = CONV_PAD - CONV_K // 2
    parts = []
    for s in range(nseq):
        for r0 in range(0, seq, CONV_ROWS):
            acc = jnp.broadcast_to(cb_ref[...], (CONV_ROWS, CONV_WIDTH))
            for t in range(CONV_K):
                m = (first + t) % SUBLANES
                lo = r0 + first + t - m
                acc = acc + pad_scr[m, s, lo:lo + CONV_ROWS, :] * cw_ref[t:t + 1, :]
            parts.append(acc)
    cv = jnp.concatenate(parts, axis=0)
    cm = cv - jnp.mean(cv, axis=-1, keepdims=True)
    cvar = jnp.mean(cm * cm, axis=-1, keepdims=True)
    un = cm * lax.rsqrt(cvar + LN_EPS) * clg_ref[...] + clb_ref[...]
    y_c = _mm(un * jax.nn.sigmoid(un), woc_ref[...])

    gates = gates_ref[...]
    merged = gates[:, 0:D_MODEL] * y_r + gates[:, D_MODEL:] * y_c
    x1 = x_ref[...] + g1 * _mm(merged, wo_ref[...])
    x1_ref[...] = x1
    xn2_ref[...] = (_rms(x1) * n2g_ref[...] * (1.0 + sc2) + sh2).astype(BF16)


def _mid(y2, bn2, lg, glu, gates, x, mod3, mod_row, g2w, lnxg, lnxb, wor, cw, cb, clg, clb, woc,
         wo, n2g, *, seq):
    n = x.shape[0]
    const = lambda i: (0, 0)
    tok = lambda i: (i, 0)
    both = lambda i: (0, i, 0)

    def full(a):
        return pl.BlockSpec(a.shape, const)

    return pl.pallas_call(
        functools.partial(_mid_kernel, seq=seq),
        grid=(n // TM,),
        in_specs=[pl.BlockSpec((2, TM, RWKV_WIDTH), both),
                  pl.BlockSpec((2, TM, RWKV_WIDTH), both),
                  pl.BlockSpec((TM, GATE_LORA), tok),
                  pl.BlockSpec((TM, CONV_WIDTH), tok),
                  pl.BlockSpec((TM, 2 * D_MODEL), tok),
                  pl.BlockSpec((TM, D_MODEL), tok),
                  pl.BlockSpec((1, 1, 6 * D_MODEL), lambda i: (mod_row(i), 0, 0)),
                  full(g2w), full(lnxg), full(lnxb), full(wor), full(cw), full(cb), full(clg),
                  full(clb), full(woc), full(wo), full(n2g)],
        out_specs=[pl.BlockSpec((TM, D_MODEL), tok), pl.BlockSpec((TM, D_MODEL), tok)],
        out_shape=[jax.ShapeDtypeStruct((n, D_MODEL), F32),
                   jax.ShapeDtypeStruct((n, D_MODEL), BF16)],
        scratch_shapes=[pltpu.VMEM((SUBLANES, TM // seq, seq + 2 * CONV_PAD, CONV_WIDTH), F32)],
        compiler_params=pltpu.CompilerParams(dimension_semantics=("arbitrary",),
                                             vmem_limit_bytes=VMEM_LIMIT),
        name="mid",
    )(y2, bn2, lg, glu, gates, x, mod3, g2w, lnxg, lnxb, wor, cw, cb, clg, clb, woc, wo, n2g)


def _mlp_kernel(x1_ref, xn2_ref, mod_ref, w1_ref, w2_ref, fg_ref, o_ref):
    g2 = mod_ref[0][:, 5 * D_MODEL:6 * D_MODEL]
    h = jnp.maximum(jnp.dot(xn2_ref[...], w1_ref[...], preferred_element_type=F32), 0.0)
    x2 = x1_ref[...] + g2 * _mm(h * h, w2_ref[...])
    o_ref[...] = _rms(x2) * fg_ref[...]


def _mlp(x1, xn2, mod3, mod_row, w1, w2, fg):
    n = x1.shape[0]
    const = lambda i: (0, 0)
    tok = lambda i: (i, 0)
    once = pl.Buffered(1)
    return pl.pallas_call(
        _mlp_kernel,
        grid=(n // TM_MLP,),
        in_specs=[pl.BlockSpec((TM_MLP, D_MODEL), tok),
                  pl.BlockSpec((TM_MLP, D_MODEL), tok),
                  pl.BlockSpec((1, 1, 6 * D_MODEL), lambda i: (mod_row(i * (TM_MLP // TM)), 0, 0)),
                  pl.BlockSpec(w1.shape, const, pipeline_mode=once),
                  pl.BlockSpec(w2.shape, const, pipeline_mode=once),
                  pl.BlockSpec((1, D_MODEL), const)],
        out_specs=pl.BlockSpec((TM_MLP, D_MODEL), tok),
        out_shape=jax.ShapeDtypeStruct((n, D_MODEL), F32),
        compiler_params=pltpu.CompilerParams(dimension_semantics=("arbitrary",),
                                             vmem_limit_bytes=VMEM_LIMIT),
        name="mlp",
    )(x1, xn2, mod3, w1, w2, fg)


def _pad_dir(w):
    return (jnp.eye(2, dtype=w.dtype)[:, :, None, None] * w[:, None]).reshape(2, 2 * w.shape[1], w.shape[2])


def kernel(x_prompt, x_sample, state_fwd, state_bwd, c, c_ctx, ada_w, ada_b, norm1_g, norm2_g, w_in,
           mu_prev, mu_next, decay_w0, decay_w1, decay_w2, iclr_a0, iclr_a1, iclr_a2, gate_g1, gate_g2,
           k_k, k_a, r_k, lnx_g, lnx_b, w_out_rwkv, conv_w, conv_b, conv_ln_g, conv_ln_b, w_out_conv,
           w_o, mlp_w1, mlp_w2, final_g):
    n_ctx, seq_ctx, _ = x_prompt.shape
    n_lat, seq_lat, _ = x_sample.shape
    depth = ada_w.shape[0]
    assert seq_ctx == TM and seq_lat % TM == 0 and GRID_W == CHUNK and c.shape[0] + 1 <= 8
    lat_blk = seq_lat // TM
    row_ctx = lambda i: 0
    row_lat = lambda i: 1 + i // lat_blk

    xp = x_prompt.reshape(n_ctx * seq_ctx, D_MODEL)
    xs = x_sample.reshape(n_lat * seq_lat, D_MODEL)
    cond = jnp.concatenate([c_ctx[None, :], c, jnp.zeros((7 - c.shape[0], D_MODEL), F32)], axis=0)
    zero_state = jnp.zeros((2, 1, N_HEADS, HEAD_DIM, HEAD_DIM), F32)
    row = lambda a: a.reshape(1, -1)
    assert depth == 1
    lay = lambda a: a.reshape(a.shape[1:])
    bf = lambda a: lay(a).astype(BF16)

    mod3 = _ada(cond, lay(ada_w), lay(ada_b)).reshape(8, 1, 6 * D_MODEL)
    w_lora = jnp.concatenate([decay_w1[0, 0], decay_w1[0, 1], iclr_a1[0, 0], iclr_a1[0, 1], gate_g1[0]],
                             axis=1).astype(BF16)
    front_w = (norm1_g, bf(w_in), w_lora)
    wkv_w = (mu_prev, mu_next, k_k, k_a, row(r_k),
             decay_w0.reshape(2, 1, RWKV_WIDTH), _pad_dir(lay(decay_w2)).astype(BF16),
             iclr_a0.reshape(2, 1, RWKV_WIDTH), _pad_dir(lay(iclr_a2)).astype(BF16))
    mid_w = (bf(gate_g2), lnx_g, lnx_b, bf(w_out_rwkv), lay(conv_w), conv_b, conv_ln_g, conv_ln_b,
             bf(w_out_conv), bf(w_o), norm2_g)
    mlp_w = (bf(mlp_w1), bf(mlp_w2), row(final_g))
    s0_lat = jnp.stack([state_fwd[:, 0], state_bwd[:, 0]]).astype(F32)

    def layer(x, mod_row, s0, nseq, nblk, conv_seq):
        rkv, glu, gates, lora, lg = _front(x, mod3, mod_row, *front_w)
        y2, bn2, s_out = _wkv(rkv, lora, s0, *wkv_w, nseq=nseq, nblk=nblk)
        x1, xn2 = _mid(y2, bn2, lg, glu, gates, x, mod3, mod_row, *mid_w, seq=conv_seq)
        return _mlp(x1, xn2, mod3, mod_row, *mlp_w), s_out

    yp, s_ctx = layer(xp, row_ctx, zero_state, n_ctx, 1, seq_ctx)
    ys, _ = layer(xs, row_lat, s0_lat, n_lat, lat_blk, GRID_W)
    states = s_ctx.astype(x_prompt.dtype)[:, :, None]
    return (yp.reshape(x_prompt.shape), ys.reshape(x_sample.shape), states[0], states[1])
```

```python
import functools
import itertools

import jax
import jax.numpy as jnp
from jax import lax
from jax.experimental import pallas as pl
from jax.experimental.pallas import tpu as pltpu

D_MODEL = 1024
RWKV_WIDTH = 512
HEAD_DIM = 64
N_HEADS = RWKV_WIDTH // HEAD_DIM
CONV_WIDTH = 512
CONV_K = 31
D_FF = 4 * D_MODEL
GRID_W = 64
LORA_W = 64
GATE_LORA = 128
RMS_EPS = 1e-6
LN_EPS = 1e-5
GN_EPS = 64e-5

LANES = 128
SUBLANES = 8
TM = 256
TM_MLP = 512
FRONT_COLS = 512
CHUNK = 64
N_CHUNK = TM // CHUNK
N_PAIR = RWKV_WIDTH // LANES
HALO = SUBLANES
CONV_PAD = 16
CONV_ROWS = 32
VMEM_LIMIT = 56 * 1024 * 1024

F32 = jnp.float32
BF16 = jnp.bfloat16
DECAY_SCALE = 0.6065306597126334


def _mm(a, b):
    return jnp.dot(a.astype(BF16), b.astype(BF16), preferred_element_type=F32)


def _mm_nt(a, b):
    return lax.dot_general(a.astype(BF16), b.astype(BF16), (((1,), (1,)), ((), ())),
                           preferred_element_type=F32)


def _split3(x):
    x1 = x.astype(BF16)
    r1 = x - x1.astype(F32)
    x2 = r1.astype(BF16)
    x3 = (r1 - x2.astype(F32)).astype(BF16)
    return x3, x2, x1


def _mm_split3(m, x):
    x3, x2, x1 = (jnp.dot(m, t, preferred_element_type=F32) for t in _split3(x))
    return x3 + x2 + x1


def _head_sum(x, bd):
    rows = x.shape[0]
    n_grp = x.shape[1] // LANES
    terms = []
    for p in range(n_grp):
        xp = x[:, p * LANES:(p + 1) * LANES]
        head = xp.astype(BF16)
        terms += [head, (xp - head.astype(F32)).astype(BF16)]
    prod = jnp.dot(jnp.concatenate(terms, axis=0), bd, preferred_element_type=F32)
    sums = [prod[(2 * p) * rows:(2 * p + 1) * rows] + prod[(2 * p + 1) * rows:(2 * p + 2) * rows]
            for p in range(n_grp)]
    return jnp.concatenate(sums, axis=1)


def _block_diag_ones():
    ri = lax.broadcasted_iota(jnp.int32, (LANES, LANES), 0)
    ci = lax.broadcasted_iota(jnp.int32, (LANES, LANES), 1)
    shift = HEAD_DIM.bit_length() - 1
    return jnp.where((ri >> shift) == (ci >> shift), 1.0, 0.0).astype(BF16)


def _rms(x):
    return x * lax.rsqrt(jnp.mean(x * x, axis=-1, keepdims=True) + RMS_EPS)


def _ada_kernel(c_ref, w_ref, b_ref, o_ref):
    c = c_ref[...]
    w = w_ref[...].astype(BF16)
    s3, s2, s1 = (jnp.dot(t, w, preferred_element_type=F32) for t in _split3(c * jax.nn.sigmoid(c)))
    o_ref[...] = s3 + s2 + s1 + b_ref[...]


def _ada(cond, w, b):
    n = w.shape[1]
    tn = 2048
    return pl.pallas_call(
        _ada_kernel,
        grid=(n // tn,),
        in_specs=[pl.BlockSpec((8, D_MODEL), lambda i: (0, 0)),
                  pl.BlockSpec((D_MODEL, tn), lambda i: (0, i)),
                  pl.BlockSpec((1, tn), lambda i: (0, i))],
        out_specs=pl.BlockSpec((8, tn), lambda i: (0, i)),
        out_shape=jax.ShapeDtypeStruct((8, n), F32),
        compiler_params=pltpu.CompilerParams(dimension_semantics=("arbitrary",),
                                             vmem_limit_bytes=VMEM_LIMIT),
        name="ada",
    )(cond, w, b.reshape(1, n))


_IN_SPLIT = (3 * RWKV_WIDTH, 2 * CONV_WIDTH, 2 * D_MODEL)
_LORA_SPLIT = (4 * LORA_W, GATE_LORA)


def _run(*streams):
    streams = list(streams)
    while streams:
        streams = [g for g in streams if next(g, True) is None]


def _conv_module(uu, cw_ref, cb_ref, clg_ref, clb_ref, pad_scr, out_ref, seq, row0):
    nseq = TM // seq
    zpad = jnp.zeros((nseq, CONV_PAD, CONV_WIDTH), F32)
    pad_scr[0, :, 0:CONV_PAD, :] = zpad
    pad_scr[0, :, CONV_PAD + seq:2 * CONV_PAD + seq, :] = zpad
    for s in range(nseq):
        pad_scr[0, s, CONV_PAD:CONV_PAD + seq, :] = uu[s * seq:(s + 1) * seq]
    live = seq + 2 * CONV_PAD - SUBLANES
    for m in range(1, SUBLANES):
        pad_scr[m, :, 0:live, :] = pad_scr[0, :, m:m + live, :]
    yield
    first = CONV_PAD - CONV_K // 2
    for s in range(nseq):
        for r0 in range(0, seq, CONV_ROWS):
            groups = (CONV_ROWS // SUBLANES, SUBLANES, CONV_WIDTH)
            acc = jnp.broadcast_to(cb_ref[...], groups)
            for t in range(CONV_K):
                m = (first + t) % SUBLANES
                rows = pl.ds(pl.multiple_of(row0 + (r0 + first + t - m), SUBLANES), CONV_ROWS)
                acc = acc + pad_scr[m, s, rows, :].reshape(groups) * cw_ref[t][None]
            acc = acc.reshape(CONV_ROWS, CONV_WIDTH)
            cm = acc - jnp.mean(acc, axis=-1, keepdims=True)
            cvar = jnp.mean(cm * cm, axis=-1, keepdims=True)
            un = cm * lax.rsqrt(cvar + LN_EPS) * clg_ref[...] + clb_ref[...]
            out_ref[s * seq + r0:s * seq + r0 + CONV_ROWS, :] = (un * jax.nn.sigmoid(un)).astype(out_ref.dtype)
            yield


def _front_kernel(zero_ref, x_ref, mod_ref, g_ref, win_ref, wlo_ref, cw_ref, cb_ref, clg_ref, clb_ref,
                  rkv_ref, conv_ref, gates_ref, lora_ref, lg_ref, pad_scr, *, seq):
    mod = mod_ref[0]
    sh1 = mod[:, 0:D_MODEL]
    sc1 = mod[:, D_MODEL:2 * D_MODEL]
    xn = (_rms(x_ref[...]) * g_ref[...] * (1.0 + sc1) + sh1).astype(BF16)
    c0, c1, c2 = itertools.accumulate(_IN_SPLIT)
    dot = functools.partial(jnp.dot, preferred_element_type=F32)
    glu = dot(xn, win_ref[:, c0:c1])
    uu = glu[:, 0:CONV_WIDTH] * jax.nn.sigmoid(glu[:, CONV_WIDTH:])

    spare = seq + 2 * CONV_PAD

    def anchor(block):
        pad_scr[0, 0, spare:, 0:LANES] = block[0:SUBLANES, 0:LANES]

    def projections():
        for lo in range(0, c0, FRONT_COLS):
            block = dot(xn, win_ref[:, lo:lo + FRONT_COLS])
            rkv_ref[:, lo:lo + FRONT_COLS] = block
            anchor(block)
            yield
        for lo in range(0, c2 - c1, FRONT_COLS):
            block = dot(xn, win_ref[:, c1 + lo:c1 + lo + FRONT_COLS])
            gates_ref[:, lo:lo + FRONT_COLS] = jax.nn.sigmoid(block)
            anchor(block)
            yield
        lora_ref[...] = dot(xn, wlo_ref[:, 0:_LORA_SPLIT[0]])
        lg_ref[...] = dot(xn, wlo_ref[:, _LORA_SPLIT[0]:])
        yield

    _run(projections(),
         _conv_module(uu, cw_ref, cb_ref, clg_ref, clb_ref, pad_scr, conv_ref, seq, zero_ref[0]))


def _front(x, mod3, mod_row, g, w_in, w_lora, cw, cb, clg, clb, *, seq):
    n = x.shape[0]
    const = lambda i: (0, 0)
    tok = lambda i: (i, 0)
    outs = ((_IN_SPLIT[0], F32), (CONV_WIDTH, BF16), (_IN_SPLIT[2], F32),
            (_LORA_SPLIT[0], F32), (_LORA_SPLIT[1], F32))
    return pl.pallas_call(
        functools.partial(_front_kernel, seq=seq),
        grid=(n // TM,),
        in_specs=[pl.BlockSpec(memory_space=pltpu.SMEM),
                  pl.BlockSpec((TM, D_MODEL), tok),
                  pl.BlockSpec((1, 1, 6 * D_MODEL), lambda i: (mod_row(i), 0, 0)),
                  pl.BlockSpec((1, D_MODEL), const)]
                 + [pl.BlockSpec(a.shape, lambda i, nd=a.ndim: (0,) * nd) for a in (w_in, w_lora, cw, cb, clg, clb)],
        out_specs=[pl.BlockSpec((TM, w), tok) for w, _ in outs],
        out_shape=[jax.ShapeDtypeStruct((n, w), dt) for w, dt in outs],
        scratch_shapes=[pltpu.VMEM((SUBLANES, TM // seq, seq + 2 * CONV_PAD + SUBLANES, CONV_WIDTH), F32)],
        compiler_params=pltpu.CompilerParams(dimension_semantics=("arbitrary",),
                                             vmem_limit_bytes=VMEM_LIMIT),
        name="front",
    )(jnp.zeros((1,), jnp.int32), x, mod3, g, w_in, w_lora, cw, cb, clg, clb)


def _tile_coords(t, nseq, nblk):
    per_dir = nseq * nblk
    d = t // per_dir
    rem = t - d * per_dir
    s = rem // nblk
    return d, s, rem - s * nblk


def _wkv_masks(d):
    shift = CHUNK.bit_length() - 1
    ri = lax.broadcasted_iota(jnp.int32, (LANES, LANES), 0)
    ci = lax.broadcasted_iota(jnp.int32, (LANES, LANES), 1)
    same = (ri >> shift) == (ci >> shift)
    rel = (ci - ri) * (1 - 2 * d)
    levels = [((ri >> (lv + 1)) == (ci >> (lv + 1))) & (((ri >> lv) & 1) == 1 - d) & (((ci >> lv) & 1) == d)
              for lv in range(shift)]
    return same & (rel < 0), same & (rel <= 0), levels


def _wkv_kernel(rkv_ref, prev_ref, next_ref, lora_ref, s0_ref, mup_ref, mun_ref, kk_ref, ka_ref,
                rk_ref, w0_ref, w2_ref, a0_ref, a2_ref,
                y_ref, bonus_ref, sout_ref,
                s_scr, r_scr, kd_scr, v_scr, aa_scr, b_scr, lw_scr, cl_scr, bn_scr,
                rhs_scr, out_scr, upd_scr, ti_scr, vs_scr, wc_scr, aab_scr, t_scr, *, nseq, nblk):
    g = pl.program_id(0)
    n_tile = 2 * nseq * nblk
    n_prob = N_CHUNK * N_PAIR
    d_a, _, j_a = _tile_coords(jnp.minimum(g, n_tile - 1), nseq, nblk)
    d_b, _, j_b = _tile_coords(jnp.maximum(g - 1, 0), nseq, nblk)
    masks_a = _wkv_masks(d_a)
    masks_b = _wkv_masks(d_b)
    shift = CHUNK.bit_length() - 1
    ri = lax.broadcasted_iota(jnp.int32, (LANES, LANES), 0)
    ci = lax.broadcasted_iota(jnp.int32, (LANES, LANES), 1)
    eye = (ri == ci).astype(F32)
    head0 = lax.broadcasted_iota(jnp.int32, (CHUNK, LANES), 1) < HEAD_DIM
    dot = functools.partial(jnp.dot, preferred_element_type=F32)

    def stack(t):
        return jnp.concatenate([jnp.where(head0, t, 0.0), jnp.where(head0, 0.0, t)], axis=0)

    def twice(t):
        return jnp.concatenate([t, t], axis=0)

    def chunk_rows(step, d):
        cidx = step + d * (N_CHUNK - 1 - 2 * step)
        return pl.ds(pl.multiple_of(cidx * CHUNK, CHUNK), CHUNK)

    def prepare(slot, d, j):
        blk = j + d * (nblk - 1 - 2 * j)
        u = rkv_ref[...]
        row = lax.broadcasted_iota(jnp.int32, (TM, 1), 0)
        prow = jnp.where(blk > 0, prev_ref[HALO - 1:HALO, :], 0.0)
        nrow = jnp.where(blk < nblk - 1, next_ref[0:1, :], 0.0)
        prev = jnp.where(row == 0, prow, pltpu.roll(u, 1, axis=0))
        nxt = jnp.where(row == TM - 1, nrow, pltpu.roll(u, TM - 1, axis=0))
        x = u + (prev - u) * mup_ref[...] + (nxt - u) * mun_ref[...]
        r = x[:, 0:RWKV_WIDTH]
        k = x[:, RWKV_WIDTH:2 * RWKV_WIDTH]
        v = x[:, 2 * RWKV_WIDTH:]
        r_scr[slot] = r
        v_scr[slot] = v
        yield
        bd = _block_diag_ones()
        kkr = k * kk_ref[...]
        kk = kkr * lax.rsqrt(jnp.maximum(_head_sum(kkr * kkr, bd), 1e-24))
        aa_scr[slot] = -kk
        yield
        lo = lora_ref[...]
        z = w0_ref[0] + _mm(jnp.tanh(lo[:, 0:2 * LORA_W]), w2_ref[0])
        lw = -DECAY_SCALE * jax.nn.sigmoid(z)
        lw_scr[slot, 0:TM, :] = lw
        a = jax.nn.sigmoid(a0_ref[0] + _mm(lo[:, 2 * LORA_W:4 * LORA_W], a2_ref[0]))
        kd = k * (1.0 + (a - 1.0) * ka_ref[...])
        kd_scr[slot] = kd
        b_scr[slot] = kk * a
        yield
        bn_scr[slot] = _head_sum(r * kd * rk_ref[...], bd) * v
        yield
        rt = lax.broadcasted_iota(jnp.int32, (TM, TM), 0)
        ct = lax.broadcasted_iota(jnp.int32, (TM, TM), 1)
        cum_m = jnp.where(((rt >> shift) == (ct >> shift)) & (((ct - rt) * (1 - 2 * d)) <= 0), 1.0, 0.0)
        cl_scr[slot] = _mm_split3(cum_m.astype(BF16), lw)
        yield

    def operands(step, slot, d, masks):
        strict, incl, levels = masks
        rows = chunk_rows(step, d)
        for p in range(N_PAIR):
            i = slot * n_prob + step * N_PAIR + p
            sl = slice(p * LANES, (p + 1) * LANES)
            lw = lw_scr[slot, rows, sl]
            cl = cl_scr[slot, rows, sl]
            tot = jnp.sum(lw, axis=0, keepdims=True)
            e_out = jnp.exp(-cl)
            e_end = jnp.exp(tot - cl)
            b_c = b_scr[slot, rows, sl]
            kd_c = kd_scr[slot, rows, sl]
            a_s = stack(aa_scr[slot, rows, sl] * jnp.exp(cl - lw)).astype(BF16)
            r_s = stack(r_scr[slot, rows, sl] * jnp.exp(cl)).astype(BF16)
            bk_s = jnp.concatenate([twice((b_c * e_out).astype(BF16)),
                                    twice((kd_c * e_out).astype(BF16))], axis=0)
            vs_scr[i] = stack(v_scr[slot, rows, sl]).astype(BF16)
            upd_scr[i] = jnp.concatenate([jnp.transpose(stack(b_c * e_end)),
                                          jnp.transpose(stack(kd_c * e_end))], axis=1).astype(BF16)
            wc_scr[i] = jnp.transpose(jnp.broadcast_to(jnp.exp(tot), (LANES, LANES)))
            ab = _mm_nt(jnp.concatenate([a_s, r_s], axis=0), bk_s)
            a_ab = jnp.where(strict, ab[0:LANES, 0:LANES], 0.0)
            a_ak = jnp.where(strict, ab[0:LANES, LANES:], 0.0)
            a_rb = jnp.where(incl, ab[LANES:, 0:LANES], 0.0)
            a_rk = jnp.where(incl, ab[LANES:, LANES:], 0.0)
            rhs_scr[i] = jnp.concatenate([a_s, a_ak.astype(BF16)], axis=1)
            out_scr[i] = jnp.concatenate([r_s, a_rb.astype(BF16), a_rk.astype(BF16)], axis=1)
            aab_scr[i] = a_ab
            t_scr[i] = eye + jnp.where(levels[0], a_ab, 0.0)
            yield

    def invert(steps, slot, masks):
        levels = masks[2]
        idx = [slot * n_prob + step * N_PAIR + p for step in steps for p in range(N_PAIR)]
        for lvl in levels[1:]:
            tbs = [t_scr[i].astype(BF16) for i in idx]
            ms = [dot(jnp.where(lvl, aab_scr[i], 0.0).astype(BF16), tb) for i, tb in zip(idx, tbs)]
            lw_scr[slot, TM:, 0:LANES] = ms[-1][0:SUBLANES, :]
            yield
            for i, tb, m in zip(idx, tbs, ms):
                t_scr[i] = t_scr[i] + dot(tb, m.astype(BF16))
            lw_scr[slot, TM:, 0:LANES] = ms[0][0:SUBLANES, :]
            yield
        for i in idx:
            ti_scr[i] = t_scr[i].astype(BF16)

    def scan(step, slot, d):
        rows = chunk_rows(step, d)
        base = slot * n_prob + step * N_PAIR
        pairs = range(N_PAIR)
        sts = [s_scr[p] for p in pairs]
        stb = [st.astype(BF16) for st in sts]
        vss = [vs_scr[base + p] for p in pairs]
        rhs = [dot(rhs_scr[base + p], jnp.concatenate([stb[p], vss[p]], axis=0)) for p in pairs]
        yield
        pmb = []
        for p in pairs:
            rh = rhs[p].astype(BF16)
            rl = (rhs[p] - rh.astype(F32)).astype(BF16)
            pm = dot(ti_scr[base + p], jnp.concatenate([rh, rl], axis=1))
            pmb.append((pm[:, 0:LANES] + pm[:, LANES:]).astype(BF16))
        yield
        for p in pairs:
            s_scr[p] = wc_scr[base + p] * sts[p] + dot(
                upd_scr[base + p], jnp.concatenate([pmb[p], vss[p]], axis=0))
        yield
        for p in pairs:
            y_s = dot(out_scr[base + p], jnp.concatenate([stb[p], pmb[p], vss[p]], axis=0))
            y_ref[0, rows, p * LANES:(p + 1) * LANES] = y_s[0:CHUNK] + y_s[CHUNK:]
        yield

    first, second = range(0, N_CHUNK // 2), range(N_CHUNK // 2, N_CHUNK)
    chain = itertools.chain.from_iterable

    def stage_a(slot):
        return (prepare(slot, d_a, j_a),
                chain(operands(s, slot, d_a, masks_a) for s in first))

    def stage_b(slot, with_inverse=(), with_scan=()):
        @pl.when(j_b == 0)
        def _():
            zero = jnp.zeros((HEAD_DIM, HEAD_DIM), F32)
            for p in range(N_PAIR):
                top = jnp.concatenate([s0_ref[0, 0, 2 * p], zero], axis=1)
                bot = jnp.concatenate([zero, s0_ref[0, 0, 2 * p + 1]], axis=1)
                s_scr[p] = jnp.transpose(jnp.concatenate([top, bot], axis=0))

        bonus_ref[0] = bn_scr[slot]
        _run(invert(first, slot, masks_b), chain(operands(s, slot, d_b, masks_b) for s in second))
        _run(invert(second, slot, masks_b), chain(scan(s, slot, d_b) for s in first), *with_inverse)
        _run(chain(scan(s, slot, d_b) for s in second), *with_scan)

        @pl.when(j_b == nblk - 1)
        def _():
            for p in range(N_PAIR):
                s_vk = jnp.transpose(s_scr[p])
                sout_ref[0, 0, 2 * p] = s_vk[0:HEAD_DIM, 0:HEAD_DIM]
                sout_ref[0, 0, 2 * p + 1] = s_vk[HEAD_DIM:, HEAD_DIM:]

    @pl.when(g == 0)
    def _():
        prep, ops = stage_a(0)
        _run(prep)
        _run(ops)

    for parity in range(2):
        @pl.when((g > 0) & (g < n_tile) & (g % 2 == parity))
        def _():
            prep, ops = stage_a(parity)
            stage_b(1 - parity, with_inverse=(prep,), with_scan=(ops,))

    @pl.when(g == n_tile)
    def _():
        stage_b((n_tile - 1) % 2)


def _wkv(rkv, lora, s0, mup, mun, k_k, k_a, r_k, w0, w2p, a0, a2p, *, nseq, nblk):
    n = rkv.shape[0]
    hb = TM // HALO
    n_tile = 2 * nseq * nblk

    def coords(t):
        d, s, j = _tile_coords(t, nseq, nblk)
        return d, s, s * nblk + j + d * (nblk - 1 - 2 * j)

    ahead = lambda g: coords(jnp.minimum(g, n_tile - 1))
    behind = lambda g: coords(jnp.maximum(g - 1, 0))
    s0_seq = (lambda s: s) if s0.shape[1] > 1 else (lambda s: 0)
    const = lambda g: (0, 0)
    per_dir = lambda g: (ahead(g)[0], 0, 0)
    out_tile = lambda g: (behind(g)[0], behind(g)[2], 0)
    kern = functools.partial(_wkv_kernel, nseq=nseq, nblk=nblk)
    tok = pltpu.VMEM((2, TM, RWKV_WIDTH), F32)
    n_prob = 2 * N_CHUNK * N_PAIR
    return pl.pallas_call(
        kern,
        grid=(n_tile + 1,),
        in_specs=[
            pl.BlockSpec((TM, 3 * RWKV_WIDTH), lambda g: (ahead(g)[2], 0)),
            pl.BlockSpec((HALO, 3 * RWKV_WIDTH), lambda g: (jnp.maximum(ahead(g)[2] * hb - 1, 0), 0)),
            pl.BlockSpec((HALO, 3 * RWKV_WIDTH),
                         lambda g: (jnp.minimum((ahead(g)[2] + 1) * hb, n // HALO - 1), 0)),
            pl.BlockSpec((TM, 4 * LORA_W), lambda g: (ahead(g)[2], 0)),
            pl.BlockSpec((1, 1, N_HEADS, HEAD_DIM, HEAD_DIM),
                         lambda g: (behind(g)[0], s0_seq(behind(g)[1]), 0, 0, 0)),
            pl.BlockSpec((1, 3 * RWKV_WIDTH), const),
            pl.BlockSpec((1, 3 * RWKV_WIDTH), const),
            pl.BlockSpec((1, RWKV_WIDTH), const),
            pl.BlockSpec((1, RWKV_WIDTH), const),
            pl.BlockSpec((1, RWKV_WIDTH), const),
            pl.BlockSpec((1, 1, RWKV_WIDTH), per_dir),
            pl.BlockSpec((1, 2 * LORA_W, RWKV_WIDTH), per_dir),
            pl.BlockSpec((1, 1, RWKV_WIDTH), per_dir),
            pl.BlockSpec((1, 2 * LORA_W, RWKV_WIDTH), per_dir),
        ],
        out_specs=[
            pl.BlockSpec((1, TM, RWKV_WIDTH), out_tile),
            pl.BlockSpec((1, TM, RWKV_WIDTH), out_tile),
            pl.BlockSpec((1, 1, N_HEADS, HEAD_DIM, HEAD_DIM),
                         lambda g: (behind(g)[0], behind(g)[1], 0, 0, 0)),
        ],
        out_shape=[jax.ShapeDtypeStruct((2, n, RWKV_WIDTH), F32),
                   jax.ShapeDtypeStruct((2, n, RWKV_WIDTH), F32),
                   jax.ShapeDtypeStruct((2, nseq, N_HEADS, HEAD_DIM, HEAD_DIM), F32)],
        scratch_shapes=[pltpu.VMEM((N_PAIR, LANES, LANES), F32),
                        tok, tok, tok, tok, tok,
                        pltpu.VMEM((2, TM + SUBLANES, RWKV_WIDTH), F32),
                        tok, tok,
                        pltpu.VMEM((n_prob, LANES, 2 * LANES), BF16),
                        pltpu.VMEM((n_prob, LANES, 3 * LANES), BF16),
                        pltpu.VMEM((n_prob, LANES, 2 * LANES), BF16),
                        pltpu.VMEM((n_prob, LANES, LANES), BF16),
                        pltpu.VMEM((n_prob, LANES, LANES), BF16),
                        pltpu.VMEM((n_prob, LANES, LANES), F32),
                        pltpu.VMEM((n_prob, LANES, LANES), F32),
                        pltpu.VMEM((n_prob, LANES, LANES), F32)],
        compiler_params=pltpu.CompilerParams(dimension_semantics=("arbitrary",),
                                             vmem_limit_bytes=VMEM_LIMIT),
        name="wkv",
    )(rkv, rkv, rkv, lora, s0, mup, mun, k_k, k_a, r_k, w0, w2p, a0, a2p)


def _mid_kernel(y_ref, bn_ref, lg_ref, conv_ref, gates_ref, x_ref, mod_ref, g2w_ref, lnxg_ref,
                lnxb_ref, wor_ref, woc_ref, wo_ref, n2g_ref, x1_ref, xn2_ref):
    mod = mod_ref[0]
    g1 = mod[:, 2 * D_MODEL:3 * D_MODEL]
    sh2 = mod[:, 3 * D_MODEL:4 * D_MODEL]
    sc2 = mod[:, 4 * D_MODEL:5 * D_MODEL]

    bd = _block_diag_ones()
    y = y_ref[0] + y_ref[1]
    yc = y - _head_sum(y, bd) * (1.0 / HEAD_DIM)
    var = _head_sum(yc * yc, bd) * (1.0 / HEAD_DIM)
    yn = yc * lax.rsqrt(var + GN_EPS) * lnxg_ref[...] + lnxb_ref[...]
    g = _mm(jax.nn.sigmoid(lg_ref[...]), g2w_ref[...])
    y_r = _mm((yn + bn_ref[0] + bn_ref[1]) * g, wor_ref[...])

    y_c = jnp.dot(conv_ref[...], woc_ref[...], preferred_element_type=F32)

    gates = gates_ref[...]
    merged = gates[:, 0:D_MODEL] * y_r + gates[:, D_MODEL:] * y_c
    x1 = x_ref[...] + g1 * _mm(merged, wo_ref[...])
    x1_ref[...] = x1
    xn2_ref[...] = (_rms(x1) * n2g_ref[...] * (1.0 + sc2) + sh2).astype(BF16)


def _mid(y2, bn2, lg, conv, gates, x, mod3, mod_row, g2w, lnxg, lnxb, wor, woc, wo, n2g):
    n = x.shape[0]
    const = lambda i: (0, 0)
    tok = lambda i: (i, 0)
    both = lambda i: (0, i, 0)

    def full(a):
        return pl.BlockSpec(a.shape, const)

    return pl.pallas_call(
        _mid_kernel,
        grid=(n // TM,),
        in_specs=[pl.BlockSpec((2, TM, RWKV_WIDTH), both),
                  pl.BlockSpec((2, TM, RWKV_WIDTH), both),
                  pl.BlockSpec((TM, GATE_LORA), tok),
                  pl.BlockSpec((TM, CONV_WIDTH), tok),
                  pl.BlockSpec((TM, 2 * D_MODEL), tok),
                  pl.BlockSpec((TM, D_MODEL), tok),
                  pl.BlockSpec((1, 1, 6 * D_MODEL), lambda i: (mod_row(i), 0, 0)),
                  full(g2w), full(lnxg), full(lnxb), full(wor), full(woc), full(wo), full(n2g)],
        out_specs=[pl.BlockSpec((TM, D_MODEL), tok), pl.BlockSpec((TM, D_MODEL), tok)],
        out_shape=[jax.ShapeDtypeStruct((n, D_MODEL), F32),
                   jax.ShapeDtypeStruct((n, D_MODEL), BF16)],
        compiler_params=pltpu.CompilerParams(dimension_semantics=("arbitrary",),
                                             vmem_limit_bytes=VMEM_LIMIT),
        name="mid",
    )(y2, bn2, lg, conv, gates, x, mod3, g2w, lnxg, lnxb, wor, woc, wo, n2g)


def _mlp_kernel(x1_ref, xn2_ref, mod_ref, w1_ref, w2_ref, fg_ref, o_ref):
    g2 = mod_ref[0][:, 5 * D_MODEL:6 * D_MODEL]
    h = jnp.maximum(jnp.dot(xn2_ref[...], w1_ref[...], preferred_element_type=F32), 0.0)
    x2 = x1_ref[...] + g2 * _mm(h * h, w2_ref[...])
    o_ref[...] = _rms(x2) * fg_ref[...]


def _mlp(x1, xn2, mod3, mod_row, w1, w2, fg):
    n = x1.shape[0]
    const = lambda i: (0, 0)
    tok = lambda i: (i, 0)
    once = pl.Buffered(1)
    return pl.pallas_call(
        _mlp_kernel,
        grid=(n // TM_MLP,),
        in_specs=[pl.BlockSpec((TM_MLP, D_MODEL), tok),
                  pl.BlockSpec((TM_MLP, D_MODEL), tok),
                  pl.BlockSpec((1, 1, 6 * D_MODEL), lambda i: (mod_row(i * (TM_MLP // TM)), 0, 0)),
                  pl.BlockSpec(w1.shape, const, pipeline_mode=once),
                  pl.BlockSpec(w2.shape, const, pipeline_mode=once),
                  pl.BlockSpec((1, D_MODEL), const)],
        out_specs=pl.BlockSpec((TM_MLP, D_MODEL), tok),
        out_shape=jax.ShapeDtypeStruct((n, D_MODEL), F32),
        compiler_params=pltpu.CompilerParams(dimension_semantics=("arbitrary",),
                                             vmem_limit_bytes=VMEM_LIMIT),
        name="mlp",
    )(x1, xn2, mod3, w1, w2, fg)


def _pad_dir(w):
    return (jnp.eye(2, dtype=w.dtype)[:, :, None, None] * w[:, None]).reshape(2, 2 * w.shape[1], w.shape[2])


def kernel(x_prompt, x_sample, state_fwd, state_bwd, c, c_ctx, ada_w, ada_b, norm1_g, norm2_g, w_in,
           mu_prev, mu_next, decay_w0, decay_w1, decay_w2, iclr_a0, iclr_a1, iclr_a2, gate_g1, gate_g2,
           k_k, k_a, r_k, lnx_g, lnx_b, w_out_rwkv, conv_w, conv_b, conv_ln_g, conv_ln_b, w_out_conv,
           w_o, mlp_w1, mlp_w2, final_g):
    n_ctx, seq_ctx, _ = x_prompt.shape
    n_lat, seq_lat, _ = x_sample.shape
    depth = ada_w.shape[0]
    assert seq_ctx == TM and seq_lat % TM == 0 and GRID_W == CHUNK and c.shape[0] + 1 <= 8
    lat_blk = seq_lat // TM
    row_ctx = lambda i: 0
    row_lat = lambda i: 1 + i // lat_blk

    xp = x_prompt.reshape(n_ctx * seq_ctx, D_MODEL)
    xs = x_sample.reshape(n_lat * seq_lat, D_MODEL)
    cond = jnp.concatenate([c_ctx[None, :], c, jnp.zeros((7 - c.shape[0], D_MODEL), F32)], axis=0)
    zero_state = jnp.zeros((2, 1, N_HEADS, HEAD_DIM, HEAD_DIM), F32)
    row = lambda a: a.reshape(1, -1)
    assert depth == 1
    lay = lambda a: a.reshape(a.shape[1:])
    bf = lambda a: lay(a).astype(BF16)

    mod3 = _ada(cond, lay(ada_w), lay(ada_b)).reshape(8, 1, 6 * D_MODEL)
    w_lora = jnp.concatenate([decay_w1[0, 0], decay_w1[0, 1], iclr_a1[0, 0], iclr_a1[0, 1], gate_g1[0]],
                             axis=1).astype(BF16)
    conv_taps = jnp.broadcast_to(conv_w[0, :, None, :], (CONV_K, SUBLANES, CONV_WIDTH))
    front_w = (norm1_g, bf(w_in), w_lora, conv_taps, conv_b, conv_ln_g, conv_ln_b)
    wkv_w = (mu_prev, mu_next, k_k, k_a, row(r_k),
             decay_w0.reshape(2, 1, RWKV_WIDTH), _pad_dir(lay(decay_w2)).astype(BF16),
             iclr_a0.reshape(2, 1, RWKV_WIDTH), _pad_dir(lay(iclr_a2)).astype(BF16))
    mid_w = (bf(gate_g2), lnx_g, lnx_b, bf(w_out_rwkv), bf(w_out_conv), bf(w_o), norm2_g)
    mlp_w = (bf(mlp_w1), bf(mlp_w2), row(final_g))
    s0_lat = jnp.stack([state_fwd[:, 0], state_bwd[:, 0]]).astype(F32)

    def layer(x, mod_row, s0, nseq, nblk, conv_seq):
        rkv, conv, gates, lora, lg = _front(x, mod3, mod_row, *front_w, seq=conv_seq)
        y2, bn2, s_out = _wkv(rkv, lora, s0, *wkv_w, nseq=nseq, nblk=nblk)
        x1, xn2 = _mid(y2, bn2, lg, conv, gates, x, mod3, mod_row, *mid_w)
        return _mlp(x1, xn2, mod3, mod_row, *mlp_w), s_out

    yp, s_ctx = layer(xp, row_ctx, zero_state, n_ctx, 1, seq_ctx)
    ys, _ = layer(xs, row_lat, s0_lat, n_lat, lat_blk, GRID_W)
    states = s_ctx.astype(x_prompt.dtype)[:, :, None]
    return (yp.reshape(x_prompt.shape), ys.reshape(x_sample.shape), states[0], states[1])
```

```python
import functools
import itertools

import jax
import jax.numpy as jnp
from jax import lax
from jax.experimental import pallas as pl
from jax.experimental.pallas import tpu as pltpu

D_MODEL = 1024
RWKV_WIDTH = 512
HEAD_DIM = 64
N_HEADS = RWKV_WIDTH // HEAD_DIM
CONV_WIDTH = 512
CONV_K = 31
D_FF = 4 * D_MODEL
GRID_W = 64
LORA_W = 64
GATE_LORA = 128
RMS_EPS = 1e-6
LN_EPS = 1e-5
GN_EPS = 64e-5

LANES = 128
SUBLANES = 8
TM = 256
TM_MLP = 512
CHUNK = 64
N_CHUNK = TM // CHUNK
N_PAIR = RWKV_WIDTH // LANES
HALO = SUBLANES
CONV_PAD = 16
CONV_ROWS = 32
VMEM_LIMIT = 56 * 1024 * 1024

F32 = jnp.float32
BF16 = jnp.bfloat16
DECAY_SCALE = 0.6065306597126334


def _mm(a, b):
    return jnp.dot(a.astype(BF16), b.astype(BF16), preferred_element_type=F32)


def _mm_nt(a, b):
    return lax.dot_general(a.astype(BF16), b.astype(BF16), (((1,), (1,)), ((), ())),
                           preferred_element_type=F32)


def _split3(x):
    x1 = x.astype(BF16)
    r1 = x - x1.astype(F32)
    x2 = r1.astype(BF16)
    x3 = (r1 - x2.astype(F32)).astype(BF16)
    return x3, x2, x1


def _mm_split3(m, x):
    x3, x2, x1 = (jnp.dot(m, t, preferred_element_type=F32) for t in _split3(x))
    return x3 + x2 + x1


def _head_sum(x, bd):
    rows = x.shape[0]
    n_grp = x.shape[1] // LANES
    terms = []
    for p in range(n_grp):
        xp = x[:, p * LANES:(p + 1) * LANES]
        head = xp.astype(BF16)
        terms += [head, (xp - head.astype(F32)).astype(BF16)]
    prod = jnp.dot(jnp.concatenate(terms, axis=0), bd, preferred_element_type=F32)
    sums = [prod[(2 * p) * rows:(2 * p + 1) * rows] + prod[(2 * p + 1) * rows:(2 * p + 2) * rows]
            for p in range(n_grp)]
    return jnp.concatenate(sums, axis=1)


def _block_diag_ones():
    ri = lax.broadcasted_iota(jnp.int32, (LANES, LANES), 0)
    ci = lax.broadcasted_iota(jnp.int32, (LANES, LANES), 1)
    shift = HEAD_DIM.bit_length() - 1
    return jnp.where((ri >> shift) == (ci >> shift), 1.0, 0.0).astype(BF16)


def _rms(x):
    return x * lax.rsqrt(jnp.mean(x * x, axis=-1, keepdims=True) + RMS_EPS)


def _ada_kernel(c_ref, w_ref, b_ref, o_ref):
    c = c_ref[...]
    w = w_ref[...].astype(BF16)
    s3, s2, s1 = (jnp.dot(t, w, preferred_element_type=F32) for t in _split3(c * jax.nn.sigmoid(c)))
    o_ref[...] = s3 + s2 + s1 + b_ref[...]


def _ada(cond, w, b):
    n = w.shape[1]
    tn = 2048
    return pl.pallas_call(
        _ada_kernel,
        grid=(n // tn,),
        in_specs=[pl.BlockSpec((8, D_MODEL), lambda i: (0, 0)),
                  pl.BlockSpec((D_MODEL, tn), lambda i: (0, i)),
                  pl.BlockSpec((1, tn), lambda i: (0, i))],
        out_specs=pl.BlockSpec((8, tn), lambda i: (0, i)),
        out_shape=jax.ShapeDtypeStruct((8, n), F32),
        compiler_params=pltpu.CompilerParams(dimension_semantics=("arbitrary",),
                                             vmem_limit_bytes=VMEM_LIMIT),
        name="ada",
    )(cond, w, b.reshape(1, n))


_IN_SPLIT = (3 * RWKV_WIDTH, 2 * CONV_WIDTH, 2 * D_MODEL)
_LORA_SPLIT = (4 * LORA_W, GATE_LORA)


def _run(*streams):
    streams = list(streams)
    while streams:
        streams = [g for g in streams if next(g, True) is None]


def _conv_module(uu, cw_ref, cb_ref, clg_ref, clb_ref, pad_scr, seq):
    nseq = TM // seq
    zpad = jnp.zeros((nseq, CONV_PAD, CONV_WIDTH), F32)
    pad_scr[0, :, 0:CONV_PAD, :] = zpad
    pad_scr[0, :, CONV_PAD + seq:, :] = zpad
    for s in range(nseq):
        pad_scr[0, s, CONV_PAD:CONV_PAD + seq, :] = uu[s * seq:(s + 1) * seq]
    live = seq + 2 * CONV_PAD - SUBLANES
    for m in range(1, SUBLANES):
        pad_scr[m, :, 0:live, :] = pad_scr[0, :, m:m + live, :]
    first = CONV_PAD - CONV_K // 2
    groups = (CONV_ROWS // SUBLANES, SUBLANES, CONV_WIDTH)
    parts = []
    for s in range(nseq):
        for r0 in range(0, seq, CONV_ROWS):
            acc = jnp.broadcast_to(cb_ref[...], groups)
            for t in range(CONV_K):
                m = (first + t) % SUBLANES
                lo = r0 + first + t - m
                acc = acc + pad_scr[m, s, lo:lo + CONV_ROWS, :].reshape(groups) * cw_ref[t][None]
            parts.append(acc.reshape(CONV_ROWS, CONV_WIDTH))
    cv = jnp.concatenate(parts, axis=0)
    cm = cv - jnp.mean(cv, axis=-1, keepdims=True)
    cvar = jnp.mean(cm * cm, axis=-1, keepdims=True)
    un = cm * lax.rsqrt(cvar + LN_EPS) * clg_ref[...] + clb_ref[...]
    return un * jax.nn.sigmoid(un)


def _front_kernel(x_ref, mod_ref, g_ref, win_ref, wlo_ref, rkv_ref, glu_ref, gates_ref, lora_ref, lg_ref):
    mod = mod_ref[0]
    sh1 = mod[:, 0:D_MODEL]
    sc1 = mod[:, D_MODEL:2 * D_MODEL]
    xn = (_rms(x_ref[...]) * g_ref[...] * (1.0 + sc1) + sh1).astype(BF16)
    c0, c1, c2 = itertools.accumulate(_IN_SPLIT)
    dot = functools.partial(jnp.dot, preferred_element_type=F32)
    rkv_ref[...] = dot(xn, win_ref[:, 0:c0])
    glu = dot(xn, win_ref[:, c0:c1])
    glu_ref[...] = glu[:, 0:CONV_WIDTH] * jax.nn.sigmoid(glu[:, CONV_WIDTH:])
    gates_ref[...] = jax.nn.sigmoid(dot(xn, win_ref[:, c1:c2]))
    lora_ref[...] = dot(xn, wlo_ref[:, 0:_LORA_SPLIT[0]])
    lg_ref[...] = dot(xn, wlo_ref[:, _LORA_SPLIT[0]:])


def _front(x, mod3, mod_row, g, w_in, w_lora):
    n = x.shape[0]
    const = lambda i: (0, 0)
    tok = lambda i: (i, 0)
    widths = (_IN_SPLIT[0], CONV_WIDTH, _IN_SPLIT[2]) + _LORA_SPLIT
    return pl.pallas_call(
        _front_kernel,
        grid=(n // TM,),
        in_specs=[pl.BlockSpec((TM, D_MODEL), tok),
                  pl.BlockSpec((1, 1, 6 * D_MODEL), lambda i: (mod_row(i), 0, 0)),
                  pl.BlockSpec((1, D_MODEL), const),
                  pl.BlockSpec(w_in.shape, const),
                  pl.BlockSpec(w_lora.shape, const)],
        out_specs=[pl.BlockSpec((TM, w), tok) for w in widths],
        out_shape=[jax.ShapeDtypeStruct((n, w), F32) for w in widths],
        compiler_params=pltpu.CompilerParams(dimension_semantics=("arbitrary",),
                                             vmem_limit_bytes=VMEM_LIMIT),
        name="front",
    )(x, mod3, g, w_in, w_lora)


def _tile_coords(t, nseq, nblk):
    per_dir = nseq * nblk
    d = t // per_dir
    rem = t - d * per_dir
    s = rem // nblk
    return d, s, rem - s * nblk


def _wkv_masks(d):
    shift = CHUNK.bit_length() - 1
    ri = lax.broadcasted_iota(jnp.int32, (LANES, LANES), 0)
    ci = lax.broadcasted_iota(jnp.int32, (LANES, LANES), 1)
    same = (ri >> shift) == (ci >> shift)
    rel = (ci - ri) * (1 - 2 * d)
    levels = [((ri >> (lv + 1)) == (ci >> (lv + 1))) & (((ri >> lv) & 1) == 1 - d) & (((ci >> lv) & 1) == d)
              for lv in range(shift)]
    return same & (rel < 0), same & (rel <= 0), levels


def _wkv_kernel(rkv_ref, prev_ref, next_ref, lora_ref, s0_ref, mup_ref, mun_ref, kk_ref, ka_ref,
                rk_ref, w0_ref, w2_ref, a0_ref, a2_ref,
                y_ref, bonus_ref, sout_ref,
                s_scr, r_scr, kd_scr, v_scr, aa_scr, b_scr, lw_scr, cl_scr, bn_scr,
                rhs_scr, out_scr, upd_scr, ti_scr, vs_scr, wc_scr, aab_scr, t_scr, *, nseq, nblk):
    g = pl.program_id(0)
    n_tile = 2 * nseq * nblk
    n_prob = N_CHUNK * N_PAIR
    d_a, _, j_a = _tile_coords(jnp.minimum(g, n_tile - 1), nseq, nblk)
    d_b, _, j_b = _tile_coords(jnp.maximum(g - 1, 0), nseq, nblk)
    masks_a = _wkv_masks(d_a)
    masks_b = _wkv_masks(d_b)
    shift = CHUNK.bit_length() - 1
    ri = lax.broadcasted_iota(jnp.int32, (LANES, LANES), 0)
    ci = lax.broadcasted_iota(jnp.int32, (LANES, LANES), 1)
    eye = (ri == ci).astype(F32)
    head0 = lax.broadcasted_iota(jnp.int32, (CHUNK, LANES), 1) < HEAD_DIM
    dot = functools.partial(jnp.dot, preferred_element_type=F32)

    def stack(t):
        return jnp.concatenate([jnp.where(head0, t, 0.0), jnp.where(head0, 0.0, t)], axis=0)

    def twice(t):
        return jnp.concatenate([t, t], axis=0)

    def chunk_rows(step, d):
        cidx = step + d * (N_CHUNK - 1 - 2 * step)
        return pl.ds(pl.multiple_of(cidx * CHUNK, CHUNK), CHUNK)

    def prepare(slot, d, j):
        blk = j + d * (nblk - 1 - 2 * j)
        u = rkv_ref[...]
        row = lax.broadcasted_iota(jnp.int32, (TM, 1), 0)
        prow = jnp.where(blk > 0, prev_ref[HALO - 1:HALO, :], 0.0)
        nrow = jnp.where(blk < nblk - 1, next_ref[0:1, :], 0.0)
        prev = jnp.where(row == 0, prow, pltpu.roll(u, 1, axis=0))
        nxt = jnp.where(row == TM - 1, nrow, pltpu.roll(u, TM - 1, axis=0))
        x = u + (prev - u) * mup_ref[...] + (nxt - u) * mun_ref[...]
        r = x[:, 0:RWKV_WIDTH]
        k = x[:, RWKV_WIDTH:2 * RWKV_WIDTH]
        v = x[:, 2 * RWKV_WIDTH:]
        r_scr[slot] = r
        v_scr[slot] = v
        yield
        bd = _block_diag_ones()
        kkr = k * kk_ref[...]
        kk = kkr * lax.rsqrt(jnp.maximum(_head_sum(kkr * kkr, bd), 1e-24))
        aa_scr[slot] = -kk
        yield
        lo = lora_ref[...]
        z = w0_ref[0] + _mm(jnp.tanh(lo[:, 0:2 * LORA_W]), w2_ref[0])
        lw = -DECAY_SCALE * jax.nn.sigmoid(z)
        lw_scr[slot, 0:TM, :] = lw
        a = jax.nn.sigmoid(a0_ref[0] + _mm(lo[:, 2 * LORA_W:4 * LORA_W], a2_ref[0]))
        kd = k * (1.0 + (a - 1.0) * ka_ref[...])
        kd_scr[slot] = kd
        b_scr[slot] = kk * a
        yield
        bn_scr[slot] = _head_sum(r * kd * rk_ref[...], bd) * v
        yield
        rt = lax.broadcasted_iota(jnp.int32, (TM, TM), 0)
        ct = lax.broadcasted_iota(jnp.int32, (TM, TM), 1)
        cum_m = jnp.where(((rt >> shift) == (ct >> shift)) & (((ct - rt) * (1 - 2 * d)) <= 0), 1.0, 0.0)
        cl_scr[slot] = _mm_split3(cum_m.astype(BF16), lw)
        yield

    def operands(step, slot, d, masks):
        strict, incl, levels = masks
        rows = chunk_rows(step, d)
        for p in range(N_PAIR):
            i = slot * n_prob + step * N_PAIR + p
            sl = slice(p * LANES, (p + 1) * LANES)
            lw = lw_scr[slot, rows, sl]
            cl = cl_scr[slot, rows, sl]
            tot = jnp.sum(lw, axis=0, keepdims=True)
            e_out = jnp.exp(-cl)
            e_end = jnp.exp(tot - cl)
            b_c = b_scr[slot, rows, sl]
            kd_c = kd_scr[slot, rows, sl]
            a_s = stack(aa_scr[slot, rows, sl] * jnp.exp(cl - lw)).astype(BF16)
            r_s = stack(r_scr[slot, rows, sl] * jnp.exp(cl)).astype(BF16)
            bk_s = jnp.concatenate([twice((b_c * e_out).astype(BF16)),
                                    twice((kd_c * e_out).astype(BF16))], axis=0)
            vs_scr[i] = stack(v_scr[slot, rows, sl]).astype(BF16)
            upd_scr[i] = jnp.concatenate([jnp.transpose(stack(b_c * e_end)),
                                          jnp.transpose(stack(kd_c * e_end))], axis=1).astype(BF16)
            wc_scr[i] = jnp.transpose(jnp.broadcast_to(jnp.exp(tot), (LANES, LANES)))
            ab = _mm_nt(jnp.concatenate([a_s, r_s], axis=0), bk_s)
            a_ab = jnp.where(strict, ab[0:LANES, 0:LANES], 0.0)
            a_ak = jnp.where(strict, ab[0:LANES, LANES:], 0.0)
            a_rb = jnp.where(incl, ab[LANES:, 0:LANES], 0.0)
            a_rk = jnp.where(incl, ab[LANES:, LANES:], 0.0)
            rhs_scr[i] = jnp.concatenate([a_s, a_ak.astype(BF16)], axis=1)
            out_scr[i] = jnp.concatenate([r_s, a_rb.astype(BF16), a_rk.astype(BF16)], axis=1)
            aab_scr[i] = a_ab
            t_scr[i] = eye + jnp.where(levels[0], a_ab, 0.0)
            yield

    def invert(steps, slot, masks):
        levels = masks[2]
        idx = [slot * n_prob + step * N_PAIR + p for step in steps for p in range(N_PAIR)]
        for lvl in levels[1:]:
            tbs = [t_scr[i].astype(BF16) for i in idx]
            ms = [dot(jnp.where(lvl, aab_scr[i], 0.0).astype(BF16), tb) for i, tb in zip(idx, tbs)]
            lw_scr[slot, TM:, 0:LANES] = ms[-1][0:SUBLANES, :]
            yield
            for i, tb, m in zip(idx, tbs, ms):
                t_scr[i] = t_scr[i] + dot(tb, m.astype(BF16))
            lw_scr[slot, TM:, 0:LANES] = ms[0][0:SUBLANES, :]
            yield
        for i in idx:
            ti_scr[i] = t_scr[i].astype(BF16)

    def scan(step, slot, d):
        rows = chunk_rows(step, d)
        base = slot * n_prob + step * N_PAIR
        pairs = range(N_PAIR)
        sts = [s_scr[p] for p in pairs]
        stb = [st.astype(BF16) for st in sts]
        vss = [vs_scr[base + p] for p in pairs]
        rhs = [dot(rhs_scr[base + p], jnp.concatenate([stb[p], vss[p]], axis=0)) for p in pairs]
        yield
        pmb = []
        for p in pairs:
            rh = rhs[p].astype(BF16)
            rl = (rhs[p] - rh.astype(F32)).astype(BF16)
            pm = dot(ti_scr[base + p], jnp.concatenate([rh, rl], axis=1))
            pmb.append((pm[:, 0:LANES] + pm[:, LANES:]).astype(BF16))
        yield
        for p in pairs:
            s_scr[p] = wc_scr[base + p] * sts[p] + dot(
                upd_scr[base + p], jnp.concatenate([pmb[p], vss[p]], axis=0))
        yield
        for p in pairs:
            y_s = dot(out_scr[base + p], jnp.concatenate([stb[p], pmb[p], vss[p]], axis=0))
            y_ref[0, rows, p * LANES:(p + 1) * LANES] = y_s[0:CHUNK] + y_s[CHUNK:]
        yield

    first, second = range(0, N_CHUNK // 2), range(N_CHUNK // 2, N_CHUNK)
    chain = itertools.chain.from_iterable

    def stage_a(slot):
        return (prepare(slot, d_a, j_a),
                chain(operands(s, slot, d_a, masks_a) for s in first))

    def stage_b(slot, with_inverse=(), with_scan=()):
        @pl.when(j_b == 0)
        def _():
            zero = jnp.zeros((HEAD_DIM, HEAD_DIM), F32)
            for p in range(N_PAIR):
                top = jnp.concatenate([s0_ref[0, 0, 2 * p], zero], axis=1)
                bot = jnp.concatenate([zero, s0_ref[0, 0, 2 * p + 1]], axis=1)
                s_scr[p] = jnp.transpose(jnp.concatenate([top, bot], axis=0))

        bonus_ref[0] = bn_scr[slot]
        _run(invert(first, slot, masks_b), chain(operands(s, slot, d_b, masks_b) for s in second))
        _run(invert(second, slot, masks_b), chain(scan(s, slot, d_b) for s in first), *with_inverse)
        _run(chain(scan(s, slot, d_b) for s in second), *with_scan)

        @pl.when(j_b == nblk - 1)
        def _():
            for p in range(N_PAIR):
                s_vk = jnp.transpose(s_scr[p])
                sout_ref[0, 0, 2 * p] = s_vk[0:HEAD_DIM, 0:HEAD_DIM]
                sout_ref[0, 0, 2 * p + 1] = s_vk[HEAD_DIM:, HEAD_DIM:]

    @pl.when(g == 0)
    def _():
        prep, ops = stage_a(0)
        _run(prep)
        _run(ops)

    for parity in range(2):
        @pl.when((g > 0) & (g < n_tile) & (g % 2 == parity))
        def _():
            prep, ops = stage_a(parity)
            stage_b(1 - parity, with_inverse=(prep,), with_scan=(ops,))

    @pl.when(g == n_tile)
    def _():
        stage_b((n_tile - 1) % 2)


def _wkv(rkv, lora, s0, mup, mun, k_k, k_a, r_k, w0, w2p, a0, a2p, *, nseq, nblk):
    n = rkv.shape[0]
    hb = TM // HALO
    n_tile = 2 * nseq * nblk

    def coords(t):
        d, s, j = _tile_coords(t, nseq, nblk)
        return d, s, s * nblk + j + d * (nblk - 1 - 2 * j)

    ahead = lambda g: coords(jnp.minimum(g, n_tile - 1))
    behind = lambda g: coords(jnp.maximum(g - 1, 0))
    s0_seq = (lambda s: s) if s0.shape[1] > 1 else (lambda s: 0)
    const = lambda g: (0, 0)
    per_dir = lambda g: (ahead(g)[0], 0, 0)
    out_tile = lambda g: (behind(g)[0], behind(g)[2], 0)
    kern = functools.partial(_wkv_kernel, nseq=nseq, nblk=nblk)
    tok = pltpu.VMEM((2, TM, RWKV_WIDTH), F32)
    n_prob = 2 * N_CHUNK * N_PAIR
    return pl.pallas_call(
        kern,
        grid=(n_tile + 1,),
        in_specs=[
            pl.BlockSpec((TM, 3 * RWKV_WIDTH), lambda g: (ahead(g)[2], 0)),
            pl.BlockSpec((HALO, 3 * RWKV_WIDTH), lambda g: (jnp.maximum(ahead(g)[2] * hb - 1, 0), 0)),
            pl.BlockSpec((HALO, 3 * RWKV_WIDTH),
                         lambda g: (jnp.minimum((ahead(g)[2] + 1) * hb, n // HALO - 1), 0)),
            pl.BlockSpec((TM, 4 * LORA_W), lambda g: (ahead(g)[2], 0)),
            pl.BlockSpec((1, 1, N_HEADS, HEAD_DIM, HEAD_DIM),
                         lambda g: (behind(g)[0], s0_seq(behind(g)[1]), 0, 0, 0)),
            pl.BlockSpec((1, 3 * RWKV_WIDTH), const),
            pl.BlockSpec((1, 3 * RWKV_WIDTH), const),
            pl.BlockSpec((1, RWKV_WIDTH), const),
            pl.BlockSpec((1, RWKV_WIDTH), const),
            pl.BlockSpec((1, RWKV_WIDTH), const),
            pl.BlockSpec((1, 1, RWKV_WIDTH), per_dir),
            pl.BlockSpec((1, 2 * LORA_W, RWKV_WIDTH), per_dir),
            pl.BlockSpec((1, 1, RWKV_WIDTH), per_dir),
            pl.BlockSpec((1, 2 * LORA_W, RWKV_WIDTH), per_dir),
        ],
        out_specs=[
            pl.BlockSpec((1, TM, RWKV_WIDTH), out_tile),
            pl.BlockSpec((1, TM, RWKV_WIDTH), out_tile),
            pl.BlockSpec((1, 1, N_HEADS, HEAD_DIM, HEAD_DIM),
                         lambda g: (behind(g)[0], behind(g)[1], 0, 0, 0)),
        ],
        out_shape=[jax.ShapeDtypeStruct((2, n, RWKV_WIDTH), F32),
                   jax.ShapeDtypeStruct((2, n, RWKV_WIDTH), F32),
                   jax.ShapeDtypeStruct((2, nseq, N_HEADS, HEAD_DIM, HEAD_DIM), F32)],
        scratch_shapes=[pltpu.VMEM((N_PAIR, LANES, LANES), F32),
                        tok, tok, tok, tok, tok,
                        pltpu.VMEM((2, TM + SUBLANES, RWKV_WIDTH), F32),
                        tok, tok,
                        pltpu.VMEM((n_prob, LANES, 2 * LANES), BF16),
                        pltpu.VMEM((n_prob, LANES, 3 * LANES), BF16),
                        pltpu.VMEM((n_prob, LANES, 2 * LANES), BF16),
                        pltpu.VMEM((n_prob, LANES, LANES), BF16),
                        pltpu.VMEM((n_prob, LANES, LANES), BF16),
                        pltpu.VMEM((n_prob, LANES, LANES), F32),
                        pltpu.VMEM((n_prob, LANES, LANES), F32),
                        pltpu.VMEM((n_prob, LANES, LANES), F32)],
        compiler_params=pltpu.CompilerParams(dimension_semantics=("arbitrary",),
                                             vmem_limit_bytes=VMEM_LIMIT),
        name="wkv",
    )(rkv, rkv, rkv, lora, s0, mup, mun, k_k, k_a, r_k, w0, w2p, a0, a2p)


def _mid_kernel(y_ref, bn_ref, lg_ref, glu_ref, gates_ref, x_ref, mod_ref, g2w_ref, lnxg_ref,
                lnxb_ref, wor_ref, cw_ref, cb_ref, clg_ref, clb_ref, woc_ref, wo_ref, n2g_ref,
                x1_ref, xn2_ref, pad_scr, *, seq):
    mod = mod_ref[0]
    g1 = mod[:, 2 * D_MODEL:3 * D_MODEL]
    sh2 = mod[:, 3 * D_MODEL:4 * D_MODEL]
    sc2 = mod[:, 4 * D_MODEL:5 * D_MODEL]

    bd = _block_diag_ones()
    y = y_ref[0] + y_ref[1]
    yc = y - _head_sum(y, bd) * (1.0 / HEAD_DIM)
    var = _head_sum(yc * yc, bd) * (1.0 / HEAD_DIM)
    yn = yc * lax.rsqrt(var + GN_EPS) * lnxg_ref[...] + lnxb_ref[...]
    g = _mm(jax.nn.sigmoid(lg_ref[...]), g2w_ref[...])
    y_r = _mm((yn + bn_ref[0] + bn_ref[1]) * g, wor_ref[...])

    y_c = _mm(_conv_module(glu_ref[...], cw_ref, cb_ref, clg_ref, clb_ref, pad_scr, seq), woc_ref[...])

    gates = gates_ref[...]
    merged = gates[:, 0:D_MODEL] * y_r + gates[:, D_MODEL:] * y_c
    x1 = x_ref[...] + g1 * _mm(merged, wo_ref[...])
    x1_ref[...] = x1
    xn2_ref[...] = (_rms(x1) * n2g_ref[...] * (1.0 + sc2) + sh2).astype(BF16)


def _mid(y2, bn2, lg, glu, gates, x, mod3, mod_row, g2w, lnxg, lnxb, wor, cw, cb, clg, clb, woc,
         wo, n2g, *, seq):
    n = x.shape[0]
    const = lambda i: (0, 0)
    tok = lambda i: (i, 0)
    both = lambda i: (0, i, 0)

    def full(a):
        return pl.BlockSpec(a.shape, lambda i: (0,) * a.ndim)

    return pl.pallas_call(
        functools.partial(_mid_kernel, seq=seq),
        grid=(n // TM,),
        in_specs=[pl.BlockSpec((2, TM, RWKV_WIDTH), both),
                  pl.BlockSpec((2, TM, RWKV_WIDTH), both),
                  pl.BlockSpec((TM, GATE_LORA), tok),
                  pl.BlockSpec((TM, CONV_WIDTH), tok),
                  pl.BlockSpec((TM, 2 * D_MODEL), tok),
                  pl.BlockSpec((TM, D_MODEL), tok),
                  pl.BlockSpec((1, 1, 6 * D_MODEL), lambda i: (mod_row(i), 0, 0)),
                  full(g2w), full(lnxg), full(lnxb), full(wor), full(cw), full(cb), full(clg),
                  full(clb), full(woc), full(wo), full(n2g)],
        out_specs=[pl.BlockSpec((TM, D_MODEL), tok), pl.BlockSpec((TM, D_MODEL), tok)],
        out_shape=[jax.ShapeDtypeStruct((n, D_MODEL), F32),
                   jax.ShapeDtypeStruct((n, D_MODEL), BF16)],
        scratch_shapes=[pltpu.VMEM((SUBLANES, TM // seq, seq + 2 * CONV_PAD, CONV_WIDTH), F32)],
        compiler_params=pltpu.CompilerParams(dimension_semantics=("arbitrary",),
                                             vmem_limit_bytes=VMEM_LIMIT),
        name="mid",
    )(y2, bn2, lg, glu, gates, x, mod3, g2w, lnxg, lnxb, wor, cw, cb, clg, clb, woc, wo, n2g)


def _mlp_kernel(x1_ref, xn2_ref, mod_ref, w1_ref, w2_ref, fg_ref, o_ref):
    g2 = mod_ref[0][:, 5 * D_MODEL:6 * D_MODEL]
    h = jnp.maximum(jnp.dot(xn2_ref[...], w1_ref[...], preferred_element_type=F32), 0.0)
    x2 = x1_ref[...] + g2 * _mm(h * h, w2_ref[...])
    o_ref[...] = _rms(x2) * fg_ref[...]


def _mlp(x1, xn2, mod3, mod_row, w1, w2, fg):
    n = x1.shape[0]
    const = lambda i: (0, 0)
    tok = lambda i: (i, 0)
    once = pl.Buffered(1)
    return pl.pallas_call(
        _mlp_kernel,
        grid=(n // TM_MLP,),
        in_specs=[pl.BlockSpec((TM_MLP, D_MODEL), tok),
                  pl.BlockSpec((TM_MLP, D_MODEL), tok),
                  pl.BlockSpec((1, 1, 6 * D_MODEL), lambda i: (mod_row(i * (TM_MLP // TM)), 0, 0)),
                  pl.BlockSpec(w1.shape, const, pipeline_mode=once),
                  pl.BlockSpec(w2.shape, const, pipeline_mode=once),
                  pl.BlockSpec((1, D_MODEL), const)],
        out_specs=pl.BlockSpec((TM_MLP, D_MODEL), tok),
        out_shape=jax.ShapeDtypeStruct((n, D_MODEL), F32),
        compiler_params=pltpu.CompilerParams(dimension_semantics=("arbitrary",),
                                             vmem_limit_bytes=VMEM_LIMIT),
        name="mlp",
    )(x1, xn2, mod3, w1, w2, fg)


def _pad_dir(w):
    return (jnp.eye(2, dtype=w.dtype)[:, :, None, None] * w[:, None]).reshape(2, 2 * w.shape[1], w.shape[2])


def kernel(x_prompt, x_sample, state_fwd, state_bwd, c, c_ctx, ada_w, ada_b, norm1_g, norm2_g, w_in,
           mu_prev, mu_next, decay_w0, decay_w1, decay_w2, iclr_a0, iclr_a1, iclr_a2, gate_g1, gate_g2,
           k_k, k_a, r_k, lnx_g, lnx_b, w_out_rwkv, conv_w, conv_b, conv_ln_g, conv_ln_b, w_out_conv,
           w_o, mlp_w1, mlp_w2, final_g):
    n_ctx, seq_ctx, _ = x_prompt.shape
    n_lat, seq_lat, _ = x_sample.shape
    depth = ada_w.shape[0]
    assert seq_ctx == TM and seq_lat % TM == 0 and GRID_W == CHUNK and c.shape[0] + 1 <= 8
    lat_blk = seq_lat // TM
    row_ctx = lambda i: 0
    row_lat = lambda i: 1 + i // lat_blk

    xp = x_prompt.reshape(n_ctx * seq_ctx, D_MODEL)
    xs = x_sample.reshape(n_lat * seq_lat, D_MODEL)
    cond = jnp.concatenate([c_ctx[None, :], c, jnp.zeros((7 - c.shape[0], D_MODEL), F32)], axis=0)
    zero_state = jnp.zeros((2, 1, N_HEADS, HEAD_DIM, HEAD_DIM), F32)
    row = lambda a: a.reshape(1, -1)
    assert depth == 1
    lay = lambda a: a.reshape(a.shape[1:])
    bf = lambda a: lay(a).astype(BF16)

    mod3 = _ada(cond, lay(ada_w), lay(ada_b)).reshape(8, 1, 6 * D_MODEL)
    w_lora = jnp.concatenate([decay_w1[0, 0], decay_w1[0, 1], iclr_a1[0, 0], iclr_a1[0, 1], gate_g1[0]],
                             axis=1).astype(BF16)
    conv_taps = jnp.broadcast_to(conv_w[0, :, None, :], (CONV_K, SUBLANES, CONV_WIDTH))
    front_w = (norm1_g, bf(w_in), w_lora)
    wkv_w = (mu_prev, mu_next, k_k, k_a, row(r_k),
             decay_w0.reshape(2, 1, RWKV_WIDTH), _pad_dir(lay(decay_w2)).astype(BF16),
             iclr_a0.reshape(2, 1, RWKV_WIDTH), _pad_dir(lay(iclr_a2)).astype(BF16))
    mid_w = (bf(gate_g2), lnx_g, lnx_b, bf(w_out_rwkv), conv_taps, conv_b, conv_ln_g, conv_ln_b,
             bf(w_out_conv), bf(w_o), norm2_g)
    mlp_w = (bf(mlp_w1), bf(mlp_w2), row(final_g))
    s0_lat = jnp.stack([state_fwd[:, 0], state_bwd[:, 0]]).astype(F32)

    def layer(x, mod_row, s0, nseq, nblk, conv_seq):
        rkv, glu, gates, lora, lg = _front(x, mod3, mod_row, *front_w)
        y2, bn2, s_out = _wkv(rkv, lora, s0, *wkv_w, nseq=nseq, nblk=nblk)
        x1, xn2 = _mid(y2, bn2, lg, glu, gates, x, mod3, mod_row, *mid_w, seq=conv_seq)
        return _mlp(x1, xn2, mod3, mod_row, *mlp_w), s_out

    yp, s_ctx = layer(xp, row_ctx, zero_state, n_ctx, 1, seq_ctx)
    ys, _ = layer(xs, row_lat, s0_lat, n_lat, lat_blk, GRID_W)
    states = s_ctx.astype(x_prompt.dtype)[:, :, None]
    return (yp.reshape(x_prompt.shape), ys.reshape(x_sample.shape), states[0], states[1])
```

```python
import functools
import itertools

import jax
import jax.numpy as jnp
from jax import lax
from jax.experimental import pallas as pl
from jax.experimental.pallas import tpu as pltpu

D_MODEL = 1024
RWKV_WIDTH = 512
HEAD_DIM = 64
N_HEADS = RWKV_WIDTH // HEAD_DIM
CONV_WIDTH = 512
CONV_K = 31
D_FF = 4 * D_MODEL
GRID_W = 64
LORA_W = 64
GATE_LORA = 128
RMS_EPS = 1e-6
LN_EPS = 1e-5
GN_EPS = 64e-5

LANES = 128
SUBLANES = 8
TM = 256
TM_MLP = 512
CHUNK = 64
N_CHUNK = TM // CHUNK
N_PAIR = RWKV_WIDTH // LANES
HALO = SUBLANES
CONV_PAD = 16
CONV_ROWS = 32
VMEM_LIMIT = 56 * 1024 * 1024

F32 = jnp.float32
BF16 = jnp.bfloat16
DECAY_SCALE = 0.6065306597126334


def _mm(a, b):
    return jnp.dot(a.astype(BF16), b.astype(BF16), preferred_element_type=F32)


def _mm_nt(a, b):
    return lax.dot_general(a.astype(BF16), b.astype(BF16), (((1,), (1,)), ((), ())),
                           preferred_element_type=F32)


def _split3(x):
    x1 = x.astype(BF16)
    r1 = x - x1.astype(F32)
    x2 = r1.astype(BF16)
    x3 = (r1 - x2.astype(F32)).astype(BF16)
    return x3, x2, x1


def _mm_split3(m, x):
    x3, x2, x1 = (jnp.dot(m, t, preferred_element_type=F32) for t in _split3(x))
    return x3 + x2 + x1


def _head_sum(x, bd):
    rows = x.shape[0]
    n_grp = x.shape[1] // LANES
    terms = []
    for p in range(n_grp):
        xp = x[:, p * LANES:(p + 1) * LANES]
        head = xp.astype(BF16)
        terms += [head, (xp - head.astype(F32)).astype(BF16)]
    prod = jnp.dot(jnp.concatenate(terms, axis=0), bd, preferred_element_type=F32)
    sums = [prod[(2 * p) * rows:(2 * p + 1) * rows] + prod[(2 * p + 1) * rows:(2 * p + 2) * rows]
            for p in range(n_grp)]
    return jnp.concatenate(sums, axis=1)


def _block_diag_ones():
    ri = lax.broadcasted_iota(jnp.int32, (LANES, LANES), 0)
    ci = lax.broadcasted_iota(jnp.int32, (LANES, LANES), 1)
    shift = HEAD_DIM.bit_length() - 1
    return jnp.where((ri >> shift) == (ci >> shift), 1.0, 0.0).astype(BF16)


def _rms(x):
    return x * lax.rsqrt(jnp.mean(x * x, axis=-1, keepdims=True) + RMS_EPS)


def _ada_kernel(c_ref, w_ref, b_ref, o_ref):
    c = c_ref[...]
    w = w_ref[...].astype(BF16)
    s3, s2, s1 = (jnp.dot(t, w, preferred_element_type=F32) for t in _split3(c * jax.nn.sigmoid(c)))
    o_ref[...] = s3 + s2 + s1 + b_ref[...]


def _ada(cond, w, b):
    n = w.shape[1]
    tn = 2048
    return pl.pallas_call(
        _ada_kernel,
        grid=(n // tn,),
        in_specs=[pl.BlockSpec((8, D_MODEL), lambda i: (0, 0)),
                  pl.BlockSpec((D_MODEL, tn), lambda i: (0, i)),
                  pl.BlockSpec((1, tn), lambda i: (0, i))],
        out_specs=pl.BlockSpec((8, tn), lambda i: (0, i)),
        out_shape=jax.ShapeDtypeStruct((8, n), F32),
        compiler_params=pltpu.CompilerParams(dimension_semantics=("arbitrary",),
                                             vmem_limit_bytes=VMEM_LIMIT),
        name="ada",
    )(cond, w, b.reshape(1, n))


_IN_SPLIT = (3 * RWKV_WIDTH, 2 * CONV_WIDTH, 2 * D_MODEL)
_LORA_SPLIT = (4 * LORA_W, GATE_LORA)


def _run(*streams):
    streams = list(streams)
    while streams:
        streams = [g for g in streams if next(g, True) is None]


def _conv_module(uu, cw_ref, cb_ref, clg_ref, clb_ref, pad_scr, seq):
    nseq = TM // seq
    zpad = jnp.zeros((nseq, CONV_PAD, CONV_WIDTH), F32)
    pad_scr[0, :, 0:CONV_PAD, :] = zpad
    pad_scr[0, :, CONV_PAD + seq:, :] = zpad
    for s in range(nseq):
        pad_scr[0, s, CONV_PAD:CONV_PAD + seq, :] = uu[s * seq:(s + 1) * seq]
    live = seq + 2 * CONV_PAD - SUBLANES
    for m in range(1, SUBLANES):
        pad_scr[m, :, 0:live, :] = pad_scr[0, :, m:m + live, :]
    first = CONV_PAD - CONV_K // 2
    groups = (CONV_ROWS // SUBLANES, SUBLANES, CONV_WIDTH)
    parts = []
    for s in range(nseq):
        for r0 in range(0, seq, CONV_ROWS):
            acc = jnp.broadcast_to(cb_ref[...], groups)
            for t in range(CONV_K):
                m = (first + t) % SUBLANES
                lo = r0 + first + t - m
                acc = acc + pad_scr[m, s, lo:lo + CONV_ROWS, :].reshape(groups) * cw_ref[t][None]
            parts.append(acc.reshape(CONV_ROWS, CONV_WIDTH))
    cv = jnp.concatenate(parts, axis=0)
    cm = cv - jnp.mean(cv, axis=-1, keepdims=True)
    cvar = jnp.mean(cm * cm, axis=-1, keepdims=True)
    un = cm * lax.rsqrt(cvar + LN_EPS) * clg_ref[...] + clb_ref[...]
    return un * jax.nn.sigmoid(un)


def _front_kernel(x_ref, mod_ref, g_ref, win_ref, wlo_ref, rkv_ref, glu_ref, gates_ref, lora_ref, lg_ref):
    mod = mod_ref[0]
    sh1 = mod[:, 0:D_MODEL]
    sc1 = mod[:, D_MODEL:2 * D_MODEL]
    xn = (_rms(x_ref[...]) * g_ref[...] * (1.0 + sc1) + sh1).astype(BF16)
    c0, c1, c2 = itertools.accumulate(_IN_SPLIT)
    dot = functools.partial(jnp.dot, preferred_element_type=F32)
    rkv_ref[...] = dot(xn, win_ref[:, 0:c0])
    glu = dot(xn, win_ref[:, c0:c1])
    glu_ref[...] = glu[:, 0:CONV_WIDTH] * jax.nn.sigmoid(glu[:, CONV_WIDTH:])
    gates_ref[...] = jax.nn.sigmoid(dot(xn, win_ref[:, c1:c2]))
    lora_ref[...] = dot(xn, wlo_ref[:, 0:_LORA_SPLIT[0]])
    lg_ref[...] = dot(xn, wlo_ref[:, _LORA_SPLIT[0]:])


def _front(x, mod3, mod_row, g, w_in, w_lora):
    n = x.shape[0]
    const = lambda i: (0, 0)
    tok = lambda i: (i, 0)
    widths = (_IN_SPLIT[0], CONV_WIDTH, _IN_SPLIT[2]) + _LORA_SPLIT
    return pl.pallas_call(
        _front_kernel,
        grid=(n // TM,),
        in_specs=[pl.BlockSpec((TM, D_MODEL), tok),
                  pl.BlockSpec((1, 1, 6 * D_MODEL), lambda i: (mod_row(i), 0, 0)),
                  pl.BlockSpec((1, D_MODEL), const),
                  pl.BlockSpec(w_in.shape, const),
                  pl.BlockSpec(w_lora.shape, const)],
        out_specs=[pl.BlockSpec((TM, w), tok) for w in widths],
        out_shape=[jax.ShapeDtypeStruct((n, w), F32) for w in widths],
        compiler_params=pltpu.CompilerParams(dimension_semantics=("arbitrary",),
                                             vmem_limit_bytes=VMEM_LIMIT),
        name="front",
    )(x, mod3, g, w_in, w_lora)


def _wkv_masks(d):
    shift = CHUNK.bit_length() - 1
    ri = lax.broadcasted_iota(jnp.int32, (LANES, LANES), 0)
    ci = lax.broadcasted_iota(jnp.int32, (LANES, LANES), 1)
    same = (ri >> shift) == (ci >> shift)
    rel = (ci - ri) * (1 - 2 * d)
    levels = [((ri >> (lv + 1)) == (ci >> (lv + 1))) & (((ri >> lv) & 1) == 1 - d) & (((ci >> lv) & 1) == d)
              for lv in range(shift)]
    rc = lax.broadcasted_iota(jnp.int32, (LANES // 2, LANES), 0)
    cc = lax.broadcasted_iota(jnp.int32, (LANES // 2, LANES), 1)
    compact = [((cc >> (lv + 1)) == (rc >> lv)) & (((cc >> lv) & 1) == d) for lv in range(shift)]
    return same & (rel < 0), same & (rel <= 0), levels, compact


def _wkv_kernel(zero_ref, rkv_ref, prev_ref, next_ref, lora_ref, s0_ref, mup_ref, mun_ref, kk_ref, ka_ref,
                rk_ref, w0_ref, w2_ref, a0_ref, a2_ref,
                y_ref, bonus_ref, sout_ref,
                s_scr, r_scr, kd_scr, v_scr, aa_scr, b_scr, lw_scr, cl_scr, bn_scr,
                rhs_scr, out_scr, upd_scr, ti_scr, vs_scr, wc_scr, aab_scr, t_scr, *, d, nseq, nblk):
    g = pl.program_id(0)
    n_tile = nseq * nblk
    n_prob = N_CHUNK * N_PAIR
    j_a = jnp.minimum(g, n_tile - 1) % nblk
    j_b = jnp.maximum(g - 1, 0) % nblk
    masks = _wkv_masks(d)
    row0 = zero_ref[0]
    shift = CHUNK.bit_length() - 1
    ri = lax.broadcasted_iota(jnp.int32, (LANES, LANES), 0)
    ci = lax.broadcasted_iota(jnp.int32, (LANES, LANES), 1)
    eye = (ri == ci).astype(F32)
    head0 = lax.broadcasted_iota(jnp.int32, (CHUNK, LANES), 1) < HEAD_DIM
    dot = functools.partial(jnp.dot, preferred_element_type=F32)

    def stack(t):
        return jnp.concatenate([jnp.where(head0, t, 0.0), jnp.where(head0, 0.0, t)], axis=0)

    def twice(t):
        return jnp.concatenate([t, t], axis=0)

    def chunk_rows(step, anchored=False):
        lo = (N_CHUNK - 1 - step if d else step) * CHUNK
        return pl.ds(pl.multiple_of(row0 + lo, CHUNK), CHUNK) if anchored else slice(lo, lo + CHUNK)

    def prepare(slot, j):
        blk = nblk - 1 - j if d else j
        u = rkv_ref[...]
        row = lax.broadcasted_iota(jnp.int32, (TM, 1), 0)
        prow = jnp.where(blk > 0, prev_ref[HALO - 1:HALO, :], 0.0)
        nrow = jnp.where(blk < nblk - 1, next_ref[0:1, :], 0.0)
        prev = jnp.where(row == 0, prow, pltpu.roll(u, 1, axis=0))
        nxt = jnp.where(row == TM - 1, nrow, pltpu.roll(u, TM - 1, axis=0))
        x = u + (prev - u) * mup_ref[...] + (nxt - u) * mun_ref[...]
        r = x[:, 0:RWKV_WIDTH]
        k = x[:, RWKV_WIDTH:2 * RWKV_WIDTH]
        v = x[:, 2 * RWKV_WIDTH:]
        r_scr[slot] = r
        v_scr[slot] = v
        yield
        bd = _block_diag_ones()
        kkr = k * kk_ref[...]
        kk = kkr * lax.rsqrt(jnp.maximum(_head_sum(kkr * kkr, bd), 1e-24))
        aa_scr[slot] = -kk
        yield
        lo = lora_ref[...]
        z = w0_ref[...] + _mm(jnp.tanh(lo[:, 0:2 * LORA_W]), w2_ref[...])
        lw = -DECAY_SCALE * jax.nn.sigmoid(z)
        lw_scr[slot, 0:TM, :] = lw
        a = jax.nn.sigmoid(a0_ref[...] + _mm(lo[:, 2 * LORA_W:4 * LORA_W], a2_ref[...]))
        kd = k * (1.0 + (a - 1.0) * ka_ref[...])
        kd_scr[slot] = kd
        b_scr[slot] = kk * a
        yield
        bn_scr[slot] = _head_sum(r * kd * rk_ref[...], bd) * v
        yield
        rt = lax.broadcasted_iota(jnp.int32, (TM, TM), 0)
        ct = lax.broadcasted_iota(jnp.int32, (TM, TM), 1)
        cum_m = jnp.where(((rt >> shift) == (ct >> shift)) & (((ct - rt) * (1 - 2 * d)) <= 0), 1.0, 0.0)
        cl_scr[slot] = _mm_split3(cum_m.astype(BF16), lw)
        yield

    def operands(step, slot):
        strict, incl, levels, _ = masks
        rows = chunk_rows(step, anchored=True)
        for p in range(N_PAIR):
            i = slot * n_prob + step * N_PAIR + p
            sl = slice(p * LANES, (p + 1) * LANES)
            lw = lw_scr[slot, rows, sl]
            cl = cl_scr[slot, rows, sl]
            tot = jnp.sum(lw, axis=0, keepdims=True)
            e_out = jnp.exp(-cl)
            e_end = jnp.exp(tot - cl)
            b_c = b_scr[slot, rows, sl]
            kd_c = kd_scr[slot, rows, sl]
            a_s = stack(aa_scr[slot, rows, sl] * jnp.exp(cl - lw)).astype(BF16)
            r_s = stack(r_scr[slot, rows, sl] * jnp.exp(cl)).astype(BF16)
            bk_s = jnp.concatenate([twice((b_c * e_out).astype(BF16)),
                                    twice((kd_c * e_out).astype(BF16))], axis=0)
            vs_scr[i] = stack(v_scr[slot, rows, sl]).astype(BF16)
            upd_scr[i] = jnp.concatenate([jnp.transpose(stack(b_c * e_end)),
                                          jnp.transpose(stack(kd_c * e_end))], axis=1).astype(BF16)
            wc_scr[i] = jnp.transpose(jnp.broadcast_to(jnp.exp(tot), (LANES, LANES)))
            ab = _mm_nt(jnp.concatenate([a_s, r_s], axis=0), bk_s)
            a_ab = jnp.where(strict, ab[0:LANES, 0:LANES], 0.0)
            a_ak = jnp.where(strict, ab[0:LANES, LANES:], 0.0)
            a_rb = jnp.where(incl, ab[LANES:, 0:LANES], 0.0)
            a_rk = jnp.where(incl, ab[LANES:, LANES:], 0.0)
            rhs_scr[i] = jnp.concatenate([a_s, a_ak.astype(BF16)], axis=1)
            out_scr[i] = jnp.concatenate([r_s, a_rb.astype(BF16), a_rk.astype(BF16)], axis=1)
            aab_scr[i] = a_ab
            t_scr[i] = eye + jnp.where(levels[0], a_ab, 0.0)
            yield

    def invert(steps, slot):
        _, _, levels, compact = masks
        idx = [slot * n_prob + step * N_PAIR + p for step in steps for p in range(N_PAIR)]
        for lv in range(1, len(levels)):
            s = 1 << lv
            gather = s >= SUBLANES
            pieces = [slice((2 * q + 1 - d) * s, (2 * q + 2 - d) * s) for q in range(LANES // (2 * s))]

            def later(x):
                return jnp.concatenate([x[rows] for rows in pieces], axis=0)

            def spread(x):
                zero = jnp.zeros((s, LANES), x.dtype)
                halves = [[x[q * s:(q + 1) * s], zero] if d else [zero, x[q * s:(q + 1) * s]]
                          for q in range(len(pieces))]
                return jnp.concatenate([h for pair in halves for h in pair], axis=0)

            ts = [t_scr[i] for i in idx]
            tbs = [t.astype(BF16) for t in ts]
            if gather:
                ms = [dot(jnp.where(compact[lv], later(aab_scr[i]), 0.0).astype(BF16), tb)
                      for i, tb in zip(idx, tbs)]
            else:
                ms = [dot(jnp.where(levels[lv], aab_scr[i], 0.0).astype(BF16), tb) for i, tb in zip(idx, tbs)]
            lw_scr[slot, TM:, 0:LANES] = ms[-1][0:SUBLANES, :]
            yield
            for i, t, tb, m in zip(idx, ts, tbs, ms):
                if gather:
                    z = dot(later(t).astype(BF16), spread(m.astype(BF16)))
                    for q, rows in enumerate(pieces):
                        t_scr[i, rows, :] = t[rows] + z[q * s:(q + 1) * s]
                else:
                    t_scr[i] = t + dot(tb, m.astype(BF16))
            lw_scr[slot, TM:, 0:LANES] = ms[0][0:SUBLANES, :]
            yield
        for i in idx:
            ti_scr[i] = t_scr[i].astype(BF16)

    def scan(step, slot):
        rows = chunk_rows(step)
        base = slot * n_prob + step * N_PAIR
        pairs = range(N_PAIR)
        sts = [s_scr[p] for p in pairs]
        stb = [st.astype(BF16) for st in sts]
        vss = [vs_scr[base + p] for p in pairs]
        rhs = [dot(rhs_scr[base + p], jnp.concatenate([stb[p], vss[p]], axis=0)) for p in pairs]
        yield
        pmb = []
        for p in pairs:
            rh = rhs[p].astype(BF16)
            rl = (rhs[p] - rh.astype(F32)).astype(BF16)
            pm = dot(ti_scr[base + p], jnp.concatenate([rh, rl], axis=1))
            pmb.append((pm[:, 0:LANES] + pm[:, LANES:]).astype(BF16))
        yield
        for p in pairs:
            s_scr[p] = wc_scr[base + p] * sts[p] + dot(
                upd_scr[base + p], jnp.concatenate([pmb[p], vss[p]], axis=0))
        yield
        for p in pairs:
            y_s = dot(out_scr[base + p], jnp.concatenate([stb[p], pmb[p], vss[p]], axis=0))
            y_ref[rows, p * LANES:(p + 1) * LANES] = y_s[0:CHUNK] + y_s[CHUNK:]
        yield

    first, second = range(0, N_CHUNK // 2), range(N_CHUNK // 2, N_CHUNK)
    chain = itertools.chain.from_iterable

    def stage_a(slot):
        return prepare(slot, j_a), chain(operands(s, slot) for s in first)

    def stage_b(slot, with_inverse=(), with_scan=()):
        @pl.when(j_b == 0)
        def _():
            zero = jnp.zeros((HEAD_DIM, HEAD_DIM), F32)
            for p in range(N_PAIR):
                top = jnp.concatenate([s0_ref[0, 2 * p], zero], axis=1)
                bot = jnp.concatenate([zero, s0_ref[0, 2 * p + 1]], axis=1)
                s_scr[p] = jnp.transpose(jnp.concatenate([top, bot], axis=0))

        bonus_ref[...] = bn_scr[slot]
        _run(invert(first, slot), chain(operands(s, slot) for s in second))
        _run(invert(second, slot), chain(scan(s, slot) for s in first), *with_inverse)
        _run(chain(scan(s, slot) for s in second), *with_scan)

        @pl.when(j_b == nblk - 1)
        def _():
            for p in range(N_PAIR):
                s_vk = jnp.transpose(s_scr[p])
                sout_ref[0, 2 * p] = s_vk[0:HEAD_DIM, 0:HEAD_DIM]
                sout_ref[0, 2 * p + 1] = s_vk[HEAD_DIM:, HEAD_DIM:]

    @pl.when(g == 0)
    def _():
        prep, ops = stage_a(0)
        _run(prep)
        _run(ops)

    for parity in range(2):
        @pl.when((g > 0) & (g < n_tile) & (g % 2 == parity))
        def _():
            prep, ops = stage_a(parity)
            stage_b(1 - parity, with_inverse=(prep,), with_scan=(ops,))

    @pl.when(g == n_tile)
    def _():
        stage_b((n_tile - 1) % 2)


def _wkv(rkv, lora, s0, mup, mun, k_k, k_a, r_k, w0, w2p, a0, a2p, *, d, nseq, nblk):
    n = rkv.shape[0]
    hb = TM // HALO
    n_tile = nseq * nblk

    def tile(t):
        s = t // nblk
        j = t - s * nblk
        return s, s * nblk + (nblk - 1 - j if d else j)

    ahead = lambda g: tile(jnp.minimum(g, n_tile - 1))
    behind = lambda g: tile(jnp.maximum(g - 1, 0))
    s0_seq = (lambda s: s) if s0.shape[0] > 1 else (lambda s: 0)
    const = lambda g: (0, 0)
    out_tile = lambda g: (behind(g)[1], 0)
    kern = functools.partial(_wkv_kernel, d=d, nseq=nseq, nblk=nblk)
    tok = pltpu.VMEM((2, TM, RWKV_WIDTH), F32)
    n_prob = 2 * N_CHUNK * N_PAIR
    return pl.pallas_call(
        kern,
        grid=(n_tile + 1,),
        in_specs=[
            pl.BlockSpec(memory_space=pltpu.SMEM),
            pl.BlockSpec((TM, 3 * RWKV_WIDTH), lambda g: (ahead(g)[1], 0)),
            pl.BlockSpec((HALO, 3 * RWKV_WIDTH), lambda g: (jnp.maximum(ahead(g)[1] * hb - 1, 0), 0)),
            pl.BlockSpec((HALO, 3 * RWKV_WIDTH),
                         lambda g: (jnp.minimum((ahead(g)[1] + 1) * hb, n // HALO - 1), 0)),
            pl.BlockSpec((TM, 4 * LORA_W), lambda g: (ahead(g)[1], 0)),
            pl.BlockSpec((1, N_HEADS, HEAD_DIM, HEAD_DIM), lambda g: (s0_seq(behind(g)[0]), 0, 0, 0)),
            pl.BlockSpec((1, 3 * RWKV_WIDTH), const),
            pl.BlockSpec((1, 3 * RWKV_WIDTH), const),
            pl.BlockSpec((1, RWKV_WIDTH), const),
            pl.BlockSpec((1, RWKV_WIDTH), const),
            pl.BlockSpec((1, RWKV_WIDTH), const),
            pl.BlockSpec((1, RWKV_WIDTH), const),
            pl.BlockSpec((2 * LORA_W, RWKV_WIDTH), const),
            pl.BlockSpec((1, RWKV_WIDTH), const),
            pl.BlockSpec((2 * LORA_W, RWKV_WIDTH), const),
        ],
        out_specs=[
            pl.BlockSpec((TM, RWKV_WIDTH), out_tile),
            pl.BlockSpec((TM, RWKV_WIDTH), out_tile),
            pl.BlockSpec((1, N_HEADS, HEAD_DIM, HEAD_DIM), lambda g: (behind(g)[0], 0, 0, 0)),
        ],
        out_shape=[jax.ShapeDtypeStruct((n, RWKV_WIDTH), F32),
                   jax.ShapeDtypeStruct((n, RWKV_WIDTH), F32),
                   jax.ShapeDtypeStruct((nseq, N_HEADS, HEAD_DIM, HEAD_DIM), F32)],
        scratch_shapes=[pltpu.VMEM((N_PAIR, LANES, LANES), F32),
                        tok, tok, tok, tok, tok,
                        pltpu.VMEM((2, TM + SUBLANES, RWKV_WIDTH), F32),
                        tok, tok,
                        pltpu.VMEM((n_prob, LANES, 2 * LANES), BF16),
                        pltpu.VMEM((n_prob, LANES, 3 * LANES), BF16),
                        pltpu.VMEM((n_prob, LANES, 2 * LANES), BF16),
                        pltpu.VMEM((n_prob, LANES, LANES), BF16),
                        pltpu.VMEM((n_prob, LANES, LANES), BF16),
                        pltpu.VMEM((n_prob, LANES, LANES), F32),
                        pltpu.VMEM((n_prob, LANES, LANES), F32),
                        pltpu.VMEM((n_prob, LANES, LANES), F32)],
        compiler_params=pltpu.CompilerParams(dimension_semantics=("arbitrary",),
                                             vmem_limit_bytes=VMEM_LIMIT),
        name="wkv",
    )(jnp.zeros((1,), jnp.int32), rkv, rkv, rkv, lora, s0, mup, mun, k_k, k_a, r_k, w0, w2p, a0, a2p)


def _mid_kernel(yf_ref, yb_ref, bf_ref, bb_ref, lg_ref, glu_ref, gates_ref, x_ref, mod_ref, g2w_ref, lnxg_ref,
                lnxb_ref, wor_ref, cw_ref, cb_ref, clg_ref, clb_ref, woc_ref, wo_ref, n2g_ref,
                x1_ref, xn2_ref, pad_scr, *, seq):
    mod = mod_ref[0]
    g1 = mod[:, 2 * D_MODEL:3 * D_MODEL]
    sh2 = mod[:, 3 * D_MODEL:4 * D_MODEL]
    sc2 = mod[:, 4 * D_MODEL:5 * D_MODEL]

    bd = _block_diag_ones()
    y = yf_ref[...] + yb_ref[...]
    yc = y - _head_sum(y, bd) * (1.0 / HEAD_DIM)
    var = _head_sum(yc * yc, bd) * (1.0 / HEAD_DIM)
    yn = yc * lax.rsqrt(var + GN_EPS) * lnxg_ref[...] + lnxb_ref[...]
    g = _mm(jax.nn.sigmoid(lg_ref[...]), g2w_ref[...])
    y_r = _mm((yn + bf_ref[...] + bb_ref[...]) * g, wor_ref[...])

    y_c = _mm(_conv_module(glu_ref[...], cw_ref, cb_ref, clg_ref, clb_ref, pad_scr, seq), woc_ref[...])

    gates = gates_ref[...]
    merged = gates[:, 0:D_MODEL] * y_r + gates[:, D_MODEL:] * y_c
    x1 = x_ref[...] + g1 * _mm(merged, wo_ref[...])
    x1_ref[...] = x1
    xn2_ref[...] = (_rms(x1) * n2g_ref[...] * (1.0 + sc2) + sh2).astype(BF16)


def _mid(y_f, y_b, bn_f, bn_b, lg, glu, gates, x, mod3, mod_row, g2w, lnxg, lnxb, wor, cw, cb, clg, clb, woc,
         wo, n2g, *, seq):
    n = x.shape[0]
    const = lambda i: (0, 0)
    tok = lambda i: (i, 0)

    def full(a):
        return pl.BlockSpec(a.shape, lambda i: (0,) * a.ndim)

    return pl.pallas_call(
        functools.partial(_mid_kernel, seq=seq),
        grid=(n // TM,),
        in_specs=[pl.BlockSpec((TM, RWKV_WIDTH), tok)] * 4 + [
                  pl.BlockSpec((TM, GATE_LORA), tok),
                  pl.BlockSpec((TM, CONV_WIDTH), tok),
                  pl.BlockSpec((TM, 2 * D_MODEL), tok),
                  pl.BlockSpec((TM, D_MODEL), tok),
                  pl.BlockSpec((1, 1, 6 * D_MODEL), lambda i: (mod_row(i), 0, 0)),
                  full(g2w), full(lnxg), full(lnxb), full(wor), full(cw), full(cb), full(clg),
                  full(clb), full(woc), full(wo), full(n2g)],
        out_specs=[pl.BlockSpec((TM, D_MODEL), tok), pl.BlockSpec((TM, D_MODEL), tok)],
        out_shape=[jax.ShapeDtypeStruct((n, D_MODEL), F32),
                   jax.ShapeDtypeStruct((n, D_MODEL), BF16)],
        scratch_shapes=[pltpu.VMEM((SUBLANES, TM // seq, seq + 2 * CONV_PAD, CONV_WIDTH), F32)],
        compiler_params=pltpu.CompilerParams(dimension_semantics=("arbitrary",),
                                             vmem_limit_bytes=VMEM_LIMIT),
        name="mid",
    )(y_f, y_b, bn_f, bn_b, lg, glu, gates, x, mod3, g2w, lnxg, lnxb, wor, cw, cb, clg, clb, woc, wo, n2g)


def _mlp_kernel(x1_ref, xn2_ref, mod_ref, w1_ref, w2_ref, fg_ref, o_ref):
    g2 = mod_ref[0][:, 5 * D_MODEL:6 * D_MODEL]
    h = jnp.maximum(jnp.dot(xn2_ref[...], w1_ref[...], preferred_element_type=F32), 0.0)
    x2 = x1_ref[...] + g2 * _mm(h * h, w2_ref[...])
    o_ref[...] = _rms(x2) * fg_ref[...]


def _mlp(x1, xn2, mod3, mod_row, w1, w2, fg):
    n = x1.shape[0]
    const = lambda i: (0, 0)
    tok = lambda i: (i, 0)
    once = pl.Buffered(1)
    return pl.pallas_call(
        _mlp_kernel,
        grid=(n // TM_MLP,),
        in_specs=[pl.BlockSpec((TM_MLP, D_MODEL), tok),
                  pl.BlockSpec((TM_MLP, D_MODEL), tok),
                  pl.BlockSpec((1, 1, 6 * D_MODEL), lambda i: (mod_row(i * (TM_MLP // TM)), 0, 0)),
                  pl.BlockSpec(w1.shape, const, pipeline_mode=once),
                  pl.BlockSpec(w2.shape, const, pipeline_mode=once),
                  pl.BlockSpec((1, D_MODEL), const)],
        out_specs=pl.BlockSpec((TM_MLP, D_MODEL), tok),
        out_shape=jax.ShapeDtypeStruct((n, D_MODEL), F32),
        compiler_params=pltpu.CompilerParams(dimension_semantics=("arbitrary",),
                                             vmem_limit_bytes=VMEM_LIMIT),
        name="mlp",
    )(x1, xn2, mod3, w1, w2, fg)


def _pad_dir(w):
    return (jnp.eye(2, dtype=w.dtype)[:, :, None, None] * w[:, None]).reshape(2, 2 * w.shape[1], w.shape[2])


def kernel(x_prompt, x_sample, state_fwd, state_bwd, c, c_ctx, ada_w, ada_b, norm1_g, norm2_g, w_in,
           mu_prev, mu_next, decay_w0, decay_w1, decay_w2, iclr_a0, iclr_a1, iclr_a2, gate_g1, gate_g2,
           k_k, k_a, r_k, lnx_g, lnx_b, w_out_rwkv, conv_w, conv_b, conv_ln_g, conv_ln_b, w_out_conv,
           w_o, mlp_w1, mlp_w2, final_g):
    n_ctx, seq_ctx, _ = x_prompt.shape
    n_lat, seq_lat, _ = x_sample.shape
    depth = ada_w.shape[0]
    assert seq_ctx == TM and seq_lat % TM == 0 and GRID_W == CHUNK and c.shape[0] + 1 <= 8
    lat_blk = seq_lat // TM
    row_ctx = lambda i: 0
    row_lat = lambda i: 1 + i // lat_blk

    xp = x_prompt.reshape(n_ctx * seq_ctx, D_MODEL)
    xs = x_sample.reshape(n_lat * seq_lat, D_MODEL)
    cond = jnp.concatenate([c_ctx[None, :], c, jnp.zeros((7 - c.shape[0], D_MODEL), F32)], axis=0)
    zero_state = (jnp.zeros((1, N_HEADS, HEAD_DIM, HEAD_DIM), F32),) * 2
    row = lambda a: a.reshape(1, -1)
    assert depth == 1
    lay = lambda a: a.reshape(a.shape[1:])
    bf = lambda a: lay(a).astype(BF16)

    mod3 = _ada(cond, lay(ada_w), lay(ada_b)).reshape(8, 1, 6 * D_MODEL)
    w_lora = jnp.concatenate([decay_w1[0, 0], decay_w1[0, 1], iclr_a1[0, 0], iclr_a1[0, 1], gate_g1[0]],
                             axis=1).astype(BF16)
    conv_taps = jnp.broadcast_to(conv_w[0, :, None, :], (CONV_K, SUBLANES, CONV_WIDTH))
    front_w = (norm1_g, bf(w_in), w_lora)
    w2p = _pad_dir(lay(decay_w2)).astype(BF16)
    a2p = _pad_dir(lay(iclr_a2)).astype(BF16)
    wkv_w = [(mu_prev, mu_next, k_k, k_a, row(r_k), decay_w0[:, d], w2p[d], iclr_a0[:, d], a2p[d])
             for d in range(2)]
    mid_w = (bf(gate_g2), lnx_g, lnx_b, bf(w_out_rwkv), conv_taps, conv_b, conv_ln_g, conv_ln_b,
             bf(w_out_conv), bf(w_o), norm2_g)
    mlp_w = (bf(mlp_w1), bf(mlp_w2), row(final_g))
    s0_lat = (state_fwd[:, 0].astype(F32), state_bwd[:, 0].astype(F32))

    def layer(x, mod_row, s0, nseq, nblk, conv_seq):
        rkv, glu, gates, lora, lg = _front(x, mod3, mod_row, *front_w)
        y_f, bn_f, s_f = _wkv(rkv, lora, s0[0], *wkv_w[0], d=0, nseq=nseq, nblk=nblk)
        y_b, bn_b, s_b = _wkv(rkv, lora, s0[1], *wkv_w[1], d=1, nseq=nseq, nblk=nblk)
        x1, xn2 = _mid(y_f, y_b, bn_f, bn_b, lg, glu, gates, x, mod3, mod_row, *mid_w, seq=conv_seq)
        return _mlp(x1, xn2, mod3, mod_row, *mlp_w), (s_f, s_b)

    yp, s_ctx = layer(xp, row_ctx, zero_state, n_ctx, 1, seq_ctx)
    ys, _ = layer(xs, row_lat, s0_lat, n_lat, lat_blk, GRID_W)
    s_f, s_b = (s.astype(x_prompt.dtype)[:, None] for s in s_ctx)
    return (yp.reshape(x_prompt.shape), ys.reshape(x_sample.shape), s_f, s_b)
```

```python
import functools
import itertools

import jax
import jax.numpy as jnp
from jax import lax
from jax.experimental import pallas as pl
from jax.experimental.pallas import tpu as pltpu

D_MODEL = 1024
RWKV_WIDTH = 512
HEAD_DIM = 64
N_HEADS = RWKV_WIDTH // HEAD_DIM
CONV_WIDTH = 512
CONV_K = 31
D_FF = 4 * D_MODEL
GRID_W = 64
LORA_W = 64
GATE_LORA = 128
RMS_EPS = 1e-6
LN_EPS = 1e-5
GN_EPS = 64e-5

LANES = 128
SUBLANES = 8
TM = 256
TM_MLP = 512
COND_ROWS = 16
CHUNK = 64
N_CHUNK = TM // CHUNK
N_PAIR = RWKV_WIDTH // LANES
HALO = SUBLANES
CONV_PAD = 16
CONV_ROWS = 32
VMEM_LIMIT = 56 * 1024 * 1024

F32 = jnp.float32
BF16 = jnp.bfloat16
DECAY_SCALE = 0.6065306597126334


def _mm(a, b):
    return jnp.dot(a.astype(BF16), b.astype(BF16), preferred_element_type=F32)


def _mm_nt(a, b):
    return lax.dot_general(a.astype(BF16), b.astype(BF16), (((1,), (1,)), ((), ())),
                           preferred_element_type=F32)


def _split3(x):
    x1 = x.astype(BF16)
    r1 = x - x1.astype(F32)
    x2 = r1.astype(BF16)
    x3 = (r1 - x2.astype(F32)).astype(BF16)
    return x3, x2, x1


def _mm_split3(m, x):
    x3, x2, x1 = (jnp.dot(m, t, preferred_element_type=F32) for t in _split3(x))
    return x3 + x2 + x1


def _head_sum(x, bd):
    rows = x.shape[0]
    n_grp = x.shape[1] // LANES
    terms = []
    for p in range(n_grp):
        xp = x[:, p * LANES:(p + 1) * LANES]
        head = xp.astype(BF16)
        terms += [head, (xp - head.astype(F32)).astype(BF16)]
    prod = jnp.dot(jnp.concatenate(terms, axis=0), bd, preferred_element_type=F32)
    sums = [prod[(2 * p) * rows:(2 * p + 1) * rows] + prod[(2 * p + 1) * rows:(2 * p + 2) * rows]
            for p in range(n_grp)]
    return jnp.concatenate(sums, axis=1)


def _block_diag_ones():
    ri = lax.broadcasted_iota(jnp.int32, (LANES, LANES), 0)
    ci = lax.broadcasted_iota(jnp.int32, (LANES, LANES), 1)
    shift = HEAD_DIM.bit_length() - 1
    return jnp.where((ri >> shift) == (ci >> shift), 1.0, 0.0).astype(BF16)


def _rms(x):
    return x * lax.rsqrt(jnp.mean(x * x, axis=-1, keepdims=True) + RMS_EPS)


def _ada_kernel(c_ref, w_ref, b_ref, o_ref):
    c = c_ref[...]
    rows = c.shape[0]
    terms = jnp.concatenate(_split3(c * jax.nn.sigmoid(c)), axis=0)
    prod = jnp.dot(terms, w_ref[...].astype(BF16), preferred_element_type=F32)
    o_ref[...] = prod[0:rows] + prod[rows:2 * rows] + prod[2 * rows:] + b_ref[...]


def _ada(cond, w, b):
    n = w.shape[1]
    rows = cond.shape[0]
    tn = 1024
    return pl.pallas_call(
        _ada_kernel,
        grid=(n // tn,),
        in_specs=[pl.BlockSpec((rows, D_MODEL), lambda i: (0, 0)),
                  pl.BlockSpec((D_MODEL, tn), lambda i: (0, i)),
                  pl.BlockSpec((1, tn), lambda i: (0, i))],
        out_specs=pl.BlockSpec((rows, tn), lambda i: (0, i)),
        out_shape=jax.ShapeDtypeStruct((rows, n), F32),
        compiler_params=pltpu.CompilerParams(dimension_semantics=("arbitrary",),
                                             vmem_limit_bytes=VMEM_LIMIT),
        name="ada",
    )(cond, w, b.reshape(1, n))


_IN_SPLIT = (3 * RWKV_WIDTH, 2 * CONV_WIDTH, 2 * D_MODEL)
_LORA_SPLIT = (4 * LORA_W, GATE_LORA)


def _run(*streams):
    streams = list(streams)
    while streams:
        streams = [g for g in streams if next(g, True) is None]


def _conv_module(uu, cw_ref, cb_ref, clg_ref, clb_ref, pad_scr, seq):
    nseq = TM // seq
    zpad = jnp.zeros((nseq, CONV_PAD, CONV_WIDTH), F32)
    pad_scr[0, :, 0:CONV_PAD, :] = zpad
    pad_scr[0, :, CONV_PAD + seq:, :] = zpad
    for s in range(nseq):
        pad_scr[0, s, CONV_PAD:CONV_PAD + seq, :] = uu[s * seq:(s + 1) * seq]
    live = seq + 2 * CONV_PAD - SUBLANES
    for m in range(1, SUBLANES):
        pad_scr[m, :, 0:live, :] = pad_scr[0, :, m:m + live, :]
    first = CONV_PAD - CONV_K // 2
    groups = (CONV_ROWS // SUBLANES, SUBLANES, CONV_WIDTH)
    parts = []
    for s in range(nseq):
        for r0 in range(0, seq, CONV_ROWS):
            acc = jnp.broadcast_to(cb_ref[...], groups)
            for t in range(CONV_K):
                m = (first + t) % SUBLANES
                lo = r0 + first + t - m
                acc = acc + pad_scr[m, s, lo:lo + CONV_ROWS, :].reshape(groups) * cw_ref[t][None]
            parts.append(acc.reshape(CONV_ROWS, CONV_WIDTH))
    cv = jnp.concatenate(parts, axis=0)
    cm = cv - jnp.mean(cv, axis=-1, keepdims=True)
    cvar = jnp.mean(cm * cm, axis=-1, keepdims=True)
    un = cm * lax.rsqrt(cvar + LN_EPS) * clg_ref[...] + clb_ref[...]
    return un * jax.nn.sigmoid(un)


def _front_kernel(x_ref, mod_ref, g_ref, win_ref, wlo_ref, rkv_ref, glu_ref, gates_ref, lora_ref, lg_ref):
    mod = mod_ref[0]
    sh1 = mod[:, 0:D_MODEL]
    sc1 = mod[:, D_MODEL:2 * D_MODEL]
    xn = (_rms(x_ref[...]) * g_ref[...] * (1.0 + sc1) + sh1).astype(BF16)
    c0, c1, c2 = itertools.accumulate(_IN_SPLIT)
    dot = functools.partial(jnp.dot, preferred_element_type=F32)
    rkv_ref[...] = dot(xn, win_ref[:, 0:c0])
    glu = dot(xn, win_ref[:, c0:c1])
    glu_ref[...] = glu[:, 0:CONV_WIDTH] * jax.nn.sigmoid(glu[:, CONV_WIDTH:])
    gates_ref[...] = jax.nn.sigmoid(dot(xn, win_ref[:, c1:c2]))
    lora_ref[...] = dot(xn, wlo_ref[:, 0:_LORA_SPLIT[0]])
    lg_ref[...] = dot(xn, wlo_ref[:, _LORA_SPLIT[0]:])


def _front(x, mod3, mod_row, g, w_in, w_lora):
    n = x.shape[0]
    const = lambda i: (0, 0)
    tok = lambda i: (i, 0)
    widths = (_IN_SPLIT[0], CONV_WIDTH, _IN_SPLIT[2]) + _LORA_SPLIT
    once = pl.Buffered(1)
    return pl.pallas_call(
        _front_kernel,
        grid=(n // TM_MLP,),
        in_specs=[pl.BlockSpec((TM_MLP, D_MODEL), tok),
                  pl.BlockSpec((1, 1, 6 * D_MODEL), lambda i: (mod_row(i * (TM_MLP // TM)), 0, 0)),
                  pl.BlockSpec((1, D_MODEL), const),
                  pl.BlockSpec(w_in.shape, const, pipeline_mode=once),
                  pl.BlockSpec(w_lora.shape, const, pipeline_mode=once)],
        out_specs=[pl.BlockSpec((TM_MLP, w), tok) for w in widths],
        out_shape=[jax.ShapeDtypeStruct((n, w), F32) for w in widths],
        compiler_params=pltpu.CompilerParams(dimension_semantics=("arbitrary",),
                                             vmem_limit_bytes=VMEM_LIMIT),
        name="front",
    )(x, mod3, g, w_in, w_lora)


def _wkv_masks(d):
    shift = CHUNK.bit_length() - 1
    ri = lax.broadcasted_iota(jnp.int32, (LANES, LANES), 0)
    ci = lax.broadcasted_iota(jnp.int32, (LANES, LANES), 1)
    same = (ri >> shift) == (ci >> shift)
    rel = (ci - ri) * (1 - 2 * d)
    levels = [((ri >> (lv + 1)) == (ci >> (lv + 1))) & (((ri >> lv) & 1) == 1 - d) & (((ci >> lv) & 1) == d)
              for lv in range(shift)]
    rc = lax.broadcasted_iota(jnp.int32, (LANES // 2, LANES), 0)
    cc = lax.broadcasted_iota(jnp.int32, (LANES // 2, LANES), 1)
    compact = [((cc >> (lv + 1)) == (rc >> lv)) & (((cc >> lv) & 1) == d) for lv in range(shift)]
    return same & (rel < 0), same & (rel <= 0), levels, compact


def _wkv_kernel(zero_ref, rkv_ref, prev_ref, next_ref, lora_ref, s0_ref, mup_ref, mun_ref, kk_ref, ka_ref,
                rk_ref, w0_ref, w2_ref, a0_ref, a2_ref,
                y_ref, bonus_ref, sout_ref,
                s_scr, r_scr, kd_scr, v_scr, aa_scr, b_scr, lw_scr, cl_scr, bn_scr,
                rhs_scr, out_scr, upd_scr, ti_scr, vs_scr, wc_scr, aab_scr, t_scr, *, d, nseq, nblk):
    g = pl.program_id(0)
    n_tile = nseq * nblk
    n_prob = N_CHUNK * N_PAIR
    j_a = jnp.minimum(g, n_tile - 1) % nblk
    j_b = jnp.maximum(g - 1, 0) % nblk
    masks = _wkv_masks(d)
    row0 = zero_ref[0]
    shift = CHUNK.bit_length() - 1
    ri = lax.broadcasted_iota(jnp.int32, (LANES, LANES), 0)
    ci = lax.broadcasted_iota(jnp.int32, (LANES, LANES), 1)
    eye = (ri == ci).astype(F32)
    head0 = lax.broadcasted_iota(jnp.int32, (CHUNK, LANES), 1) < HEAD_DIM
    dot = functools.partial(jnp.dot, preferred_element_type=F32)

    def stack(t):
        return jnp.concatenate([jnp.where(head0, t, 0.0), jnp.where(head0, 0.0, t)], axis=0)

    def twice(t):
        return jnp.concatenate([t, t], axis=0)

    def chunk_rows(step, anchored=False):
        lo = (N_CHUNK - 1 - step if d else step) * CHUNK
        return pl.ds(pl.multiple_of(row0 + lo, CHUNK), CHUNK) if anchored else slice(lo, lo + CHUNK)

    def prepare(slot, j):
        blk = nblk - 1 - j if d else j
        u = rkv_ref[...]
        row = lax.broadcasted_iota(jnp.int32, (TM, 1), 0)
        prow = jnp.where(blk > 0, prev_ref[HALO - 1:HALO, :], 0.0)
        nrow = jnp.where(blk < nblk - 1, next_ref[0:1, :], 0.0)
        prev = jnp.where(row == 0, prow, pltpu.roll(u, 1, axis=0))
        nxt = jnp.where(row == TM - 1, nrow, pltpu.roll(u, TM - 1, axis=0))
        x = u + (prev - u) * mup_ref[...] + (nxt - u) * mun_ref[...]
        r = x[:, 0:RWKV_WIDTH]
        k = x[:, RWKV_WIDTH:2 * RWKV_WIDTH]
        v = x[:, 2 * RWKV_WIDTH:]
        r_scr[slot] = r
        v_scr[slot] = v
        yield
        bd = _block_diag_ones()
        kkr = k * kk_ref[...]
        kk = kkr * lax.rsqrt(jnp.maximum(_head_sum(kkr * kkr, bd), 1e-24))
        aa_scr[slot] = -kk
        yield
        lo = lora_ref[...]
        z = w0_ref[...] + _mm(jnp.tanh(lo[:, 0:2 * LORA_W]), w2_ref[...])
        lw = -DECAY_SCALE * jax.nn.sigmoid(z)
        lw_scr[slot, 0:TM, :] = lw
        a = jax.nn.sigmoid(a0_ref[...] + _mm(lo[:, 2 * LORA_W:4 * LORA_W], a2_ref[...]))
        kd = k * (1.0 + (a - 1.0) * ka_ref[...])
        kd_scr[slot] = kd
        b_scr[slot] = kk * a
        yield
        bn_scr[slot] = _head_sum(r * kd * rk_ref[...], bd) * v
        yield
        rt = lax.broadcasted_iota(jnp.int32, (TM, TM), 0)
        ct = lax.broadcasted_iota(jnp.int32, (TM, TM), 1)
        cum_m = jnp.where(((rt >> shift) == (ct >> shift)) & (((ct - rt) * (1 - 2 * d)) <= 0), 1.0, 0.0)
        cl_scr[slot] = _mm_split3(cum_m.astype(BF16), lw)
        yield

    def operands(step, slot):
        strict, incl, levels, _ = masks
        rows = chunk_rows(step, anchored=True)
        for p in range(N_PAIR):
            i = slot * n_prob + step * N_PAIR + p
            sl = slice(p * LANES, (p + 1) * LANES)
            lw = lw_scr[slot, rows, sl]
            cl = cl_scr[slot, rows, sl]
            tot = jnp.sum(lw, axis=0, keepdims=True)
            e_out = jnp.exp(-cl)
            e_end = jnp.exp(tot - cl)
            b_c = b_scr[slot, rows, sl]
            kd_c = kd_scr[slot, rows, sl]
            a_s = stack(aa_scr[slot, rows, sl] * jnp.exp(cl - lw)).astype(BF16)
            r_s = stack(r_scr[slot, rows, sl] * jnp.exp(cl)).astype(BF16)
            bk_s = jnp.concatenate([twice((b_c * e_out).astype(BF16)),
                                    twice((kd_c * e_out).astype(BF16))], axis=0)
            vs_scr[i] = stack(v_scr[slot, rows, sl]).astype(BF16)
            upd_scr[i] = jnp.concatenate([jnp.transpose(stack(b_c * e_end)),
                                          jnp.transpose(stack(kd_c * e_end))], axis=1).astype(BF16)
            wc_scr[i] = jnp.transpose(jnp.broadcast_to(jnp.exp(tot), (LANES, LANES)))
            ab = _mm_nt(jnp.concatenate([a_s, r_s], axis=0), bk_s)
            a_ab = jnp.where(strict, ab[0:LANES, 0:LANES], 0.0)
            a_ak = jnp.where(strict, ab[0:LANES, LANES:], 0.0)
            a_rb = jnp.where(incl, ab[LANES:, 0:LANES], 0.0)
            a_rk = jnp.where(incl, ab[LANES:, LANES:], 0.0)
            rhs_scr[i] = jnp.concatenate([a_s, a_ak.astype(BF16)], axis=1)
            out_scr[i] = jnp.concatenate([r_s, a_rb.astype(BF16), a_rk.astype(BF16)], axis=1)
            aab_scr[i] = a_ab
            t_scr[i] = eye + jnp.where(levels[0], a_ab, 0.0)
            yield

    def invert(steps, slot):
        _, _, levels, compact = masks
        idx = [slot * n_prob + step * N_PAIR + p for step in steps for p in range(N_PAIR)]
        for lv in range(1, len(levels)):
            s = 1 << lv
            gather = s >= SUBLANES
            pieces = [slice((2 * q + 1 - d) * s, (2 * q + 2 - d) * s) for q in range(LANES // (2 * s))]

            def later(x):
                return jnp.concatenate([x[rows] for rows in pieces], axis=0)

            def spread(x):
                zero = jnp.zeros((s, LANES), x.dtype)
                halves = [[x[q * s:(q + 1) * s], zero] if d else [zero, x[q * s:(q + 1) * s]]
                          for q in range(len(pieces))]
                return jnp.concatenate([h for pair in halves for h in pair], axis=0)

            ts = [t_scr[i] for i in idx]
            tbs = [t.astype(BF16) for t in ts]
            if gather:
                ms = [dot(jnp.where(compact[lv], later(aab_scr[i]), 0.0).astype(BF16), tb)
                      for i, tb in zip(idx, tbs)]
            else:
                ms = [dot(jnp.where(levels[lv], aab_scr[i], 0.0).astype(BF16), tb) for i, tb in zip(idx, tbs)]
            lw_scr[slot, TM:, 0:LANES] = ms[-1][0:SUBLANES, :]
            yield
            for i, t, tb, m in zip(idx, ts, tbs, ms):
                if gather:
                    z = dot(later(t).astype(BF16), spread(m.astype(BF16)))
                    for q, rows in enumerate(pieces):
                        t_scr[i, rows, :] = t[rows] + z[q * s:(q + 1) * s]
                else:
                    t_scr[i] = t + dot(tb, m.astype(BF16))
            lw_scr[slot, TM:, 0:LANES] = ms[0][0:SUBLANES, :]
            yield
        for i in idx:
            ti_scr[i] = t_scr[i].astype(BF16)

    def scan(step, slot):
        rows = chunk_rows(step)
        base = slot * n_prob + step * N_PAIR
        pairs = range(N_PAIR)
        sts = [s_scr[p] for p in pairs]
        stb = [st.astype(BF16) for st in sts]
        vss = [vs_scr[base + p] for p in pairs]
        rhs = [dot(rhs_scr[base + p], jnp.concatenate([stb[p], vss[p]], axis=0)) for p in pairs]
        yield
        pmb = []
        for p in pairs:
            rh = rhs[p].astype(BF16)
            rl = (rhs[p] - rh.astype(F32)).astype(BF16)
            pm = dot(ti_scr[base + p], jnp.concatenate([rh, rl], axis=1))
            pmb.append((pm[:, 0:LANES] + pm[:, LANES:]).astype(BF16))
        yield
        for p in pairs:
            s_scr[p] = wc_scr[base + p] * sts[p] + dot(
                upd_scr[base + p], jnp.concatenate([pmb[p], vss[p]], axis=0))
        yield
        for p in pairs:
            y_s = dot(out_scr[base + p], jnp.concatenate([stb[p], pmb[p], vss[p]], axis=0))
            y_ref[rows, p * LANES:(p + 1) * LANES] = y_s[0:CHUNK] + y_s[CHUNK:]
        yield

    first, second = range(0, N_CHUNK // 2), range(N_CHUNK // 2, N_CHUNK)
    chain = itertools.chain.from_iterable

    def stage_a(slot):
        return prepare(slot, j_a), chain(operands(s, slot) for s in first)

    def stage_b(slot, with_inverse=(), with_scan=()):
        @pl.when(j_b == 0)
        def _():
            zero = jnp.zeros((HEAD_DIM, HEAD_DIM), F32)
            for p in range(N_PAIR):
                top = jnp.concatenate([s0_ref[0, 2 * p], zero], axis=1)
                bot = jnp.concatenate([zero, s0_ref[0, 2 * p + 1]], axis=1)
                s_scr[p] = jnp.transpose(jnp.concatenate([top, bot], axis=0))

        bonus_ref[...] = bn_scr[slot]
        _run(invert(first, slot), chain(operands(s, slot) for s in second))
        _run(invert(second, slot), chain(scan(s, slot) for s in first), *with_inverse)
        _run(chain(scan(s, slot) for s in second), *with_scan)

        @pl.when(j_b == nblk - 1)
        def _():
            for p in range(N_PAIR):
                s_vk = jnp.transpose(s_scr[p])
                sout_ref[0, 2 * p] = s_vk[0:HEAD_DIM, 0:HEAD_DIM]
                sout_ref[0, 2 * p + 1] = s_vk[HEAD_DIM:, HEAD_DIM:]

    @pl.when(g == 0)
    def _():
        prep, ops = stage_a(0)
        _run(prep)
        _run(ops)

    for parity in range(2):
        @pl.when((g > 0) & (g < n_tile) & (g % 2 == parity))
        def _():
            prep, ops = stage_a(parity)
            stage_b(1 - parity, with_inverse=(prep,), with_scan=(ops,))

    @pl.when(g == n_tile)
    def _():
        stage_b((n_tile - 1) % 2)


def _wkv(rkv, lora, s0, mup, mun, k_k, k_a, r_k, w0, w2p, a0, a2p, *, d, nseq, nblk):
    n = rkv.shape[0]
    hb = TM // HALO
    n_tile = nseq * nblk

    def tile(t):
        s = t // nblk
        j = t - s * nblk
        return s, s * nblk + (nblk - 1 - j if d else j)

    ahead = lambda g: tile(jnp.minimum(g, n_tile - 1))
    behind = lambda g: tile(jnp.maximum(g - 1, 0))
    s0_seq = (lambda s: s) if s0.shape[0] > 1 else (lambda s: 0)
    const = lambda g: (0, 0)
    out_tile = lambda g: (behind(g)[1], 0)
    kern = functools.partial(_wkv_kernel, d=d, nseq=nseq, nblk=nblk)
    tok = pltpu.VMEM((2, TM, RWKV_WIDTH), F32)
    n_prob = 2 * N_CHUNK * N_PAIR
    return pl.pallas_call(
        kern,
        grid=(n_tile + 1,),
        in_specs=[
            pl.BlockSpec(memory_space=pltpu.SMEM),
            pl.BlockSpec((TM, 3 * RWKV_WIDTH), lambda g: (ahead(g)[1], 0)),
            pl.BlockSpec((HALO, 3 * RWKV_WIDTH), lambda g: (jnp.maximum(ahead(g)[1] * hb - 1, 0), 0)),
            pl.BlockSpec((HALO, 3 * RWKV_WIDTH),
                         lambda g: (jnp.minimum((ahead(g)[1] + 1) * hb, n // HALO - 1), 0)),
            pl.BlockSpec((TM, 4 * LORA_W), lambda g: (ahead(g)[1], 0)),
            pl.BlockSpec((1, N_HEADS, HEAD_DIM, HEAD_DIM), lambda g: (s0_seq(behind(g)[0]), 0, 0, 0)),
            pl.BlockSpec((1, 3 * RWKV_WIDTH), const),
            pl.BlockSpec((1, 3 * RWKV_WIDTH), const),
            pl.BlockSpec((1, RWKV_WIDTH), const),
            pl.BlockSpec((1, RWKV_WIDTH), const),
            pl.BlockSpec((1, RWKV_WIDTH), const),
            pl.BlockSpec((1, RWKV_WIDTH), const),
            pl.BlockSpec((2 * LORA_W, RWKV_WIDTH), const),
            pl.BlockSpec((1, RWKV_WIDTH), const),
            pl.BlockSpec((2 * LORA_W, RWKV_WIDTH), const),
        ],
        out_specs=[
            pl.BlockSpec((TM, RWKV_WIDTH), out_tile),
            pl.BlockSpec((TM, RWKV_WIDTH), out_tile),
            pl.BlockSpec((1, N_HEADS, HEAD_DIM, HEAD_DIM), lambda g: (behind(g)[0], 0, 0, 0)),
        ],
        out_shape=[jax.ShapeDtypeStruct((n, RWKV_WIDTH), F32),
                   jax.ShapeDtypeStruct((n, RWKV_WIDTH), F32),
                   jax.ShapeDtypeStruct((nseq, N_HEADS, HEAD_DIM, HEAD_DIM), F32)],
        scratch_shapes=[pltpu.VMEM((N_PAIR, LANES, LANES), F32),
                        tok, tok, tok, tok, tok,
                        pltpu.VMEM((2, TM + SUBLANES, RWKV_WIDTH), F32),
                        tok, tok,
                        pltpu.VMEM((n_prob, LANES, 2 * LANES), BF16),
                        pltpu.VMEM((n_prob, LANES, 3 * LANES), BF16),
                        pltpu.VMEM((n_prob, LANES, 2 * LANES), BF16),
                        pltpu.VMEM((n_prob, LANES, LANES), BF16),
                        pltpu.VMEM((n_prob, LANES, LANES), BF16),
                        pltpu.VMEM((n_prob, LANES, LANES), F32),
                        pltpu.VMEM((n_prob, LANES, LANES), F32),
                        pltpu.VMEM((n_prob, LANES, LANES), F32)],
        compiler_params=pltpu.CompilerParams(dimension_semantics=("arbitrary",),
                                             vmem_limit_bytes=VMEM_LIMIT),
        name="wkv",
    )(jnp.zeros((1,), jnp.int32), rkv, rkv, rkv, lora, s0, mup, mun, k_k, k_a, r_k, w0, w2p, a0, a2p)


def _mid_kernel(yf_ref, yb_ref, bf_ref, bb_ref, lg_ref, glu_ref, gates_ref, x_ref, mod_ref, g2w_ref, lnxg_ref,
                lnxb_ref, wor_ref, cw_ref, cb_ref, clg_ref, clb_ref, woc_ref, wo_ref, n2g_ref,
                x1_ref, xn2_ref, pad_scr, *, seq):
    mod = mod_ref[0]
    g1 = mod[:, 2 * D_MODEL:3 * D_MODEL]
    sh2 = mod[:, 3 * D_MODEL:4 * D_MODEL]
    sc2 = mod[:, 4 * D_MODEL:5 * D_MODEL]

    bd = _block_diag_ones()
    y = yf_ref[...] + yb_ref[...]
    yc = y - _head_sum(y, bd) * (1.0 / HEAD_DIM)
    var = _head_sum(yc * yc, bd) * (1.0 / HEAD_DIM)
    yn = yc * lax.rsqrt(var + GN_EPS) * lnxg_ref[...] + lnxb_ref[...]
    g = _mm(jax.nn.sigmoid(lg_ref[...]), g2w_ref[...])
    y_r = _mm((yn + bf_ref[...] + bb_ref[...]) * g, wor_ref[...])

    y_c = _mm(_conv_module(glu_ref[...], cw_ref, cb_ref, clg_ref, clb_ref, pad_scr, seq), woc_ref[...])

    gates = gates_ref[...]
    merged = gates[:, 0:D_MODEL] * y_r + gates[:, D_MODEL:] * y_c
    x1 = x_ref[...] + g1 * _mm(merged, wo_ref[...])
    x1_ref[...] = x1
    xn2_ref[...] = (_rms(x1) * n2g_ref[...] * (1.0 + sc2) + sh2).astype(BF16)


def _mid(y_f, y_b, bn_f, bn_b, lg, glu, gates, x, mod3, mod_row, g2w, lnxg, lnxb, wor, cw, cb, clg, clb, woc,
         wo, n2g, *, seq):
    n = x.shape[0]
    const = lambda i: (0, 0)
    tok = lambda i: (i, 0)

    def full(a):
        return pl.BlockSpec(a.shape, lambda i: (0,) * a.ndim)

    return pl.pallas_call(
        functools.partial(_mid_kernel, seq=seq),
        grid=(n // TM,),
        in_specs=[pl.BlockSpec((TM, RWKV_WIDTH), tok)] * 4 + [
                  pl.BlockSpec((TM, GATE_LORA), tok),
                  pl.BlockSpec((TM, CONV_WIDTH), tok),
                  pl.BlockSpec((TM, 2 * D_MODEL), tok),
                  pl.BlockSpec((TM, D_MODEL), tok),
                  pl.BlockSpec((1, 1, 6 * D_MODEL), lambda i: (mod_row(i), 0, 0)),
                  full(g2w), full(lnxg), full(lnxb), full(wor), full(cw), full(cb), full(clg),
                  full(clb), full(woc), full(wo), full(n2g)],
        out_specs=[pl.BlockSpec((TM, D_MODEL), tok), pl.BlockSpec((TM, D_MODEL), tok)],
        out_shape=[jax.ShapeDtypeStruct((n, D_MODEL), F32),
                   jax.ShapeDtypeStruct((n, D_MODEL), BF16)],
        scratch_shapes=[pltpu.VMEM((SUBLANES, TM // seq, seq + 2 * CONV_PAD, CONV_WIDTH), F32)],
        compiler_params=pltpu.CompilerParams(dimension_semantics=("arbitrary",),
                                             vmem_limit_bytes=VMEM_LIMIT),
        name="mid",
    )(y_f, y_b, bn_f, bn_b, lg, glu, gates, x, mod3, g2w, lnxg, lnxb, wor, cw, cb, clg, clb, woc, wo, n2g)


def _mlp_kernel(x1_ref, xn2_ref, mod_ref, w1_ref, w2_ref, fg_ref, o_ref):
    g2 = mod_ref[0][:, 5 * D_MODEL:6 * D_MODEL]
    h = jnp.maximum(jnp.dot(xn2_ref[...], w1_ref[...], preferred_element_type=F32), 0.0)
    x2 = x1_ref[...] + g2 * _mm(h * h, w2_ref[...])
    o_ref[...] = _rms(x2) * fg_ref[...]


def _mlp(x1, xn2, mod3, mod_row, w1, w2, fg):
    n = x1.shape[0]
    const = lambda i: (0, 0)
    tok = lambda i: (i, 0)
    once = pl.Buffered(1)
    return pl.pallas_call(
        _mlp_kernel,
        grid=(n // TM_MLP,),
        in_specs=[pl.BlockSpec((TM_MLP, D_MODEL), tok),
                  pl.BlockSpec((TM_MLP, D_MODEL), tok),
                  pl.BlockSpec((1, 1, 6 * D_MODEL), lambda i: (mod_row(i * (TM_MLP // TM)), 0, 0)),
                  pl.BlockSpec(w1.shape, const, pipeline_mode=once),
                  pl.BlockSpec(w2.shape, const, pipeline_mode=once),
                  pl.BlockSpec((1, D_MODEL), const)],
        out_specs=pl.BlockSpec((TM_MLP, D_MODEL), tok),
        out_shape=jax.ShapeDtypeStruct((n, D_MODEL), F32),
        compiler_params=pltpu.CompilerParams(dimension_semantics=("arbitrary",),
                                             vmem_limit_bytes=VMEM_LIMIT),
        name="mlp",
    )(x1, xn2, mod3, w1, w2, fg)


def _pad_dir(w):
    return (jnp.eye(2, dtype=w.dtype)[:, :, None, None] * w[:, None]).reshape(2, 2 * w.shape[1], w.shape[2])


def kernel(x_prompt, x_sample, state_fwd, state_bwd, c, c_ctx, ada_w, ada_b, norm1_g, norm2_g, w_in,
           mu_prev, mu_next, decay_w0, decay_w1, decay_w2, iclr_a0, iclr_a1, iclr_a2, gate_g1, gate_g2,
           k_k, k_a, r_k, lnx_g, lnx_b, w_out_rwkv, conv_w, conv_b, conv_ln_g, conv_ln_b, w_out_conv,
           w_o, mlp_w1, mlp_w2, final_g):
    n_ctx, seq_ctx, _ = x_prompt.shape
    n_lat, seq_lat, _ = x_sample.shape
    depth = ada_w.shape[0]
    assert seq_ctx == TM and seq_lat % TM_MLP == 0 and GRID_W == CHUNK and c.shape[0] + 1 <= COND_ROWS
    lat_blk = seq_lat // TM
    row_ctx = lambda i: 0
    row_lat = lambda i: 1 + i // lat_blk

    xp = x_prompt.reshape(n_ctx * seq_ctx, D_MODEL)
    xs = x_sample.reshape(n_lat * seq_lat, D_MODEL)
    cond = jnp.concatenate([c_ctx[None, :], c, jnp.zeros((COND_ROWS - 1 - c.shape[0], D_MODEL), F32)], axis=0)
    zero_state = (jnp.zeros((1, N_HEADS, HEAD_DIM, HEAD_DIM), F32),) * 2
    row = lambda a: a.reshape(1, -1)
    assert depth == 1
    lay = lambda a: a.reshape(a.shape[1:])
    bf = lambda a: lay(a).astype(BF16)

    mod3 = _ada(cond, lay(ada_w), lay(ada_b)).reshape(COND_ROWS, 1, 6 * D_MODEL)
    w_lora = jnp.concatenate([decay_w1[0, 0], decay_w1[0, 1], iclr_a1[0, 0], iclr_a1[0, 1], gate_g1[0]],
                             axis=1).astype(BF16)
    conv_taps = jnp.broadcast_to(conv_w[0, :, None, :], (CONV_K, SUBLANES, CONV_WIDTH))
    front_w = (norm1_g, bf(w_in), w_lora)
    w2p = _pad_dir(lay(decay_w2)).astype(BF16)
    a2p = _pad_dir(lay(iclr_a2)).astype(BF16)
    wkv_w = [(mu_prev, mu_next, k_k, k_a, row(r_k), decay_w0[:, d], w2p[d], iclr_a0[:, d], a2p[d])
             for d in range(2)]
    mid_w = (bf(gate_g2), lnx_g, lnx_b, bf(w_out_rwkv), conv_taps, conv_b, conv_ln_g, conv_ln_b,
             bf(w_out_conv), bf(w_o), norm2_g)
    mlp_w = (bf(mlp_w1), bf(mlp_w2), row(final_g))
    s0_lat = (state_fwd[:, 0].astype(F32), state_bwd[:, 0].astype(F32))

    def layer(x, mod_row, s0, nseq, nblk, conv_seq):
        rkv, glu, gates, lora, lg = _front(x, mod3, mod_row, *front_w)
        y_f, bn_f, s_f = _wkv(rkv, lora, s0[0], *wkv_w[0], d=0, nseq=nseq, nblk=nblk)
        y_b, bn_b, s_b = _wkv(rkv, lora, s0[1], *wkv_w[1], d=1, nseq=nseq, nblk=nblk)
        x1, xn2 = _mid(y_f, y_b, bn_f, bn_b, lg, glu, gates, x, mod3, mod_row, *mid_w, seq=conv_seq)
        return _mlp(x1, xn2, mod3, mod_row, *mlp_w), (s_f, s_b)

    yp, s_ctx = layer(xp, row_ctx, zero_state, n_ctx, 1, seq_ctx)
    ys, _ = layer(xs, row_lat, s0_lat, n_lat, lat_blk, GRID_W)
    s_f, s_b = (s.astype(x_prompt.dtype)[:, None] for s in s_ctx)
    return (yp.reshape(x_prompt.shape), ys.reshape(x_sample.shape), s_f, s_b)
```

```python
import functools
import itertools

import jax
import jax.numpy as jnp
from jax import lax
from jax.experimental import pallas as pl
from jax.experimental.pallas import tpu as pltpu

D_MODEL = 1024
RWKV_WIDTH = 512
HEAD_DIM = 64
N_HEADS = RWKV_WIDTH // HEAD_DIM
CONV_WIDTH = 512
CONV_K = 31
D_FF = 4 * D_MODEL
GRID_W = 64
LORA_W = 64
GATE_LORA = 128
RMS_EPS = 1e-6
LN_EPS = 1e-5
GN_EPS = 64e-5

LANES = 128
SUBLANES = 8
TM = 256
TM_MLP = 512
COND_ROWS = 16
CHUNK = 64
N_CHUNK = TM // CHUNK
N_PAIR = RWKV_WIDTH // LANES
HALO = SUBLANES
CONV_PAD = 16
CONV_ROWS = 32
VMEM_LIMIT = 56 * 1024 * 1024

F32 = jnp.float32
BF16 = jnp.bfloat16
DECAY_SCALE = 0.6065306597126334


def _mm(a, b):
    return jnp.dot(a.astype(BF16), b.astype(BF16), preferred_element_type=F32)


def _mm_nt(a, b):
    return lax.dot_general(a.astype(BF16), b.astype(BF16), (((1,), (1,)), ((), ())),
                           preferred_element_type=F32)


def _split3(x):
    x1 = x.astype(BF16)
    r1 = x - x1.astype(F32)
    x2 = r1.astype(BF16)
    x3 = (r1 - x2.astype(F32)).astype(BF16)
    return x3, x2, x1


def _mm_split3(m, x):
    x3, x2, x1 = (jnp.dot(m, t, preferred_element_type=F32) for t in _split3(x))
    return x3 + x2 + x1


def _head_sum(x, bd):
    rows = x.shape[0]
    n_grp = x.shape[1] // LANES
    terms = []
    for p in range(n_grp):
        xp = x[:, p * LANES:(p + 1) * LANES]
        head = xp.astype(BF16)
        terms += [head, (xp - head.astype(F32)).astype(BF16)]
    prod = jnp.dot(jnp.concatenate(terms, axis=0), bd, preferred_element_type=F32)
    sums = [prod[(2 * p) * rows:(2 * p + 1) * rows] + prod[(2 * p + 1) * rows:(2 * p + 2) * rows]
            for p in range(n_grp)]
    return jnp.concatenate(sums, axis=1)


def _block_diag_ones():
    ri = lax.broadcasted_iota(jnp.int32, (LANES, LANES), 0)
    ci = lax.broadcasted_iota(jnp.int32, (LANES, LANES), 1)
    shift = HEAD_DIM.bit_length() - 1
    return jnp.where((ri >> shift) == (ci >> shift), 1.0, 0.0).astype(BF16)


def _rms(x):
    return x * lax.rsqrt(jnp.mean(x * x, axis=-1, keepdims=True) + RMS_EPS)


def _ada_kernel(c_ref, w_ref, b_ref, o_ref):
    c = c_ref[...]
    rows = c.shape[0]
    terms = jnp.concatenate(_split3(c * jax.nn.sigmoid(c)), axis=0)
    prod = jnp.dot(terms, w_ref[...].astype(BF16), preferred_element_type=F32)
    o_ref[...] = prod[0:rows] + prod[rows:2 * rows] + prod[2 * rows:] + b_ref[...]


def _ada(cond, w, b):
    n = w.shape[1]
    rows = cond.shape[0]
    tn = 1024
    return pl.pallas_call(
        _ada_kernel,
        grid=(n // tn,),
        in_specs=[pl.BlockSpec((rows, D_MODEL), lambda i: (0, 0)),
                  pl.BlockSpec((D_MODEL, tn), lambda i: (0, i)),
                  pl.BlockSpec((1, tn), lambda i: (0, i))],
        out_specs=pl.BlockSpec((rows, tn), lambda i: (0, i)),
        out_shape=jax.ShapeDtypeStruct((rows, n), F32),
        compiler_params=pltpu.CompilerParams(dimension_semantics=("arbitrary",),
                                             vmem_limit_bytes=VMEM_LIMIT),
        name="ada",
    )(cond, w, b.reshape(1, n))


_IN_SPLIT = (3 * RWKV_WIDTH, 2 * CONV_WIDTH, 2 * D_MODEL)
_LORA_SPLIT = (4 * LORA_W, GATE_LORA)


def _run(*streams):
    streams = list(streams)
    while streams:
        streams = [g for g in streams if next(g, True) is None]


def _conv_module(uu, cw_ref, cb_ref, clg_ref, clb_ref, pad_scr, seq):
    nseq = TM // seq
    zpad = jnp.zeros((nseq, CONV_PAD, CONV_WIDTH), F32)
    pad_scr[0, :, 0:CONV_PAD, :] = zpad
    pad_scr[0, :, CONV_PAD + seq:, :] = zpad
    for s in range(nseq):
        pad_scr[0, s, CONV_PAD:CONV_PAD + seq, :] = uu[s * seq:(s + 1) * seq]
    live = seq + 2 * CONV_PAD - SUBLANES
    for m in range(1, SUBLANES):
        pad_scr[m, :, 0:live, :] = pad_scr[0, :, m:m + live, :]
    first = CONV_PAD - CONV_K // 2
    groups = (CONV_ROWS // SUBLANES, SUBLANES, CONV_WIDTH)
    parts = []
    for s in range(nseq):
        for r0 in range(0, seq, CONV_ROWS):
            acc = jnp.broadcast_to(cb_ref[...], groups)
            for t in range(CONV_K):
                m = (first + t) % SUBLANES
                lo = r0 + first + t - m
                acc = acc + pad_scr[m, s, lo:lo + CONV_ROWS, :].reshape(groups) * cw_ref[t][None]
            parts.append(acc.reshape(CONV_ROWS, CONV_WIDTH))
    cv = jnp.concatenate(parts, axis=0)
    cm = cv - jnp.mean(cv, axis=-1, keepdims=True)
    cvar = jnp.mean(cm * cm, axis=-1, keepdims=True)
    un = cm * lax.rsqrt(cvar + LN_EPS) * clg_ref[...] + clb_ref[...]
    return un * jax.nn.sigmoid(un)


def _front_kernel(x_ref, mod_ref, g_ref, win_ref, wlo_ref, rkv_ref, glu_ref, gates_ref, lora_ref, lg_ref):
    mod = mod_ref[0]
    sh1 = mod[:, 0:D_MODEL]
    sc1 = mod[:, D_MODEL:2 * D_MODEL]
    xn = (_rms(x_ref[...]) * g_ref[...] * (1.0 + sc1) + sh1).astype(BF16)
    c0, c1, c2 = itertools.accumulate(_IN_SPLIT)
    dot = functools.partial(jnp.dot, preferred_element_type=F32)
    rkv_ref[...] = dot(xn, win_ref[:, 0:c0])
    glu = dot(xn, win_ref[:, c0:c1])
    glu_ref[...] = glu[:, 0:CONV_WIDTH] * jax.nn.sigmoid(glu[:, CONV_WIDTH:])
    gates_ref[...] = jax.nn.sigmoid(dot(xn, win_ref[:, c1:c2]))
    lora_ref[...] = dot(xn, wlo_ref[:, 0:_LORA_SPLIT[0]])
    lg_ref[...] = dot(xn, wlo_ref[:, _LORA_SPLIT[0]:])


def _front(x, mod3, mod_row, g, w_in, w_lora):
    n = x.shape[0]
    const = lambda i: (0, 0)
    tok = lambda i: (i, 0)
    widths = (_IN_SPLIT[0], CONV_WIDTH, _IN_SPLIT[2]) + _LORA_SPLIT
    return pl.pallas_call(
        _front_kernel,
        grid=(n // TM,),
        in_specs=[pl.BlockSpec((TM, D_MODEL), tok),
                  pl.BlockSpec((1, 1, 6 * D_MODEL), lambda i: (mod_row(i), 0, 0)),
                  pl.BlockSpec((1, D_MODEL), const),
                  pl.BlockSpec(w_in.shape, const),
                  pl.BlockSpec(w_lora.shape, const)],
        out_specs=[pl.BlockSpec((TM, w), tok) for w in widths],
        out_shape=[jax.ShapeDtypeStruct((n, w), F32) for w in widths],
        compiler_params=pltpu.CompilerParams(dimension_semantics=("arbitrary",),
                                             vmem_limit_bytes=VMEM_LIMIT),
        name="front",
    )(x, mod3, g, w_in, w_lora)


def _wkv_masks(d):
    shift = CHUNK.bit_length() - 1
    ri = lax.broadcasted_iota(jnp.int32, (LANES, LANES), 0)
    ci = lax.broadcasted_iota(jnp.int32, (LANES, LANES), 1)
    same = (ri >> shift) == (ci >> shift)
    rel = (ci - ri) * (1 - 2 * d)
    levels = [((ri >> (lv + 1)) == (ci >> (lv + 1))) & (((ri >> lv) & 1) == 1 - d) & (((ci >> lv) & 1) == d)
              for lv in range(shift)]
    rc = lax.broadcasted_iota(jnp.int32, (LANES // 2, LANES), 0)
    cc = lax.broadcasted_iota(jnp.int32, (LANES // 2, LANES), 1)
    compact = [((cc >> (lv + 1)) == (rc >> lv)) & (((cc >> lv) & 1) == d) for lv in range(shift)]
    return same & (rel < 0), same & (rel <= 0), levels, compact


def _wkv_kernel(zero_ref, rkv_ref, prev_ref, next_ref, lora_ref, s0_ref, mup_ref, mun_ref, kk_ref, ka_ref,
                rk_ref, w0_ref, w2_ref, a0_ref, a2_ref,
                y_ref, bonus_ref, sout_ref,
                s_scr, r_scr, kd_scr, v_scr, aa_scr, b_scr, lw_scr, cl_scr, bn_scr,
                rhs_scr, out_scr, upd_scr, ti_scr, vs_scr, wc_scr, aab_scr, t_scr, *, d, nseq, nblk):
    g = pl.program_id(0)
    n_tile = nseq * nblk
    n_prob = N_CHUNK * N_PAIR
    j_a = jnp.minimum(g, n_tile - 1) % nblk
    j_b = jnp.maximum(g - 1, 0) % nblk
    masks = _wkv_masks(d)
    row0 = zero_ref[0]
    shift = CHUNK.bit_length() - 1
    ri = lax.broadcasted_iota(jnp.int32, (LANES, LANES), 0)
    ci = lax.broadcasted_iota(jnp.int32, (LANES, LANES), 1)
    eye = (ri == ci).astype(F32)
    head0 = lax.broadcasted_iota(jnp.int32, (CHUNK, LANES), 1) < HEAD_DIM
    dot = functools.partial(jnp.dot, preferred_element_type=F32)

    def stack(t):
        return jnp.concatenate([jnp.where(head0, t, 0.0), jnp.where(head0, 0.0, t)], axis=0)

    def twice(t):
        return jnp.concatenate([t, t], axis=0)

    def chunk_rows(step, anchored=False):
        lo = (N_CHUNK - 1 - step if d else step) * CHUNK
        return pl.ds(pl.multiple_of(row0 + lo, CHUNK), CHUNK) if anchored else slice(lo, lo + CHUNK)

    def prepare(slot, j):
        blk = nblk - 1 - j if d else j
        u = rkv_ref[...]
        row = lax.broadcasted_iota(jnp.int32, (TM, 1), 0)
        prow = jnp.where(blk > 0, prev_ref[HALO - 1:HALO, :], 0.0)
        nrow = jnp.where(blk < nblk - 1, next_ref[0:1, :], 0.0)
        prev = jnp.where(row == 0, prow, pltpu.roll(u, 1, axis=0))
        nxt = jnp.where(row == TM - 1, nrow, pltpu.roll(u, TM - 1, axis=0))
        x = u + (prev - u) * mup_ref[...] + (nxt - u) * mun_ref[...]
        r = x[:, 0:RWKV_WIDTH]
        k = x[:, RWKV_WIDTH:2 * RWKV_WIDTH]
        v = x[:, 2 * RWKV_WIDTH:]
        r_scr[slot] = r
        v_scr[slot] = v
        yield
        bd = _block_diag_ones()
        kkr = k * kk_ref[...]
        kk = kkr * lax.rsqrt(jnp.maximum(_head_sum(kkr * kkr, bd), 1e-24))
        aa_scr[slot] = -kk
        yield
        lo = lora_ref[...]
        z = w0_ref[...] + _mm(jnp.tanh(lo[:, 0:2 * LORA_W]), w2_ref[...])
        lw = -DECAY_SCALE * jax.nn.sigmoid(z)
        lw_scr[slot, 0:TM, :] = lw
        a = jax.nn.sigmoid(a0_ref[...] + _mm(lo[:, 2 * LORA_W:4 * LORA_W], a2_ref[...]))
        kd = k * (1.0 + (a - 1.0) * ka_ref[...])
        kd_scr[slot] = kd
        b_scr[slot] = kk * a
        yield
        bn_scr[slot] = _head_sum(r * kd * rk_ref[...], bd) * v
        yield
        rt = lax.broadcasted_iota(jnp.int32, (TM, TM), 0)
        ct = lax.broadcasted_iota(jnp.int32, (TM, TM), 1)
        cum_m = jnp.where(((rt >> shift) == (ct >> shift)) & (((ct - rt) * (1 - 2 * d)) <= 0), 1.0, 0.0)
        cl_scr[slot] = _mm_split3(cum_m.astype(BF16), lw)
        yield

    def operands(step, slot):
        strict, incl, levels, _ = masks
        rows = chunk_rows(step, anchored=True)
        for p in range(N_PAIR):
            i = slot * n_prob + step * N_PAIR + p
            sl = slice(p * LANES, (p + 1) * LANES)
            lw = lw_scr[slot, rows, sl]
            cl = cl_scr[slot, rows, sl]
            tot = jnp.sum(lw, axis=0, keepdims=True)
            e_out = jnp.exp(-cl)
            e_end = jnp.exp(tot - cl)
            b_c = b_scr[slot, rows, sl]
            kd_c = kd_scr[slot, rows, sl]
            a_s = stack(aa_scr[slot, rows, sl] * jnp.exp(cl - lw)).astype(BF16)
            r_s = stack(r_scr[slot, rows, sl] * jnp.exp(cl)).astype(BF16)
            bk_s = jnp.concatenate([twice((b_c * e_out).astype(BF16)),
                                    twice((kd_c * e_out).astype(BF16))], axis=0)
            vs_scr[i] = stack(v_scr[slot, rows, sl]).astype(BF16)
            upd_scr[i] = jnp.concatenate([jnp.transpose(stack(b_c * e_end)),
                                          jnp.transpose(stack(kd_c * e_end))], axis=1).astype(BF16)
            wc_scr[i] = jnp.transpose(jnp.broadcast_to(jnp.exp(tot), (LANES, LANES)))
            ab = _mm_nt(jnp.concatenate([a_s, r_s], axis=0), bk_s)
            a_ab = jnp.where(strict, ab[0:LANES, 0:LANES], 0.0)
            a_ak = jnp.where(strict, ab[0:LANES, LANES:], 0.0)
            a_rb = jnp.where(incl, ab[LANES:, 0:LANES], 0.0)
            a_rk = jnp.where(incl, ab[LANES:, LANES:], 0.0)
            rhs_scr[i] = jnp.concatenate([a_s, a_ak.astype(BF16)], axis=1)
            out_scr[i] = jnp.concatenate([r_s, a_rb.astype(BF16), a_rk.astype(BF16)], axis=1)
            aab_scr[i] = a_ab
            t_scr[i] = eye + jnp.where(levels[0], a_ab, 0.0)
            yield

    def invert(steps, slot):
        _, _, levels, compact = masks
        idx = [slot * n_prob + step * N_PAIR + p for step in steps for p in range(N_PAIR)]
        for lv in range(1, len(levels)):
            s = 1 << lv
            gather = s >= SUBLANES
            pieces = [slice((2 * q + 1 - d) * s, (2 * q + 2 - d) * s) for q in range(LANES // (2 * s))]

            def later(x):
                return jnp.concatenate([x[rows] for rows in pieces], axis=0)

            def spread(x):
                zero = jnp.zeros((s, LANES), x.dtype)
                halves = [[x[q * s:(q + 1) * s], zero] if d else [zero, x[q * s:(q + 1) * s]]
                          for q in range(len(pieces))]
                return jnp.concatenate([h for pair in halves for h in pair], axis=0)

            ts = [t_scr[i] for i in idx]
            tbs = [t.astype(BF16) for t in ts]
            if gather:
                ms = [dot(jnp.where(compact[lv], later(aab_scr[i]), 0.0).astype(BF16), tb)
                      for i, tb in zip(idx, tbs)]
            else:
                ms = [dot(jnp.where(levels[lv], aab_scr[i], 0.0).astype(BF16), tb) for i, tb in zip(idx, tbs)]
            lw_scr[slot, TM:, 0:LANES] = ms[-1][0:SUBLANES, :]
            yield
            for i, t, tb, m in zip(idx, ts, tbs, ms):
                if gather:
                    z = dot(later(t).astype(BF16), spread(m.astype(BF16)))
                    for q, rows in enumerate(pieces):
                        t_scr[i, rows, :] = t[rows] + z[q * s:(q + 1) * s]
                else:
                    t_scr[i] = t + dot(tb, m.astype(BF16))
            lw_scr[slot, TM:, 0:LANES] = ms[0][0:SUBLANES, :]
            yield
        for i in idx:
            ti_scr[i] = t_scr[i].astype(BF16)

    def scan(step, slot):
        rows = chunk_rows(step)
        base = slot * n_prob + step * N_PAIR
        pairs = range(N_PAIR)
        sts = [s_scr[p] for p in pairs]
        stb = [st.astype(BF16) for st in sts]
        vss = [vs_scr[base + p] for p in pairs]
        rhs = [dot(rhs_scr[base + p], jnp.concatenate([stb[p], vss[p]], axis=0)) for p in pairs]
        yield
        pmb = []
        for p in pairs:
            rh = rhs[p].astype(BF16)
            rl = (rhs[p] - rh.astype(F32)).astype(BF16)
            pm = dot(ti_scr[base + p], jnp.concatenate([rh, rl], axis=1))
            pmb.append((pm[:, 0:LANES] + pm[:, LANES:]).astype(BF16))
        yield
        for p in pairs:
            s_scr[p] = wc_scr[base + p] * sts[p] + dot(
                upd_scr[base + p], jnp.concatenate([pmb[p], vss[p]], axis=0))
        yield
        for p in pairs:
            y_s = dot(out_scr[base + p], jnp.concatenate([stb[p], pmb[p], vss[p]], axis=0))
            y_ref[rows, p * LANES:(p + 1) * LANES] = y_s[0:CHUNK] + y_s[CHUNK:]
        yield

    first, second = range(0, N_CHUNK // 2), range(N_CHUNK // 2, N_CHUNK)
    chain = itertools.chain.from_iterable

    def stage_a(slot):
        return prepare(slot, j_a), chain(operands(s, slot) for s in first)

    def stage_b(slot, with_inverse=(), with_scan=()):
        @pl.when(j_b == 0)
        def _():
            zero = jnp.zeros((HEAD_DIM, HEAD_DIM), F32)
            for p in range(N_PAIR):
                top = jnp.concatenate([s0_ref[0, 2 * p], zero], axis=1)
                bot = jnp.concatenate([zero, s0_ref[0, 2 * p + 1]], axis=1)
                s_scr[p] = jnp.transpose(jnp.concatenate([top, bot], axis=0))

        bonus_ref[...] = bn_scr[slot]
        _run(invert(first, slot), chain(operands(s, slot) for s in second))
        _run(invert(second, slot), chain(scan(s, slot) for s in first), *with_inverse)
        _run(chain(scan(s, slot) for s in second), *with_scan)

        @pl.when(j_b == nblk - 1)
        def _():
            for p in range(N_PAIR):
                s_vk = jnp.transpose(s_scr[p])
                sout_ref[0, 2 * p] = s_vk[0:HEAD_DIM, 0:HEAD_DIM]
                sout_ref[0, 2 * p + 1] = s_vk[HEAD_DIM:, HEAD_DIM:]

    @pl.when(g == 0)
    def _():
        prep, ops = stage_a(0)
        _run(prep)
        _run(ops)

    for parity in range(2):
        @pl.when((g > 0) & (g < n_tile) & (g % 2 == parity))
        def _():
            prep, ops = stage_a(parity)
            stage_b(1 - parity, with_inverse=(prep,), with_scan=(ops,))

    @pl.when(g == n_tile)
    def _():
        stage_b((n_tile - 1) % 2)


def _wkv(rkv, lora, s0, mup, mun, k_k, k_a, r_k, w0, w2p, a0, a2p, *, d, nseq, nblk):
    n = rkv.shape[0]
    hb = TM // HALO
    n_tile = nseq * nblk

    def tile(t):
        s = t // nblk
        j = t - s * nblk
        return s, s * nblk + (nblk - 1 - j if d else j)

    ahead = lambda g: tile(jnp.minimum(g, n_tile - 1))
    behind = lambda g: tile(jnp.maximum(g - 1, 0))
    s0_seq = (lambda s: s) if s0.shape[0] > 1 else (lambda s: 0)
    const = lambda g: (0, 0)
    out_tile = lambda g: (behind(g)[1], 0)
    kern = functools.partial(_wkv_kernel, d=d, nseq=nseq, nblk=nblk)
    tok = pltpu.VMEM((2, TM, RWKV_WIDTH), F32)
    n_prob = 2 * N_CHUNK * N_PAIR
    return pl.pallas_call(
        kern,
        grid=(n_tile + 1,),
        in_specs=[
            pl.BlockSpec(memory_space=pltpu.SMEM),
            pl.BlockSpec((TM, 3 * RWKV_WIDTH), lambda g: (ahead(g)[1], 0)),
            pl.BlockSpec((HALO, 3 * RWKV_WIDTH), lambda g: (jnp.maximum(ahead(g)[1] * hb - 1, 0), 0)),
            pl.BlockSpec((HALO, 3 * RWKV_WIDTH),
                         lambda g: (jnp.minimum((ahead(g)[1] + 1) * hb, n // HALO - 1), 0)),
            pl.BlockSpec((TM, 4 * LORA_W), lambda g: (ahead(g)[1], 0)),
            pl.BlockSpec((1, N_HEADS, HEAD_DIM, HEAD_DIM), lambda g: (s0_seq(behind(g)[0]), 0, 0, 0)),
            pl.BlockSpec((1, 3 * RWKV_WIDTH), const),
            pl.BlockSpec((1, 3 * RWKV_WIDTH), const),
            pl.BlockSpec((1, RWKV_WIDTH), const),
            pl.BlockSpec((1, RWKV_WIDTH), const),
            pl.BlockSpec((1, RWKV_WIDTH), const),
            pl.BlockSpec((1, RWKV_WIDTH), const),
            pl.BlockSpec((2 * LORA_W, RWKV_WIDTH), const),
            pl.BlockSpec((1, RWKV_WIDTH), const),
            pl.BlockSpec((2 * LORA_W, RWKV_WIDTH), const),
        ],
        out_specs=[
            pl.BlockSpec((TM, RWKV_WIDTH), out_tile),
            pl.BlockSpec((TM, RWKV_WIDTH), out_tile),
            pl.BlockSpec((1, N_HEADS, HEAD_DIM, HEAD_DIM), lambda g: (behind(g)[0], 0, 0, 0)),
        ],
        out_shape=[jax.ShapeDtypeStruct((n, RWKV_WIDTH), F32),
                   jax.ShapeDtypeStruct((n, RWKV_WIDTH), F32),
                   jax.ShapeDtypeStruct((nseq, N_HEADS, HEAD_DIM, HEAD_DIM), F32)],
        scratch_shapes=[pltpu.VMEM((N_PAIR, LANES, LANES), F32),
                        tok, tok, tok, tok, tok,
                        pltpu.VMEM((2, TM + SUBLANES, RWKV_WIDTH), F32),
                        tok, tok,
                        pltpu.VMEM((n_prob, LANES, 2 * LANES), BF16),
                        pltpu.VMEM((n_prob, LANES, 3 * LANES), BF16),
                        pltpu.VMEM((n_prob, LANES, 2 * LANES), BF16),
                        pltpu.VMEM((n_prob, LANES, LANES), BF16),
                        pltpu.VMEM((n_prob, LANES, LANES), BF16),
                        pltpu.VMEM((n_prob, LANES, LANES), F32),
                        pltpu.VMEM((n_prob, LANES, LANES), F32),
                        pltpu.VMEM((n_prob, LANES, LANES), F32)],
        compiler_params=pltpu.CompilerParams(dimension_semantics=("arbitrary",),
                                             vmem_limit_bytes=VMEM_LIMIT),
        name="wkv",
    )(jnp.zeros((1,), jnp.int32), rkv, rkv, rkv, lora, s0, mup, mun, k_k, k_a, r_k, w0, w2p, a0, a2p)


def _mid_kernel(yf_ref, yb_ref, bf_ref, bb_ref, lg_ref, glu_ref, gates_ref, x_ref, mod_ref, g2w_ref, lnxg_ref,
                lnxb_ref, wor_ref, cw_ref, cb_ref, clg_ref, clb_ref, woc_ref, wo_ref, n2g_ref,
                x1_ref, xn2_ref, pad_scr, *, seq):
    mod = mod_ref[0]
    g1 = mod[:, 2 * D_MODEL:3 * D_MODEL]
    sh2 = mod[:, 3 * D_MODEL:4 * D_MODEL]
    sc2 = mod[:, 4 * D_MODEL:5 * D_MODEL]

    bd = _block_diag_ones()
    y = yf_ref[...] + yb_ref[...]
    yc = y - _head_sum(y, bd) * (1.0 / HEAD_DIM)
    var = _head_sum(yc * yc, bd) * (1.0 / HEAD_DIM)
    yn = yc * lax.rsqrt(var + GN_EPS) * lnxg_ref[...] + lnxb_ref[...]
    g = _mm(jax.nn.sigmoid(lg_ref[...]), g2w_ref[...])
    y_r = _mm((yn + bf_ref[...] + bb_ref[...]) * g, wor_ref[...])

    y_c = _mm(_conv_module(glu_ref[...], cw_ref, cb_ref, clg_ref, clb_ref, pad_scr, seq), woc_ref[...])

    gates = gates_ref[...]
    merged = gates[:, 0:D_MODEL] * y_r + gates[:, D_MODEL:] * y_c
    x1 = x_ref[...] + g1 * _mm(merged, wo_ref[...])
    x1_ref[...] = x1
    xn2_ref[...] = (_rms(x1) * n2g_ref[...] * (1.0 + sc2) + sh2).astype(BF16)


def _mid(y_f, y_b, bn_f, bn_b, lg, glu, gates, x, mod3, mod_row, g2w, lnxg, lnxb, wor, cw, cb, clg, clb, woc,
         wo, n2g, *, seq):
    n = x.shape[0]
    const = lambda i: (0, 0)
    tok = lambda i: (i, 0)

    def full(a):
        return pl.BlockSpec(a.shape, lambda i: (0,) * a.ndim)

    return pl.pallas_call(
        functools.partial(_mid_kernel, seq=seq),
        grid=(n // TM,),
        in_specs=[pl.BlockSpec((TM, RWKV_WIDTH), tok)] * 4 + [
                  pl.BlockSpec((TM, GATE_LORA), tok),
                  pl.BlockSpec((TM, CONV_WIDTH), tok),
                  pl.BlockSpec((TM, 2 * D_MODEL), tok),
                  pl.BlockSpec((TM, D_MODEL), tok),
                  pl.BlockSpec((1, 1, 6 * D_MODEL), lambda i: (mod_row(i), 0, 0)),
                  full(g2w), full(lnxg), full(lnxb), full(wor), full(cw), full(cb), full(clg),
                  full(clb), full(woc), full(wo), full(n2g)],
        out_specs=[pl.BlockSpec((TM, D_MODEL), tok), pl.BlockSpec((TM, D_MODEL), tok)],
        out_shape=[jax.ShapeDtypeStruct((n, D_MODEL), F32),
                   jax.ShapeDtypeStruct((n, D_MODEL), BF16)],
        scratch_shapes=[pltpu.VMEM((SUBLANES, TM // seq, seq + 2 * CONV_PAD, CONV_WIDTH), F32)],
        compiler_params=pltpu.CompilerParams(dimension_semantics=("arbitrary",),
                                             vmem_limit_bytes=VMEM_LIMIT),
        name="mid",
    )(y_f, y_b, bn_f, bn_b, lg, glu, gates, x, mod3, g2w, lnxg, lnxb, wor, cw, cb, clg, clb, woc, wo, n2g)


def _mlp_kernel(x1_ref, xn2_ref, mod_ref, w1_ref, w2_ref, fg_ref, o_ref):
    g2 = mod_ref[0][:, 5 * D_MODEL:6 * D_MODEL]
    h = jnp.maximum(jnp.dot(xn2_ref[...], w1_ref[...], preferred_element_type=F32), 0.0)
    x2 = x1_ref[...] + g2 * _mm(h * h, w2_ref[...])
    o_ref[...] = _rms(x2) * fg_ref[...]


def _mlp(x1, xn2, mod3, mod_row, w1, w2, fg):
    n = x1.shape[0]
    const = lambda i: (0, 0)
    tok = lambda i: (i, 0)
    once = pl.Buffered(1)
    return pl.pallas_call(
        _mlp_kernel,
        grid=(n // TM_MLP,),
        in_specs=[pl.BlockSpec((TM_MLP, D_MODEL), tok),
                  pl.BlockSpec((TM_MLP, D_MODEL), tok),
                  pl.BlockSpec((1, 1, 6 * D_MODEL), lambda i: (mod_row(i * (TM_MLP // TM)), 0, 0)),
                  pl.BlockSpec(w1.shape, const, pipeline_mode=once),
                  pl.BlockSpec(w2.shape, const, pipeline_mode=once),
                  pl.BlockSpec((1, D_MODEL), const)],
        out_specs=pl.BlockSpec((TM_MLP, D_MODEL), tok),
        out_shape=jax.ShapeDtypeStruct((n, D_MODEL), F32),
        compiler_params=pltpu.CompilerParams(dimension_semantics=("arbitrary",),
                                             vmem_limit_bytes=VMEM_LIMIT),
        name="mlp",
    )(x1, xn2, mod3, w1, w2, fg)


def _pad_dir(w):
    return (jnp.eye(2, dtype=w.dtype)[:, :, None, None] * w[:, None]).reshape(2, 2 * w.shape[1], w.shape[2])


def kernel(x_prompt, x_sample, state_fwd, state_bwd, c, c_ctx, ada_w, ada_b, norm1_g, norm2_g, w_in,
           mu_prev, mu_next, decay_w0, decay_w1, decay_w2, iclr_a0, iclr_a1, iclr_a2, gate_g1, gate_g2,
           k_k, k_a, r_k, lnx_g, lnx_b, w_out_rwkv, conv_w, conv_b, conv_ln_g, conv_ln_b, w_out_conv,
           w_o, mlp_w1, mlp_w2, final_g):
    n_ctx, seq_ctx, _ = x_prompt.shape
    n_lat, seq_lat, _ = x_sample.shape
    depth = ada_w.shape[0]
    assert seq_ctx == TM and seq_lat % TM_MLP == 0 and GRID_W == CHUNK and c.shape[0] + 1 <= COND_ROWS
    lat_blk = seq_lat // TM
    row_ctx = lambda i: 0
    row_lat = lambda i: 1 + i // lat_blk

    xp = x_prompt.reshape(n_ctx * seq_ctx, D_MODEL)
    xs = x_sample.reshape(n_lat * seq_lat, D_MODEL)
    cond = jnp.concatenate([c_ctx[None, :], c, jnp.zeros((COND_ROWS - 1 - c.shape[0], D_MODEL), F32)], axis=0)
    zero_state = (jnp.zeros((1, N_HEADS, HEAD_DIM, HEAD_DIM), F32),) * 2
    row = lambda a: a.reshape(1, -1)
    assert depth == 1
    lay = lambda a: a.reshape(a.shape[1:])
    bf = lambda a: lay(a).astype(BF16)

    mod3 = _ada(cond, lay(ada_w), lay(ada_b)).reshape(COND_ROWS, 1, 6 * D_MODEL)
    w_lora = jnp.concatenate([decay_w1[0, 0], decay_w1[0, 1], iclr_a1[0, 0], iclr_a1[0, 1], gate_g1[0]],
                             axis=1).astype(BF16)
    conv_taps = jnp.broadcast_to(conv_w[0, :, None, :], (CONV_K, SUBLANES, CONV_WIDTH))
    front_w = (norm1_g, bf(w_in), w_lora)
    w2p = _pad_dir(lay(decay_w2)).astype(BF16)
    a2p = _pad_dir(lay(iclr_a2)).astype(BF16)
    wkv_w = [(mu_prev, mu_next, k_k, k_a, row(r_k), decay_w0[:, d], w2p[d], iclr_a0[:, d], a2p[d])
             for d in range(2)]
    mid_w = (bf(gate_g2), lnx_g, lnx_b, bf(w_out_rwkv), conv_taps, conv_b, conv_ln_g, conv_ln_b,
             bf(w_out_conv), bf(w_o), norm2_g)
    mlp_w = (bf(mlp_w1), bf(mlp_w2), row(final_g))
    s0_lat = (state_fwd[:, 0].astype(F32), state_bwd[:, 0].astype(F32))

    def layer(x, mod_row, s0, nseq, nblk, conv_seq):
        rkv, glu, gates, lora, lg = _front(x, mod3, mod_row, *front_w)
        y_f, bn_f, s_f = _wkv(rkv, lora, s0[0], *wkv_w[0], d=0, nseq=nseq, nblk=nblk)
        y_b, bn_b, s_b = _wkv(rkv, lora, s0[1], *wkv_w[1], d=1, nseq=nseq, nblk=nblk)
        x1, xn2 = _mid(y_f, y_b, bn_f, bn_b, lg, glu, gates, x, mod3, mod_row, *mid_w, seq=conv_seq)
        return _mlp(x1, xn2, mod3, mod_row, *mlp_w), (s_f, s_b)

    yp, s_ctx = layer(xp, row_ctx, zero_state, n_ctx, 1, seq_ctx)
    ys, _ = layer(xs, row_lat, s0_lat, n_lat, lat_blk, GRID_W)
    s_f, s_b = (s.astype(x_prompt.dtype)[:, None] for s in s_ctx)
    return (yp.reshape(x_prompt.shape), ys.reshape(x_sample.shape), s_f, s_b)
```

```python
import functools
import itertools

import jax
import jax.numpy as jnp
from jax import lax
from jax.experimental import pallas as pl
from jax.experimental.pallas import tpu as pltpu

D_MODEL = 1024
RWKV_WIDTH = 512
HEAD_DIM = 64
N_HEADS = RWKV_WIDTH // HEAD_DIM
CONV_WIDTH = 512
CONV_K = 31
D_FF = 4 * D_MODEL
GRID_W = 64
LORA_W = 64
GATE_LORA = 128
RMS_EPS = 1e-6
LN_EPS = 1e-5
GN_EPS = 64e-5

LANES = 128
SUBLANES = 8
TM = 256
TM_MLP = 512
COND_ROWS = 16
ADA_COLS = 1024
CHUNK = 64
N_CHUNK = TM // CHUNK
N_PAIR = RWKV_WIDTH // LANES
HALO = SUBLANES
CONV_PAD = 16
CONV_ROWS = 32
VMEM_LIMIT = 56 * 1024 * 1024

F32 = jnp.float32
BF16 = jnp.bfloat16
DECAY_SCALE = 0.6065306597126334


def _mm(a, b):
    return jnp.dot(a.astype(BF16), b.astype(BF16), preferred_element_type=F32)


def _mm_nt(a, b):
    return lax.dot_general(a.astype(BF16), b.astype(BF16), (((1,), (1,)), ((), ())),
                           preferred_element_type=F32)


def _split3(x):
    x1 = x.astype(BF16)
    r1 = x - x1.astype(F32)
    x2 = r1.astype(BF16)
    x3 = (r1 - x2.astype(F32)).astype(BF16)
    return x3, x2, x1


def _mm_split3(m, x):
    x3, x2, x1 = (jnp.dot(m, t, preferred_element_type=F32) for t in _split3(x))
    return x3 + x2 + x1


def _head_sum(x, bd):
    rows = x.shape[0]
    n_grp = x.shape[1] // LANES
    terms = []
    for p in range(n_grp):
        xp = x[:, p * LANES:(p + 1) * LANES]
        head = xp.astype(BF16)
        terms += [head, (xp - head.astype(F32)).astype(BF16)]
    prod = jnp.dot(jnp.concatenate(terms, axis=0), bd, preferred_element_type=F32)
    sums = [prod[(2 * p) * rows:(2 * p + 1) * rows] + prod[(2 * p + 1) * rows:(2 * p + 2) * rows]
            for p in range(n_grp)]
    return jnp.concatenate(sums, axis=1)


def _block_diag_ones():
    ri = lax.broadcasted_iota(jnp.int32, (LANES, LANES), 0)
    ci = lax.broadcasted_iota(jnp.int32, (LANES, LANES), 1)
    shift = HEAD_DIM.bit_length() - 1
    return jnp.where((ri >> shift) == (ci >> shift), 1.0, 0.0).astype(BF16)


def _rms(x):
    return x * lax.rsqrt(jnp.mean(x * x, axis=-1, keepdims=True) + RMS_EPS)


def _ada_kernel(c_ref, w_ref, b_ref, o_ref):
    c = c_ref[...]
    rows = c.shape[0]
    terms = jnp.concatenate(_split3(c * jax.nn.sigmoid(c)), axis=0)
    prod = jnp.dot(terms, w_ref[...].astype(BF16), preferred_element_type=F32)
    o_ref[...] = prod[0:rows] + prod[rows:2 * rows] + prod[2 * rows:] + b_ref[...]


def _ada(cond, w, b):
    n = w.shape[1]
    rows = cond.shape[0]
    tn = ADA_COLS
    return pl.pallas_call(
        _ada_kernel,
        grid=(n // tn,),
        in_specs=[pl.BlockSpec((rows, D_MODEL), lambda i: (0, 0)),
                  pl.BlockSpec((D_MODEL, tn), lambda i: (0, i)),
                  pl.BlockSpec((1, tn), lambda i: (0, i))],
        out_specs=pl.BlockSpec((rows, tn), lambda i: (0, i)),
        out_shape=jax.ShapeDtypeStruct((rows, n), F32),
        compiler_params=pltpu.CompilerParams(dimension_semantics=("arbitrary",),
                                             vmem_limit_bytes=VMEM_LIMIT),
        name="ada",
    )(cond, w, b.reshape(1, n))


_IN_SPLIT = (3 * RWKV_WIDTH, 2 * CONV_WIDTH, 2 * D_MODEL)
_LORA_SPLIT = (4 * LORA_W, GATE_LORA)


def _run(*streams):
    streams = list(streams)
    while streams:
        streams = [g for g in streams if next(g, True) is None]


def _conv_module(uu, cw_ref, cb_ref, clg_ref, clb_ref, pad_scr, seq):
    nseq = TM // seq
    zpad = jnp.zeros((nseq, CONV_PAD, CONV_WIDTH), F32)
    pad_scr[0, :, 0:CONV_PAD, :] = zpad
    pad_scr[0, :, CONV_PAD + seq:, :] = zpad
    for s in range(nseq):
        pad_scr[0, s, CONV_PAD:CONV_PAD + seq, :] = uu[s * seq:(s + 1) * seq]
    live = seq + 2 * CONV_PAD - SUBLANES
    for m in range(1, SUBLANES):
        pad_scr[m, :, 0:live, :] = pad_scr[0, :, m:m + live, :]
    first = CONV_PAD - CONV_K // 2
    groups = (CONV_ROWS // SUBLANES, SUBLANES, CONV_WIDTH)
    parts = []
    for s in range(nseq):
        for r0 in range(0, seq, CONV_ROWS):
            acc = jnp.broadcast_to(cb_ref[...], groups)
            for t in range(CONV_K):
                m = (first + t) % SUBLANES
                lo = r0 + first + t - m
                acc = acc + pad_scr[m, s, lo:lo + CONV_ROWS, :].reshape(groups) * cw_ref[t][None]
            parts.append(acc.reshape(CONV_ROWS, CONV_WIDTH))
    cv = jnp.concatenate(parts, axis=0)
    cm = cv - jnp.mean(cv, axis=-1, keepdims=True)
    cvar = jnp.mean(cm * cm, axis=-1, keepdims=True)
    un = cm * lax.rsqrt(cvar + LN_EPS) * clg_ref[...] + clb_ref[...]
    return un * jax.nn.sigmoid(un)


def _front_kernel(x_ref, mod_ref, g_ref, win_ref, wlo_ref, rkv_ref, glu_ref, gates_ref, lora_ref, lg_ref):
    mod = mod_ref[0]
    sh1 = mod[:, 0:D_MODEL]
    sc1 = mod[:, D_MODEL:2 * D_MODEL]
    xn = (_rms(x_ref[...]) * g_ref[...] * (1.0 + sc1) + sh1).astype(BF16)
    c0, c1, c2 = itertools.accumulate(_IN_SPLIT)
    dot = functools.partial(jnp.dot, preferred_element_type=F32)
    rkv_ref[...] = dot(xn, win_ref[:, 0:c0])
    glu = dot(xn, win_ref[:, c0:c1])
    glu_ref[...] = glu[:, 0:CONV_WIDTH] * jax.nn.sigmoid(glu[:, CONV_WIDTH:])
    gates_ref[...] = jax.nn.sigmoid(dot(xn, win_ref[:, c1:c2]))
    lora_ref[...] = dot(xn, wlo_ref[:, 0:_LORA_SPLIT[0]])
    lg_ref[...] = dot(xn, wlo_ref[:, _LORA_SPLIT[0]:])


def _front(x, mod3, mod_row, g, w_in, w_lora):
    n = x.shape[0]
    const = lambda i: (0, 0)
    tok = lambda i: (i, 0)
    widths = (_IN_SPLIT[0], CONV_WIDTH, _IN_SPLIT[2]) + _LORA_SPLIT
    return pl.pallas_call(
        _front_kernel,
        grid=(n // TM,),
        in_specs=[pl.BlockSpec((TM, D_MODEL), tok),
                  pl.BlockSpec((1, 1, 6 * D_MODEL), lambda i: (mod_row(i), 0, 0)),
                  pl.BlockSpec((1, D_MODEL), const),
                  pl.BlockSpec(w_in.shape, const),
                  pl.BlockSpec(w_lora.shape, const)],
        out_specs=[pl.BlockSpec((TM, w), tok) for w in widths],
        out_shape=[jax.ShapeDtypeStruct((n, w), F32) for w in widths],
        compiler_params=pltpu.CompilerParams(dimension_semantics=("arbitrary",),
                                             vmem_limit_bytes=VMEM_LIMIT),
        name="front",
    )(x, mod3, g, w_in, w_lora)


def _wkv_masks(d):
    shift = CHUNK.bit_length() - 1
    ri = lax.broadcasted_iota(jnp.int32, (LANES, LANES), 0)
    ci = lax.broadcasted_iota(jnp.int32, (LANES, LANES), 1)
    same = (ri >> shift) == (ci >> shift)
    rel = (ci - ri) * (1 - 2 * d)
    levels = [((ri >> (lv + 1)) == (ci >> (lv + 1))) & (((ri >> lv) & 1) == 1 - d) & (((ci >> lv) & 1) == d)
              for lv in range(shift)]
    rc = lax.broadcasted_iota(jnp.int32, (LANES // 2, LANES), 0)
    cc = lax.broadcasted_iota(jnp.int32, (LANES // 2, LANES), 1)
    compact = [((cc >> (lv + 1)) == (rc >> lv)) & (((cc >> lv) & 1) == d) for lv in range(shift)]
    return same & (rel < 0), same & (rel <= 0), levels, compact


def _wkv_kernel(zero_ref, rkv_ref, prev_ref, next_ref, lora_ref, s0_ref, mup_ref, mun_ref, kk_ref, ka_ref,
                rk_ref, w0_ref, w2_ref, a0_ref, a2_ref,
                y_ref, bonus_ref, sout_ref,
                s_scr, r_scr, kd_scr, v_scr, aa_scr, b_scr, lw_scr, cl_scr, bn_scr,
                rhs_scr, out_scr, upd_scr, ti_scr, vs_scr, wc_scr, aab_scr, t_scr, *, d, nseq, nblk):
    g = pl.program_id(0)
    n_tile = nseq * nblk
    n_prob = N_CHUNK * N_PAIR
    j_a = jnp.minimum(g, n_tile - 1) % nblk
    j_b = jnp.maximum(g - 1, 0) % nblk
    masks = _wkv_masks(d)
    row0 = zero_ref[0]
    shift = CHUNK.bit_length() - 1
    ri = lax.broadcasted_iota(jnp.int32, (LANES, LANES), 0)
    ci = lax.broadcasted_iota(jnp.int32, (LANES, LANES), 1)
    eye = (ri == ci).astype(F32)
    head0 = lax.broadcasted_iota(jnp.int32, (CHUNK, LANES), 1) < HEAD_DIM
    dot = functools.partial(jnp.dot, preferred_element_type=F32)

    def stack(t):
        return jnp.concatenate([jnp.where(head0, t, 0.0), jnp.where(head0, 0.0, t)], axis=0)

    def twice(t):
        return jnp.concatenate([t, t], axis=0)

    def chunk_rows(step, anchored=False):
        lo = (N_CHUNK - 1 - step if d else step) * CHUNK
        return pl.ds(pl.multiple_of(row0 + lo, CHUNK), CHUNK) if anchored else slice(lo, lo + CHUNK)

    def prepare(slot, j):
        blk = nblk - 1 - j if d else j
        u = rkv_ref[...]
        row = lax.broadcasted_iota(jnp.int32, (TM, 1), 0)
        prow = jnp.where(blk > 0, prev_ref[HALO - 1:HALO, :], 0.0)
        nrow = jnp.where(blk < nblk - 1, next_ref[0:1, :], 0.0)
        prev = jnp.where(row == 0, prow, pltpu.roll(u, 1, axis=0))
        nxt = jnp.where(row == TM - 1, nrow, pltpu.roll(u, TM - 1, axis=0))
        x = u + (prev - u) * mup_ref[...] + (nxt - u) * mun_ref[...]
        r = x[:, 0:RWKV_WIDTH]
        k = x[:, RWKV_WIDTH:2 * RWKV_WIDTH]
        v = x[:, 2 * RWKV_WIDTH:]
        r_scr[slot] = r
        v_scr[slot] = v
        yield
        bd = _block_diag_ones()
        kkr = k * kk_ref[...]
        kk = kkr * lax.rsqrt(jnp.maximum(_head_sum(kkr * kkr, bd), 1e-24))
        aa_scr[slot] = -kk
        yield
        lo = lora_ref[...]
        z = w0_ref[...] + _mm(jnp.tanh(lo[:, 0:2 * LORA_W]), w2_ref[...])
        lw = -DECAY_SCALE * jax.nn.sigmoid(z)
        lw_scr[slot, 0:TM, :] = lw
        a = jax.nn.sigmoid(a0_ref[...] + _mm(lo[:, 2 * LORA_W:4 * LORA_W], a2_ref[...]))
        kd = k * (1.0 + (a - 1.0) * ka_ref[...])
        kd_scr[slot] = kd
        b_scr[slot] = kk * a
        yield
        bn_scr[slot] = _head_sum(r * kd * rk_ref[...], bd) * v
        yield
        rt = lax.broadcasted_iota(jnp.int32, (TM, TM), 0)
        ct = lax.broadcasted_iota(jnp.int32, (TM, TM), 1)
        cum_m = jnp.where(((rt >> shift) == (ct >> shift)) & (((ct - rt) * (1 - 2 * d)) <= 0), 1.0, 0.0)
        cl_scr[slot] = _mm_split3(cum_m.astype(BF16), lw)
        yield

    def operands(step, slot):
        strict, incl, levels, _ = masks
        rows = chunk_rows(step, anchored=True)
        for p in range(N_PAIR):
            i = slot * n_prob + step * N_PAIR + p
            sl = slice(p * LANES, (p + 1) * LANES)
            lw = lw_scr[slot, rows, sl]
            cl = cl_scr[slot, rows, sl]
            tot = jnp.sum(lw, axis=0, keepdims=True)
            e_out = jnp.exp(-cl)
            e_end = jnp.exp(tot - cl)
            b_c = b_scr[slot, rows, sl]
            kd_c = kd_scr[slot, rows, sl]
            a_s = stack(aa_scr[slot, rows, sl] * jnp.exp(cl - lw)).astype(BF16)
            r_s = stack(r_scr[slot, rows, sl] * jnp.exp(cl)).astype(BF16)
            bk_s = jnp.concatenate([twice((b_c * e_out).astype(BF16)),
                                    twice((kd_c * e_out).astype(BF16))], axis=0)
            vs_scr[i] = stack(v_scr[slot, rows, sl]).astype(BF16)
            upd_scr[i] = jnp.concatenate([jnp.transpose(stack(b_c * e_end)),
                                          jnp.transpose(stack(kd_c * e_end))], axis=1).astype(BF16)
            wc_scr[i] = jnp.transpose(jnp.broadcast_to(jnp.exp(tot), (LANES, LANES)))
            ab = _mm_nt(jnp.concatenate([a_s, r_s], axis=0), bk_s)
            a_ab = jnp.where(strict, ab[0:LANES, 0:LANES], 0.0)
            a_ak = jnp.where(strict, ab[0:LANES, LANES:], 0.0)
            a_rb = jnp.where(incl, ab[LANES:, 0:LANES], 0.0)
            a_rk = jnp.where(incl, ab[LANES:, LANES:], 0.0)
            rhs_scr[i] = jnp.concatenate([a_s, a_ak.astype(BF16)], axis=1)
            out_scr[i] = jnp.concatenate([r_s, a_rb.astype(BF16), a_rk.astype(BF16)], axis=1)
            aab_scr[i] = a_ab
            t_scr[i] = eye + jnp.where(levels[0], a_ab, 0.0)
            yield

    def invert(steps, slot):
        _, _, levels, compact = masks
        idx = [slot * n_prob + step * N_PAIR + p for step in steps for p in range(N_PAIR)]
        for lv in range(1, len(levels)):
            s = 1 << lv
            gather = s >= SUBLANES
            pieces = [slice((2 * q + 1 - d) * s, (2 * q + 2 - d) * s) for q in range(LANES // (2 * s))]

            def later(x):
                return jnp.concatenate([x[rows] for rows in pieces], axis=0)

            def spread(x):
                zero = jnp.zeros((s, LANES), x.dtype)
                halves = [[x[q * s:(q + 1) * s], zero] if d else [zero, x[q * s:(q + 1) * s]]
                          for q in range(len(pieces))]
                return jnp.concatenate([h for pair in halves for h in pair], axis=0)

            ts = [t_scr[i] for i in idx]
            tbs = [t.astype(BF16) for t in ts]
            if gather:
                ms = [dot(jnp.where(compact[lv], later(aab_scr[i]), 0.0).astype(BF16), tb)
                      for i, tb in zip(idx, tbs)]
            else:
                ms = [dot(jnp.where(levels[lv], aab_scr[i], 0.0).astype(BF16), tb) for i, tb in zip(idx, tbs)]
            yield
            for i, t, tb, m in zip(idx, ts, tbs, ms):
                if gather:
                    z = dot(later(t).astype(BF16), spread(m.astype(BF16)))
                    for q, rows in enumerate(pieces):
                        t_scr[i, rows, :] = t[rows] + z[q * s:(q + 1) * s]
                else:
                    t_scr[i] = t + dot(tb, m.astype(BF16))
            yield
        for i in idx:
            ti_scr[i] = t_scr[i].astype(BF16)

    def scan(step, slot):
        rows = chunk_rows(step)
        base = slot * n_prob + step * N_PAIR
        pairs = range(N_PAIR)
        sts = [s_scr[p] for p in pairs]
        stb = [st.astype(BF16) for st in sts]
        vss = [vs_scr[base + p] for p in pairs]
        rhs = [dot(rhs_scr[base + p], jnp.concatenate([stb[p], vss[p]], axis=0)) for p in pairs]
        lw_scr[1 - slot, TM:, 0:LANES] = rhs[0][0:SUBLANES, :]
        yield
        pmb = []
        for p in pairs:
            rh = rhs[p].astype(BF16)
            rl = (rhs[p] - rh.astype(F32)).astype(BF16)
            pm = dot(ti_scr[base + p], jnp.concatenate([rh, rl], axis=1))
            pmb.append((pm[:, 0:LANES] + pm[:, LANES:]).astype(BF16))
        yield
        for p in pairs:
            s_scr[p] = wc_scr[base + p] * sts[p] + dot(
                upd_scr[base + p], jnp.concatenate([pmb[p], vss[p]], axis=0))
        yield
        for p in pairs:
            y_s = dot(out_scr[base + p], jnp.concatenate([stb[p], pmb[p], vss[p]], axis=0))
            y_ref[rows, p * LANES:(p + 1) * LANES] = y_s[0:CHUNK] + y_s[CHUNK:]
        yield

    chain = itertools.chain.from_iterable

    def stage_a(slot):
        return prepare(slot, j_a), chain(operands(s, slot) for s in range(N_CHUNK))

    def stage_b(slot, with_inverse=(), with_scan=()):
        @pl.when(j_b == 0)
        def _():
            zero = jnp.zeros((HEAD_DIM, HEAD_DIM), F32)
            for p in range(N_PAIR):
                top = jnp.concatenate([s0_ref[0, 2 * p], zero], axis=1)
                bot = jnp.concatenate([zero, s0_ref[0, 2 * p + 1]], axis=1)
                s_scr[p] = jnp.transpose(jnp.concatenate([top, bot], axis=0))

        bonus_ref[...] = bn_scr[slot]
        _run(invert(range(N_CHUNK), slot), *with_inverse)
        _run(chain(scan(s, slot) for s in range(N_CHUNK)), *with_scan)

        @pl.when(j_b == nblk - 1)
        def _():
            for p in range(N_PAIR):
                s_vk = jnp.transpose(s_scr[p])
                sout_ref[0, 2 * p] = s_vk[0:HEAD_DIM, 0:HEAD_DIM]
                sout_ref[0, 2 * p + 1] = s_vk[HEAD_DIM:, HEAD_DIM:]

    @pl.when(g == 0)
    def _():
        prep, ops = stage_a(0)
        _run(prep)
        _run(ops)

    for parity in range(2):
        @pl.when((g > 0) & (g < n_tile) & (g % 2 == parity))
        def _():
            prep, ops = stage_a(parity)
            stage_b(1 - parity, with_inverse=(prep,), with_scan=(ops,))

    @pl.when(g == n_tile)
    def _():
        stage_b((n_tile - 1) % 2)


def _wkv(rkv, lora, s0, mup, mun, k_k, k_a, r_k, w0, w2p, a0, a2p, *, d, nseq, nblk):
    n = rkv.shape[0]
    hb = TM // HALO
    n_tile = nseq * nblk

    def tile(t):
        s = t // nblk
        j = t - s * nblk
        return s, s * nblk + (nblk - 1 - j if d else j)

    ahead = lambda g: tile(jnp.minimum(g, n_tile - 1))
    behind = lambda g: tile(jnp.maximum(g - 1, 0))
    s0_seq = (lambda s: s) if s0.shape[0] > 1 else (lambda s: 0)
    const = lambda g: (0, 0)
    out_tile = lambda g: (behind(g)[1], 0)
    kern = functools.partial(_wkv_kernel, d=d, nseq=nseq, nblk=nblk)
    tok = pltpu.VMEM((2, TM, RWKV_WIDTH), F32)
    n_prob = 2 * N_CHUNK * N_PAIR
    return pl.pallas_call(
        kern,
        grid=(n_tile + 1,),
        in_specs=[
            pl.BlockSpec(memory_space=pltpu.SMEM),
            pl.BlockSpec((TM, 3 * RWKV_WIDTH), lambda g: (ahead(g)[1], 0)),
            pl.BlockSpec((HALO, 3 * RWKV_WIDTH), lambda g: (jnp.maximum(ahead(g)[1] * hb - 1, 0), 0)),
            pl.BlockSpec((HALO, 3 * RWKV_WIDTH),
                         lambda g: (jnp.minimum((ahead(g)[1] + 1) * hb, n // HALO - 1), 0)),
            pl.BlockSpec((TM, 4 * LORA_W), lambda g: (ahead(g)[1], 0)),
            pl.BlockSpec((1, N_HEADS, HEAD_DIM, HEAD_DIM), lambda g: (s0_seq(behind(g)[0]), 0, 0, 0)),
            pl.BlockSpec((1, 3 * RWKV_WIDTH), const),
            pl.BlockSpec((1, 3 * RWKV_WIDTH), const),
            pl.BlockSpec((1, RWKV_WIDTH), const),
            pl.BlockSpec((1, RWKV_WIDTH), const),
            pl.BlockSpec((1, RWKV_WIDTH), const),
            pl.BlockSpec((1, RWKV_WIDTH), const),
            pl.BlockSpec((2 * LORA_W, RWKV_WIDTH), const),
            pl.BlockSpec((1, RWKV_WIDTH), const),
            pl.BlockSpec((2 * LORA_W, RWKV_WIDTH), const),
        ],
        out_specs=[
            pl.BlockSpec((TM, RWKV_WIDTH), out_tile),
            pl.BlockSpec((TM, RWKV_WIDTH), out_tile),
            pl.BlockSpec((1, N_HEADS, HEAD_DIM, HEAD_DIM), lambda g: (behind(g)[0], 0, 0, 0)),
        ],
        out_shape=[jax.ShapeDtypeStruct((n, RWKV_WIDTH), F32),
                   jax.ShapeDtypeStruct((n, RWKV_WIDTH), F32),
                   jax.ShapeDtypeStruct((nseq, N_HEADS, HEAD_DIM, HEAD_DIM), F32)],
        scratch_shapes=[pltpu.VMEM((N_PAIR, LANES, LANES), F32),
                        tok, tok, tok, tok, tok,
                        pltpu.VMEM((2, TM + SUBLANES, RWKV_WIDTH), F32),
                        tok, tok,
                        pltpu.VMEM((n_prob, LANES, 2 * LANES), BF16),
                        pltpu.VMEM((n_prob, LANES, 3 * LANES), BF16),
                        pltpu.VMEM((n_prob, LANES, 2 * LANES), BF16),
                        pltpu.VMEM((n_prob, LANES, LANES), BF16),
                        pltpu.VMEM((n_prob, LANES, LANES), BF16),
                        pltpu.VMEM((n_prob, LANES, LANES), F32),
                        pltpu.VMEM((n_prob, LANES, LANES), F32),
                        pltpu.VMEM((n_prob, LANES, LANES), F32)],
        compiler_params=pltpu.CompilerParams(dimension_semantics=("arbitrary",),
                                             vmem_limit_bytes=VMEM_LIMIT),
        name="wkv",
    )(jnp.zeros((1,), jnp.int32), rkv, rkv, rkv, lora, s0, mup, mun, k_k, k_a, r_k, w0, w2p, a0, a2p)


def _mid_kernel(yf_ref, yb_ref, bf_ref, bb_ref, lg_ref, glu_ref, gates_ref, x_ref, mod_ref, g2w_ref, lnxg_ref,
                lnxb_ref, wor_ref, cw_ref, cb_ref, clg_ref, clb_ref, woc_ref, wo_ref, n2g_ref,
                x1_ref, xn2_ref, pad_scr, *, seq):
    mod = mod_ref[0]
    g1 = mod[:, 2 * D_MODEL:3 * D_MODEL]
    sh2 = mod[:, 3 * D_MODEL:4 * D_MODEL]
    sc2 = mod[:, 4 * D_MODEL:5 * D_MODEL]

    bd = _block_diag_ones()
    y = yf_ref[...] + yb_ref[...]
    yc = y - _head_sum(y, bd) * (1.0 / HEAD_DIM)
    var = _head_sum(yc * yc, bd) * (1.0 / HEAD_DIM)
    yn = yc * lax.rsqrt(var + GN_EPS) * lnxg_ref[...] + lnxb_ref[...]
    g = _mm(jax.nn.sigmoid(lg_ref[...]), g2w_ref[...])
    y_r = _mm((yn + bf_ref[...] + bb_ref[...]) * g, wor_ref[...])

    y_c = _mm(_conv_module(glu_ref[...], cw_ref, cb_ref, clg_ref, clb_ref, pad_scr, seq), woc_ref[...])

    gates = gates_ref[...]
    merged = gates[:, 0:D_MODEL] * y_r + gates[:, D_MODEL:] * y_c
    x1 = x_ref[...] + g1 * _mm(merged, wo_ref[...])
    x1_ref[...] = x1
    xn2_ref[...] = (_rms(x1) * n2g_ref[...] * (1.0 + sc2) + sh2).astype(BF16)


def _mid(y_f, y_b, bn_f, bn_b, lg, glu, gates, x, mod3, mod_row, g2w, lnxg, lnxb, wor, cw, cb, clg, clb, woc,
         wo, n2g, *, seq):
    n = x.shape[0]
    const = lambda i: (0, 0)
    tok = lambda i: (i, 0)

    def full(a):
        return pl.BlockSpec(a.shape, lambda i: (0,) * a.ndim)

    return pl.pallas_call(
        functools.partial(_mid_kernel, seq=seq),
        grid=(n // TM,),
        in_specs=[pl.BlockSpec((TM, RWKV_WIDTH), tok)] * 4 + [
                  pl.BlockSpec((TM, GATE_LORA), tok),
                  pl.BlockSpec((TM, CONV_WIDTH), tok),
                  pl.BlockSpec((TM, 2 * D_MODEL), tok),
                  pl.BlockSpec((TM, D_MODEL), tok),
                  pl.BlockSpec((1, 1, 6 * D_MODEL), lambda i: (mod_row(i), 0, 0)),
                  full(g2w), full(lnxg), full(lnxb), full(wor), full(cw), full(cb), full(clg),
                  full(clb), full(woc), full(wo), full(n2g)],
        out_specs=[pl.BlockSpec((TM, D_MODEL), tok), pl.BlockSpec((TM, D_MODEL), tok)],
        out_shape=[jax.ShapeDtypeStruct((n, D_MODEL), F32),
                   jax.ShapeDtypeStruct((n, D_MODEL), BF16)],
        scratch_shapes=[pltpu.VMEM((SUBLANES, TM // seq, seq + 2 * CONV_PAD, CONV_WIDTH), F32)],
        compiler_params=pltpu.CompilerParams(dimension_semantics=("arbitrary",),
                                             vmem_limit_bytes=VMEM_LIMIT),
        name="mid",
    )(y_f, y_b, bn_f, bn_b, lg, glu, gates, x, mod3, g2w, lnxg, lnxb, wor, cw, cb, clg, clb, woc, wo, n2g)


def _mlp_kernel(x1_ref, xn2_ref, mod_ref, w1_ref, w2_ref, fg_ref, o_ref):
    g2 = mod_ref[0][:, 5 * D_MODEL:6 * D_MODEL]
    h = jnp.maximum(jnp.dot(xn2_ref[...], w1_ref[...], preferred_element_type=F32), 0.0)
    x2 = x1_ref[...] + g2 * _mm(h * h, w2_ref[...])
    o_ref[...] = _rms(x2) * fg_ref[...]


def _mlp(x1, xn2, mod3, mod_row, w1, w2, fg):
    n = x1.shape[0]
    const = lambda i: (0, 0)
    tok = lambda i: (i, 0)
    once = pl.Buffered(1)
    return pl.pallas_call(
        _mlp_kernel,
        grid=(n // TM_MLP,),
        in_specs=[pl.BlockSpec((TM_MLP, D_MODEL), tok),
                  pl.BlockSpec((TM_MLP, D_MODEL), tok),
                  pl.BlockSpec((1, 1, 6 * D_MODEL), lambda i: (mod_row(i * (TM_MLP // TM)), 0, 0)),
                  pl.BlockSpec(w1.shape, const, pipeline_mode=once),
                  pl.BlockSpec(w2.shape, const, pipeline_mode=once),
                  pl.BlockSpec((1, D_MODEL), const)],
        out_specs=pl.BlockSpec((TM_MLP, D_MODEL), tok),
        out_shape=jax.ShapeDtypeStruct((n, D_MODEL), F32),
        compiler_params=pltpu.CompilerParams(dimension_semantics=("arbitrary",),
                                             vmem_limit_bytes=VMEM_LIMIT),
        name="mlp",
    )(x1, xn2, mod3, w1, w2, fg)


def _pad_dir(w):
    return (jnp.eye(2, dtype=w.dtype)[:, :, None, None] * w[:, None]).reshape(2, 2 * w.shape[1], w.shape[2])


def kernel(x_prompt, x_sample, state_fwd, state_bwd, c, c_ctx, ada_w, ada_b, norm1_g, norm2_g, w_in,
           mu_prev, mu_next, decay_w0, decay_w1, decay_w2, iclr_a0, iclr_a1, iclr_a2, gate_g1, gate_g2,
           k_k, k_a, r_k, lnx_g, lnx_b, w_out_rwkv, conv_w, conv_b, conv_ln_g, conv_ln_b, w_out_conv,
           w_o, mlp_w1, mlp_w2, final_g):
    n_ctx, seq_ctx, _ = x_prompt.shape
    n_lat, seq_lat, _ = x_sample.shape
    depth = ada_w.shape[0]
    assert seq_ctx == TM and seq_lat % TM_MLP == 0 and GRID_W == CHUNK and c.shape[0] + 1 <= COND_ROWS
    lat_blk = seq_lat // TM
    row_ctx = lambda i: 0
    row_lat = lambda i: 1 + i // lat_blk

    xp = x_prompt.reshape(n_ctx * seq_ctx, D_MODEL)
    xs = x_sample.reshape(n_lat * seq_lat, D_MODEL)
    cond = jnp.concatenate([c_ctx[None, :], c, jnp.zeros((COND_ROWS - 1 - c.shape[0], D_MODEL), F32)], axis=0)
    zero_state = (jnp.zeros((1, N_HEADS, HEAD_DIM, HEAD_DIM), F32),) * 2
    row = lambda a: a.reshape(1, -1)
    assert depth == 1
    lay = lambda a: a.reshape(a.shape[1:])
    bf = lambda a: lay(a).astype(BF16)

    mod3 = _ada(cond, lay(ada_w), lay(ada_b)).reshape(COND_ROWS, 1, 6 * D_MODEL)
    w_lora = jnp.concatenate([decay_w1[0, 0], decay_w1[0, 1], iclr_a1[0, 0], iclr_a1[0, 1], gate_g1[0]],
                             axis=1).astype(BF16)
    conv_taps = jnp.broadcast_to(conv_w[0, :, None, :], (CONV_K, SUBLANES, CONV_WIDTH))
    front_w = (norm1_g, bf(w_in), w_lora)
    w2p = _pad_dir(lay(decay_w2)).astype(BF16)
    a2p = _pad_dir(lay(iclr_a2)).astype(BF16)
    wkv_w = [(mu_prev, mu_next, k_k, k_a, row(r_k), decay_w0[:, d], w2p[d], iclr_a0[:, d], a2p[d])
             for d in range(2)]
    mid_w = (bf(gate_g2), lnx_g, lnx_b, bf(w_out_rwkv), conv_taps, conv_b, conv_ln_g, conv_ln_b,
             bf(w_out_conv), bf(w_o), norm2_g)
    mlp_w = (bf(mlp_w1), bf(mlp_w2), row(final_g))
    s0_lat = (state_fwd[:, 0].astype(F32), state_bwd[:, 0].astype(F32))

    def layer(x, mod_row, s0, nseq, nblk, conv_seq):
        rkv, glu, gates, lora, lg = _front(x, mod3, mod_row, *front_w)
        y_f, bn_f, s_f = _wkv(rkv, lora, s0[0], *wkv_w[0], d=0, nseq=nseq, nblk=nblk)
        y_b, bn_b, s_b = _wkv(rkv, lora, s0[1], *wkv_w[1], d=1, nseq=nseq, nblk=nblk)
        x1, xn2 = _mid(y_f, y_b, bn_f, bn_b, lg, glu, gates, x, mod3, mod_row, *mid_w, seq=conv_seq)
        return _mlp(x1, xn2, mod3, mod_row, *mlp_w), (s_f, s_b)

    yp, s_ctx = layer(xp, row_ctx, zero_state, n_ctx, 1, seq_ctx)
    ys, _ = layer(xs, row_lat, s0_lat, n_lat, lat_blk, GRID_W)
    s_f, s_b = (s.astype(x_prompt.dtype)[:, None] for s in s_ctx)
    return (yp.reshape(x_prompt.shape), ys.reshape(x_sample.shape), s_f, s_b)
```

```python
import functools
import itertools

import jax
import jax.numpy as jnp
from jax import lax
from jax.experimental import pallas as pl
from jax.experimental.pallas import tpu as pltpu

D_MODEL = 1024
RWKV_WIDTH = 512
HEAD_DIM = 64
N_HEADS = RWKV_WIDTH // HEAD_DIM
CONV_WIDTH = 512
CONV_K = 31
D_FF = 4 * D_MODEL
GRID_W = 64
LORA_W = 64
GATE_LORA = 128
RMS_EPS = 1e-6
LN_EPS = 1e-5
GN_EPS = 64e-5

LANES = 128
SUBLANES = 8
TM = 256
TM_MLP = 512
COND_ROWS = 16
ADA_COLS = 1024
CHUNK = 64
N_CHUNK = TM // CHUNK
N_PAIR = RWKV_WIDTH // LANES
HALO = SUBLANES
CONV_PAD = 16
CONV_ROWS = 32
VMEM_LIMIT = 56 * 1024 * 1024

F32 = jnp.float32
BF16 = jnp.bfloat16
DECAY_SCALE = 0.6065306597126334


def _mm(a, b):
    return jnp.dot(a.astype(BF16), b.astype(BF16), preferred_element_type=F32)


def _mm_nt(a, b):
    return lax.dot_general(a.astype(BF16), b.astype(BF16), (((1,), (1,)), ((), ())),
                           preferred_element_type=F32)


def _split3(x):
    x1 = x.astype(BF16)
    r1 = x - x1.astype(F32)
    x2 = r1.astype(BF16)
    x3 = (r1 - x2.astype(F32)).astype(BF16)
    return x3, x2, x1


def _mm_split3(m, x):
    x3, x2, x1 = (jnp.dot(m, t, preferred_element_type=F32) for t in _split3(x))
    return x3 + x2 + x1


def _head_sum(x, bd):
    rows = x.shape[0]
    n_grp = x.shape[1] // LANES
    terms = []
    for p in range(n_grp):
        xp = x[:, p * LANES:(p + 1) * LANES]
        head = xp.astype(BF16)
        terms += [head, (xp - head.astype(F32)).astype(BF16)]
    prod = jnp.dot(jnp.concatenate(terms, axis=0), bd, preferred_element_type=F32)
    sums = [prod[(2 * p) * rows:(2 * p + 1) * rows] + prod[(2 * p + 1) * rows:(2 * p + 2) * rows]
            for p in range(n_grp)]
    return jnp.concatenate(sums, axis=1)


def _block_diag_ones():
    ri = lax.broadcasted_iota(jnp.int32, (LANES, LANES), 0)
    ci = lax.broadcasted_iota(jnp.int32, (LANES, LANES), 1)
    shift = HEAD_DIM.bit_length() - 1
    return jnp.where((ri >> shift) == (ci >> shift), 1.0, 0.0).astype(BF16)


def _rms(x):
    return x * lax.rsqrt(jnp.mean(x * x, axis=-1, keepdims=True) + RMS_EPS)


def _ada_kernel(c_ref, w_ref, b_ref, o_ref):
    c = c_ref[...]
    rows = c.shape[0]
    terms = jnp.concatenate(_split3(c * jax.nn.sigmoid(c)), axis=0)
    prod = jnp.dot(terms, w_ref[...].astype(BF16), preferred_element_type=F32)
    o_ref[...] = prod[0:rows] + prod[rows:2 * rows] + prod[2 * rows:] + b_ref[...]


def _ada(cond, w, b):
    n = w.shape[1]
    rows = cond.shape[0]
    tn = ADA_COLS
    return pl.pallas_call(
        _ada_kernel,
        grid=(n // tn,),
        in_specs=[pl.BlockSpec((rows, D_MODEL), lambda i: (0, 0)),
                  pl.BlockSpec((D_MODEL, tn), lambda i: (0, i)),
                  pl.BlockSpec((1, tn), lambda i: (0, i))],
        out_specs=pl.BlockSpec((rows, tn), lambda i: (0, i)),
        out_shape=jax.ShapeDtypeStruct((rows, n), F32),
        compiler_params=pltpu.CompilerParams(dimension_semantics=("arbitrary",),
                                             vmem_limit_bytes=VMEM_LIMIT),
        name="ada",
    )(cond, w, b.reshape(1, n))


_IN_SPLIT = (3 * RWKV_WIDTH, 2 * CONV_WIDTH, 2 * D_MODEL)
_LORA_SPLIT = (4 * LORA_W, GATE_LORA)


def _run(*streams):
    streams = list(streams)
    while streams:
        streams = [g for g in streams if next(g, True) is None]


def _conv_module(uu, cw_ref, cb_ref, clg_ref, clb_ref, pad_scr, seq):
    nseq = TM // seq
    zpad = jnp.zeros((nseq, CONV_PAD, CONV_WIDTH), F32)
    pad_scr[0, :, 0:CONV_PAD, :] = zpad
    pad_scr[0, :, CONV_PAD + seq:, :] = zpad
    for s in range(nseq):
        pad_scr[0, s, CONV_PAD:CONV_PAD + seq, :] = uu[s * seq:(s + 1) * seq]
    live = seq + 2 * CONV_PAD - SUBLANES
    for m in range(1, SUBLANES):
        pad_scr[m, :, 0:live, :] = pad_scr[0, :, m:m + live, :]
    first = CONV_PAD - CONV_K // 2
    groups = (CONV_ROWS // SUBLANES, SUBLANES, CONV_WIDTH)
    parts = []
    for s in range(nseq):
        for r0 in range(0, seq, CONV_ROWS):
            acc = jnp.broadcast_to(cb_ref[...], groups)
            for t in range(CONV_K):
                m = (first + t) % SUBLANES
                lo = r0 + first + t - m
                acc = acc + pad_scr[m, s, lo:lo + CONV_ROWS, :].reshape(groups) * cw_ref[t][None]
            parts.append(acc.reshape(CONV_ROWS, CONV_WIDTH))
    cv = jnp.concatenate(parts, axis=0)
    cm = cv - jnp.mean(cv, axis=-1, keepdims=True)
    cvar = jnp.mean(cm * cm, axis=-1, keepdims=True)
    un = cm * lax.rsqrt(cvar + LN_EPS) * clg_ref[...] + clb_ref[...]
    return un * jax.nn.sigmoid(un)


def _front_kernel(x_ref, mod_ref, g_ref, win_ref, wlo_ref, rkv_ref, glu_ref, gates_ref, lora_ref, lg_ref):
    mod = mod_ref[0]
    sh1 = mod[:, 0:D_MODEL]
    sc1 = mod[:, D_MODEL:2 * D_MODEL]
    xn = (_rms(x_ref[...]) * g_ref[...] * (1.0 + sc1) + sh1).astype(BF16)
    c0, c1, c2 = itertools.accumulate(_IN_SPLIT)
    dot = functools.partial(jnp.dot, preferred_element_type=F32)
    rkv_ref[...] = dot(xn, win_ref[:, 0:c0])
    glu = dot(xn, win_ref[:, c0:c1])
    glu_ref[...] = glu[:, 0:CONV_WIDTH] * jax.nn.sigmoid(glu[:, CONV_WIDTH:])
    gates_ref[...] = jax.nn.sigmoid(dot(xn, win_ref[:, c1:c2]))
    lora_ref[...] = dot(xn, wlo_ref[:, 0:_LORA_SPLIT[0]])
    lg_ref[...] = dot(xn, wlo_ref[:, _LORA_SPLIT[0]:])


def _front(x, mod3, mod_row, g, w_in, w_lora):
    n = x.shape[0]
    const = lambda i: (0, 0)
    tok = lambda i: (i, 0)
    widths = (_IN_SPLIT[0], CONV_WIDTH, _IN_SPLIT[2]) + _LORA_SPLIT
    return pl.pallas_call(
        _front_kernel,
        grid=(n // TM,),
        in_specs=[pl.BlockSpec((TM, D_MODEL), tok),
                  pl.BlockSpec((1, 1, 6 * D_MODEL), lambda i: (mod_row(i), 0, 0)),
                  pl.BlockSpec((1, D_MODEL), const),
                  pl.BlockSpec(w_in.shape, const),
                  pl.BlockSpec(w_lora.shape, const)],
        out_specs=[pl.BlockSpec((TM, w), tok) for w in widths],
        out_shape=[jax.ShapeDtypeStruct((n, w), F32) for w in widths],
        compiler_params=pltpu.CompilerParams(dimension_semantics=("arbitrary",),
                                             vmem_limit_bytes=VMEM_LIMIT),
        name="front",
    )(x, mod3, g, w_in, w_lora)


def _wkv_masks(d):
    shift = CHUNK.bit_length() - 1
    ri = lax.broadcasted_iota(jnp.int32, (LANES, LANES), 0)
    ci = lax.broadcasted_iota(jnp.int32, (LANES, LANES), 1)
    same = (ri >> shift) == (ci >> shift)
    rel = (ci - ri) * (1 - 2 * d)
    levels = [((ri >> (lv + 1)) == (ci >> (lv + 1))) & (((ri >> lv) & 1) == 1 - d) & (((ci >> lv) & 1) == d)
              for lv in range(shift)]
    rc = lax.broadcasted_iota(jnp.int32, (LANES // 2, LANES), 0)
    cc = lax.broadcasted_iota(jnp.int32, (LANES // 2, LANES), 1)
    compact = [((cc >> (lv + 1)) == (rc >> lv)) & (((cc >> lv) & 1) == d) for lv in range(shift)]
    return same & (rel < 0), same & (rel <= 0), levels, compact


def _wkv_kernel(zero_ref, rkv_ref, prev_ref, next_ref, lora_ref, s0_ref, mup_ref, mun_ref, kk_ref, ka_ref,
                rk_ref, w0_ref, w2_ref, a0_ref, a2_ref,
                y_ref, bonus_ref, sout_ref,
                s_scr, r_scr, kd_scr, v_scr, aa_scr, b_scr, lw_scr, cl_scr, bn_scr,
                rhs_scr, out_scr, upd_scr, ti_scr, vs_scr, wc_scr, aab_scr, t_scr, *, d, nseq, nblk):
    g = pl.program_id(0)
    n_tile = nseq * nblk
    n_prob = N_CHUNK * N_PAIR
    j_a = jnp.minimum(g, n_tile - 1) % nblk
    j_b = jnp.maximum(g - 1, 0) % nblk
    masks = _wkv_masks(d)
    row0 = zero_ref[0]
    shift = CHUNK.bit_length() - 1
    ri = lax.broadcasted_iota(jnp.int32, (LANES, LANES), 0)
    ci = lax.broadcasted_iota(jnp.int32, (LANES, LANES), 1)
    eye = (ri == ci).astype(F32)
    head0 = lax.broadcasted_iota(jnp.int32, (CHUNK, LANES), 1) < HEAD_DIM
    dot = functools.partial(jnp.dot, preferred_element_type=F32)

    def stack(t):
        return jnp.concatenate([jnp.where(head0, t, 0.0), jnp.where(head0, 0.0, t)], axis=0)

    def twice(t):
        return jnp.concatenate([t, t], axis=0)

    def chunk_rows(step, anchored=False):
        lo = (N_CHUNK - 1 - step if d else step) * CHUNK
        return pl.ds(pl.multiple_of(row0 + lo, CHUNK), CHUNK) if anchored else slice(lo, lo + CHUNK)

    def prepare(slot, j):
        blk = nblk - 1 - j if d else j
        u = rkv_ref[...]
        row = lax.broadcasted_iota(jnp.int32, (TM, 1), 0)
        prow = jnp.where(blk > 0, prev_ref[HALO - 1:HALO, :], 0.0)
        nrow = jnp.where(blk < nblk - 1, next_ref[0:1, :], 0.0)
        prev = jnp.where(row == 0, prow, pltpu.roll(u, 1, axis=0))
        nxt = jnp.where(row == TM - 1, nrow, pltpu.roll(u, TM - 1, axis=0))
        x = u + (prev - u) * mup_ref[...] + (nxt - u) * mun_ref[...]
        r = x[:, 0:RWKV_WIDTH]
        k = x[:, RWKV_WIDTH:2 * RWKV_WIDTH]
        v = x[:, 2 * RWKV_WIDTH:]
        r_scr[slot] = r
        v_scr[slot] = v
        yield
        bd = _block_diag_ones()
        kkr = k * kk_ref[...]
        kk = kkr * lax.rsqrt(jnp.maximum(_head_sum(kkr * kkr, bd), 1e-24))
        aa_scr[slot] = -kk
        yield
        lo = lora_ref[...]
        z = w0_ref[...] + _mm(jnp.tanh(lo[:, 0:2 * LORA_W]), w2_ref[...])
        lw = -DECAY_SCALE * jax.nn.sigmoid(z)
        lw_scr[slot, 0:TM, :] = lw
        a = jax.nn.sigmoid(a0_ref[...] + _mm(lo[:, 2 * LORA_W:4 * LORA_W], a2_ref[...]))
        kd = k * (1.0 + (a - 1.0) * ka_ref[...])
        kd_scr[slot] = kd
        b_scr[slot] = kk * a
        yield
        bn_scr[slot] = _head_sum(r * kd * rk_ref[...], bd) * v
        yield
        rt = lax.broadcasted_iota(jnp.int32, (TM, TM), 0)
        ct = lax.broadcasted_iota(jnp.int32, (TM, TM), 1)
        cum_m = jnp.where(((rt >> shift) == (ct >> shift)) & (((ct - rt) * (1 - 2 * d)) <= 0), 1.0, 0.0)
        cl_scr[slot] = _mm_split3(cum_m.astype(BF16), lw)
        yield

    def operands(step, slot):
        strict, incl, levels, _ = masks
        rows = chunk_rows(step, anchored=True)
        for p in range(N_PAIR):
            i = slot * n_prob + step * N_PAIR + p
            sl = slice(p * LANES, (p + 1) * LANES)
            lw = lw_scr[slot, rows, sl]
            cl = cl_scr[slot, rows, sl]
            tot = jnp.sum(lw, axis=0, keepdims=True)
            e_out = jnp.exp(-cl)
            e_end = jnp.exp(tot - cl)
            b_c = b_scr[slot, rows, sl]
            kd_c = kd_scr[slot, rows, sl]
            a_s = stack(aa_scr[slot, rows, sl] * jnp.exp(cl - lw)).astype(BF16)
            r_s = stack(r_scr[slot, rows, sl] * jnp.exp(cl)).astype(BF16)
            bk_s = jnp.concatenate([twice((b_c * e_out).astype(BF16)),
                                    twice((kd_c * e_out).astype(BF16))], axis=0)
            vs_scr[i] = stack(v_scr[slot, rows, sl]).astype(BF16)
            upd_scr[i] = jnp.concatenate([jnp.transpose(stack(b_c * e_end)),
                                          jnp.transpose(stack(kd_c * e_end))], axis=1).astype(BF16)
            wc_scr[i] = jnp.transpose(jnp.broadcast_to(jnp.exp(tot), (LANES, LANES)))
            ab = _mm_nt(jnp.concatenate([a_s, r_s], axis=0), bk_s)
            a_ab = jnp.where(strict, ab[0:LANES, 0:LANES], 0.0)
            a_ak = jnp.where(strict, ab[0:LANES, LANES:], 0.0)
            a_rb = jnp.where(incl, ab[LANES:, 0:LANES], 0.0)
            a_rk = jnp.where(incl, ab[LANES:, LANES:], 0.0)
            rhs_scr[i] = jnp.concatenate([a_s, a_ak.astype(BF16)], axis=1)
            out_scr[i] = jnp.concatenate([r_s, a_rb.astype(BF16), a_rk.astype(BF16)], axis=1)
            aab_scr[i] = a_ab
            t_scr[i] = eye + jnp.where(levels[0], a_ab, 0.0)
            yield

    def invert(steps, slot):
        _, _, levels, compact = masks
        idx = [slot * n_prob + step * N_PAIR + p for step in steps for p in range(N_PAIR)]
        for lv in range(1, len(levels)):
            s = 1 << lv
            gather = s >= SUBLANES
            pieces = [slice((2 * q + 1 - d) * s, (2 * q + 2 - d) * s) for q in range(LANES // (2 * s))]

            def later(x):
                return jnp.concatenate([x[rows] for rows in pieces], axis=0)

            def spread(x):
                zero = jnp.zeros((s, LANES), x.dtype)
                halves = [[x[q * s:(q + 1) * s], zero] if d else [zero, x[q * s:(q + 1) * s]]
                          for q in range(len(pieces))]
                return jnp.concatenate([h for pair in halves for h in pair], axis=0)

            ts = [t_scr[i] for i in idx]
            tbs = [t.astype(BF16) for t in ts]
            if gather:
                ms = [dot(jnp.where(compact[lv], later(aab_scr[i]), 0.0).astype(BF16), tb)
                      for i, tb in zip(idx, tbs)]
            else:
                ms = [dot(jnp.where(levels[lv], aab_scr[i], 0.0).astype(BF16), tb) for i, tb in zip(idx, tbs)]
            yield
            for i, t, tb, m in zip(idx, ts, tbs, ms):
                if gather:
                    z = dot(later(t).astype(BF16), spread(m.astype(BF16)))
                    for q, rows in enumerate(pieces):
                        t_scr[i, rows, :] = t[rows] + z[q * s:(q + 1) * s]
                else:
                    t_scr[i] = t + dot(tb, m.astype(BF16))
            yield
        for i in idx:
            ti_scr[i] = t_scr[i].astype(BF16)

    def scan(step, slot):
        rows = chunk_rows(step)
        base = slot * n_prob + step * N_PAIR
        pairs = range(N_PAIR)
        sts = [s_scr[p] for p in pairs]
        stb = [st.astype(BF16) for st in sts]
        vss = [vs_scr[base + p] for p in pairs]
        rhs = [dot(rhs_scr[base + p], jnp.concatenate([stb[p], vss[p]], axis=0)) for p in pairs]
        lw_scr[1 - slot, TM:, 0:LANES] = rhs[0][0:SUBLANES, :]
        yield
        pmb = []
        for p in pairs:
            rh = rhs[p].astype(BF16)
            rl = (rhs[p] - rh.astype(F32)).astype(BF16)
            pm = dot(ti_scr[base + p], jnp.concatenate([rh, rl], axis=1))
            pmb.append((pm[:, 0:LANES] + pm[:, LANES:]).astype(BF16))
        yield
        for p in pairs:
            s_scr[p] = wc_scr[base + p] * sts[p] + dot(
                upd_scr[base + p], jnp.concatenate([pmb[p], vss[p]], axis=0))
        yield
        for p in pairs:
            y_s = dot(out_scr[base + p], jnp.concatenate([stb[p], pmb[p], vss[p]], axis=0))
            y_ref[rows, p * LANES:(p + 1) * LANES] = y_s[0:CHUNK] + y_s[CHUNK:]
        yield

    chain = itertools.chain.from_iterable

    def stage_a(slot):
        return prepare(slot, j_a), chain(operands(s, slot) for s in range(N_CHUNK))

    def stage_b(slot, with_inverse=(), with_scan=()):
        @pl.when(j_b == 0)
        def _():
            zero = jnp.zeros((HEAD_DIM, HEAD_DIM), F32)
            for p in range(N_PAIR):
                top = jnp.concatenate([s0_ref[0, 2 * p], zero], axis=1)
                bot = jnp.concatenate([zero, s0_ref[0, 2 * p + 1]], axis=1)
                s_scr[p] = jnp.transpose(jnp.concatenate([top, bot], axis=0))

        bonus_ref[...] = bn_scr[slot]
        _run(invert(range(N_CHUNK), slot), *with_inverse)
        _run(chain(scan(s, slot) for s in range(N_CHUNK)), *with_scan)

        @pl.when(j_b == nblk - 1)
        def _():
            for p in range(N_PAIR):
                s_vk = jnp.transpose(s_scr[p])
                sout_ref[0, 2 * p] = s_vk[0:HEAD_DIM, 0:HEAD_DIM]
                sout_ref[0, 2 * p + 1] = s_vk[HEAD_DIM:, HEAD_DIM:]

    @pl.when(g == 0)
    def _():
        prep, ops = stage_a(0)
        _run(prep)
        _run(ops)

    for parity in range(2):
        @pl.when((g > 0) & (g < n_tile) & (g % 2 == parity))
        def _():
            prep, ops = stage_a(parity)
            stage_b(1 - parity, with_inverse=(prep,), with_scan=(ops,))

    @pl.when(g == n_tile)
    def _():
        stage_b((n_tile - 1) % 2)


def _wkv(rkv, lora, s0, mup, mun, k_k, k_a, r_k, w0, w2p, a0, a2p, *, d, nseq, nblk):
    n = rkv.shape[0]
    hb = TM // HALO
    n_tile = nseq * nblk

    def tile(t):
        s = t // nblk
        j = t - s * nblk
        return s, s * nblk + (nblk - 1 - j if d else j)

    ahead = lambda g: tile(jnp.minimum(g, n_tile - 1))
    behind = lambda g: tile(jnp.maximum(g - 1, 0))
    s0_seq = (lambda s: s) if s0.shape[0] > 1 else (lambda s: 0)
    const = lambda g: (0, 0)
    out_tile = lambda g: (behind(g)[1], 0)
    kern = functools.partial(_wkv_kernel, d=d, nseq=nseq, nblk=nblk)
    tok = pltpu.VMEM((2, TM, RWKV_WIDTH), F32)
    n_prob = 2 * N_CHUNK * N_PAIR
    return pl.pallas_call(
        kern,
        grid=(n_tile + 1,),
        in_specs=[
            pl.BlockSpec(memory_space=pltpu.SMEM),
            pl.BlockSpec((TM, 3 * RWKV_WIDTH), lambda g: (ahead(g)[1], 0)),
            pl.BlockSpec((HALO, 3 * RWKV_WIDTH), lambda g: (jnp.maximum(ahead(g)[1] * hb - 1, 0), 0)),
            pl.BlockSpec((HALO, 3 * RWKV_WIDTH),
                         lambda g: (jnp.minimum((ahead(g)[1] + 1) * hb, n // HALO - 1), 0)),
            pl.BlockSpec((TM, 4 * LORA_W), lambda g: (ahead(g)[1], 0)),
            pl.BlockSpec((1, N_HEADS, HEAD_DIM, HEAD_DIM), lambda g: (s0_seq(behind(g)[0]), 0, 0, 0)),
            pl.BlockSpec((1, 3 * RWKV_WIDTH), const),
            pl.BlockSpec((1, 3 * RWKV_WIDTH), const),
            pl.BlockSpec((1, RWKV_WIDTH), const),
            pl.BlockSpec((1, RWKV_WIDTH), const),
            pl.BlockSpec((1, RWKV_WIDTH), const),
            pl.BlockSpec((1, RWKV_WIDTH), const),
            pl.BlockSpec((2 * LORA_W, RWKV_WIDTH), const),
            pl.BlockSpec((1, RWKV_WIDTH), const),
            pl.BlockSpec((2 * LORA_W, RWKV_WIDTH), const),
        ],
        out_specs=[
            pl.BlockSpec((TM, RWKV_WIDTH), out_tile),
            pl.BlockSpec((TM, RWKV_WIDTH), out_tile),
            pl.BlockSpec((1, N_HEADS, HEAD_DIM, HEAD_DIM), lambda g: (behind(g)[0], 0, 0, 0)),
        ],
        out_shape=[jax.ShapeDtypeStruct((n, RWKV_WIDTH), F32),
                   jax.ShapeDtypeStruct((n, RWKV_WIDTH), F32),
                   jax.ShapeDtypeStruct((nseq, N_HEADS, HEAD_DIM, HEAD_DIM), F32)],
        scratch_shapes=[pltpu.VMEM((N_PAIR, LANES, LANES), F32),
                        tok, tok, tok, tok, tok,
                        pltpu.VMEM((2, TM + SUBLANES, RWKV_WIDTH), F32),
                        tok, tok,
                        pltpu.VMEM((n_prob, LANES, 2 * LANES), BF16),
                        pltpu.VMEM((n_prob, LANES, 3 * LANES), BF16),
                        pltpu.VMEM((n_prob, LANES, 2 * LANES), BF16),
                        pltpu.VMEM((n_prob, LANES, LANES), BF16),
                        pltpu.VMEM((n_prob, LANES, LANES), BF16),
                        pltpu.VMEM((n_prob, LANES, LANES), F32),
                        pltpu.VMEM((n_prob, LANES, LANES), F32),
                        pltpu.VMEM((n_prob, LANES, LANES), F32)],
        compiler_params=pltpu.CompilerParams(dimension_semantics=("arbitrary",),
                                             vmem_limit_bytes=VMEM_LIMIT),
        name="wkv",
    )(jnp.zeros((1,), jnp.int32), rkv, rkv, rkv, lora, s0, mup, mun, k_k, k_a, r_k, w0, w2p, a0, a2p)


def _mid_kernel(yf_ref, yb_ref, bf_ref, bb_ref, lg_ref, glu_ref, gates_ref, x_ref, mod_ref, g2w_ref, lnxg_ref,
                lnxb_ref, wor_ref, cw_ref, cb_ref, clg_ref, clb_ref, woc_ref, wo_ref,
                x1_ref, pad_scr, *, seq):
    g1 = mod_ref[0][:, 2 * D_MODEL:3 * D_MODEL]

    bd = _block_diag_ones()
    y = yf_ref[...] + yb_ref[...]
    yc = y - _head_sum(y, bd) * (1.0 / HEAD_DIM)
    var = _head_sum(yc * yc, bd) * (1.0 / HEAD_DIM)
    yn = yc * lax.rsqrt(var + GN_EPS) * lnxg_ref[...] + lnxb_ref[...]
    g = _mm(jax.nn.sigmoid(lg_ref[...]), g2w_ref[...])
    y_r = _mm((yn + bf_ref[...] + bb_ref[...]) * g, wor_ref[...])

    y_c = _mm(_conv_module(glu_ref[...], cw_ref, cb_ref, clg_ref, clb_ref, pad_scr, seq), woc_ref[...])

    gates = gates_ref[...]
    merged = gates[:, 0:D_MODEL] * y_r + gates[:, D_MODEL:] * y_c
    x1_ref[...] = x_ref[...] + g1 * _mm(merged, wo_ref[...])


def _mid(y_f, y_b, bn_f, bn_b, lg, glu, gates, x, mod3, mod_row, g2w, lnxg, lnxb, wor, cw, cb, clg, clb, woc,
         wo, *, seq):
    n = x.shape[0]
    const = lambda i: (0, 0)
    tok = lambda i: (i, 0)

    def full(a):
        return pl.BlockSpec(a.shape, lambda i: (0,) * a.ndim)

    return pl.pallas_call(
        functools.partial(_mid_kernel, seq=seq),
        grid=(n // TM,),
        in_specs=[pl.BlockSpec((TM, RWKV_WIDTH), tok)] * 4 + [
                  pl.BlockSpec((TM, GATE_LORA), tok),
                  pl.BlockSpec((TM, CONV_WIDTH), tok),
                  pl.BlockSpec((TM, 2 * D_MODEL), tok),
                  pl.BlockSpec((TM, D_MODEL), tok),
                  pl.BlockSpec((1, 1, 6 * D_MODEL), lambda i: (mod_row(i), 0, 0)),
                  full(g2w), full(lnxg), full(lnxb), full(wor), full(cw), full(cb), full(clg),
                  full(clb), full(woc), full(wo)],
        out_specs=pl.BlockSpec((TM, D_MODEL), tok),
        out_shape=jax.ShapeDtypeStruct((n, D_MODEL), F32),
        scratch_shapes=[pltpu.VMEM((SUBLANES, TM // seq, seq + 2 * CONV_PAD, CONV_WIDTH), F32)],
        compiler_params=pltpu.CompilerParams(dimension_semantics=("arbitrary",),
                                             vmem_limit_bytes=VMEM_LIMIT),
        name="mid",
    )(y_f, y_b, bn_f, bn_b, lg, glu, gates, x, mod3, g2w, lnxg, lnxb, wor, cw, cb, clg, clb, woc, wo)


def _mlp_kernel(x1_ref, mod_ref, n2g_ref, w1_ref, w2_ref, fg_ref, o_ref):
    mod = mod_ref[0]
    sh2 = mod[:, 3 * D_MODEL:4 * D_MODEL]
    sc2 = mod[:, 4 * D_MODEL:5 * D_MODEL]
    g2 = mod[:, 5 * D_MODEL:6 * D_MODEL]
    x1 = x1_ref[...]
    xn2 = _rms(x1) * n2g_ref[...] * (1.0 + sc2) + sh2
    h = jnp.maximum(_mm(xn2, w1_ref[...]), 0.0)
    x2 = x1 + g2 * _mm(h * h, w2_ref[...])
    o_ref[...] = _rms(x2) * fg_ref[...]


def _mlp(x1, mod3, mod_row, n2g, w1, w2, fg):
    n = x1.shape[0]
    const = lambda i: (0, 0)
    tok = lambda i: (i, 0)
    once = pl.Buffered(1)
    return pl.pallas_call(
        _mlp_kernel,
        grid=(n // TM_MLP,),
        in_specs=[pl.BlockSpec((TM_MLP, D_MODEL), tok),
                  pl.BlockSpec((1, 1, 6 * D_MODEL), lambda i: (mod_row(i * (TM_MLP // TM)), 0, 0)),
                  pl.BlockSpec((1, D_MODEL), const),
                  pl.BlockSpec(w1.shape, const, pipeline_mode=once),
                  pl.BlockSpec(w2.shape, const, pipeline_mode=once),
                  pl.BlockSpec((1, D_MODEL), const)],
        out_specs=pl.BlockSpec((TM_MLP, D_MODEL), tok),
        out_shape=jax.ShapeDtypeStruct((n, D_MODEL), F32),
        compiler_params=pltpu.CompilerParams(dimension_semantics=("arbitrary",),
                                             vmem_limit_bytes=VMEM_LIMIT),
        name="mlp",
    )(x1, mod3, n2g, w1, w2, fg)


def _pad_dir(w):
    return (jnp.eye(2, dtype=w.dtype)[:, :, None, None] * w[:, None]).reshape(2, 2 * w.shape[1], w.shape[2])


def kernel(x_prompt, x_sample, state_fwd, state_bwd, c, c_ctx, ada_w, ada_b, norm1_g, norm2_g, w_in,
           mu_prev, mu_next, decay_w0, decay_w1, decay_w2, iclr_a0, iclr_a1, iclr_a2, gate_g1, gate_g2,
           k_k, k_a, r_k, lnx_g, lnx_b, w_out_rwkv, conv_w, conv_b, conv_ln_g, conv_ln_b, w_out_conv,
           w_o, mlp_w1, mlp_w2, final_g):
    n_ctx, seq_ctx, _ = x_prompt.shape
    n_lat, seq_lat, _ = x_sample.shape
    depth = ada_w.shape[0]
    assert seq_ctx == TM and seq_lat % TM_MLP == 0 and GRID_W == CHUNK and c.shape[0] + 1 <= COND_ROWS
    lat_blk = seq_lat // TM
    row_ctx = lambda i: 0
    row_lat = lambda i: 1 + i // lat_blk

    xp = x_prompt.reshape(n_ctx * seq_ctx, D_MODEL)
    xs = x_sample.reshape(n_lat * seq_lat, D_MODEL)
    cond = jnp.concatenate([c_ctx[None, :], c, jnp.zeros((COND_ROWS - 1 - c.shape[0], D_MODEL), F32)], axis=0)
    zero_state = (jnp.zeros((1, N_HEADS, HEAD_DIM, HEAD_DIM), F32),) * 2
    row = lambda a: a.reshape(1, -1)
    assert depth == 1
    lay = lambda a: a.reshape(a.shape[1:])
    bf = lambda a: lay(a).astype(BF16)

    mod3 = _ada(cond, lay(ada_w), lay(ada_b)).reshape(COND_ROWS, 1, 6 * D_MODEL)
    w_lora = jnp.concatenate([decay_w1[0, 0], decay_w1[0, 1], iclr_a1[0, 0], iclr_a1[0, 1], gate_g1[0]],
                             axis=1).astype(BF16)
    conv_taps = jnp.broadcast_to(conv_w[0, :, None, :], (CONV_K, SUBLANES, CONV_WIDTH))
    front_w = (norm1_g, bf(w_in), w_lora)
    w2p = _pad_dir(lay(decay_w2)).astype(BF16)
    a2p = _pad_dir(lay(iclr_a2)).astype(BF16)
    wkv_w = [(mu_prev, mu_next, k_k, k_a, row(r_k), decay_w0[:, d], w2p[d], iclr_a0[:, d], a2p[d])
             for d in range(2)]
    mid_w = (bf(gate_g2), lnx_g, lnx_b, bf(w_out_rwkv), conv_taps, conv_b, conv_ln_g, conv_ln_b,
             bf(w_out_conv), bf(w_o))
    mlp_w = (norm2_g, bf(mlp_w1), bf(mlp_w2), row(final_g))
    s0_lat = (state_fwd[:, 0].astype(F32), state_bwd[:, 0].astype(F32))

    def layer(x, mod_row, s0, nseq, nblk, conv_seq):
        rkv, glu, gates, lora, lg = _front(x, mod3, mod_row, *front_w)
        y_f, bn_f, s_f = _wkv(rkv, lora, s0[0], *wkv_w[0], d=0, nseq=nseq, nblk=nblk)
        y_b, bn_b, s_b = _wkv(rkv, lora, s0[1], *wkv_w[1], d=1, nseq=nseq, nblk=nblk)
        x1 = _mid(y_f, y_b, bn_f, bn_b, lg, glu, gates, x, mod3, mod_row, *mid_w, seq=conv_seq)
        return _mlp(x1, mod3, mod_row, *mlp_w), (s_f, s_b)

    yp, s_ctx = layer(xp, row_ctx, zero_state, n_ctx, 1, seq_ctx)
    ys, _ = layer(xs, row_lat, s0_lat, n_lat, lat_blk, GRID_W)
    s_f, s_b = (s.astype(x_prompt.dtype)[:, None] for s in s_ctx)
    return (yp.reshape(x_prompt.shape), ys.reshape(x_sample.shape), s_f, s_b)
```

```python
import functools
import itertools

import jax
import jax.numpy as jnp
from jax import lax
from jax.experimental import pallas as pl
from jax.experimental.pallas import tpu as pltpu

D_MODEL = 1024
RWKV_WIDTH = 512
HEAD_DIM = 64
N_HEADS = RWKV_WIDTH // HEAD_DIM
CONV_WIDTH = 512
CONV_K = 31
D_FF = 4 * D_MODEL
GRID_W = 64
LORA_W = 64
GATE_LORA = 128
RMS_EPS = 1e-6
LN_EPS = 1e-5
GN_EPS = 64e-5

LANES = 128
SUBLANES = 8
TM = 256
TM_MLP = 512
COND_ROWS = 16
ADA_COLS = 1024
CHUNK = 64
N_CHUNK = TM // CHUNK
N_PAIR = RWKV_WIDTH // LANES
HALO = SUBLANES
CONV_PAD = 16
CONV_ROWS = 32
VMEM_LIMIT = 56 * 1024 * 1024

F32 = jnp.float32
BF16 = jnp.bfloat16
DECAY_SCALE = 0.6065306597126334


def _mm(a, b):
    return jnp.dot(a.astype(BF16), b.astype(BF16), preferred_element_type=F32)


def _mm_nt(a, b):
    return lax.dot_general(a.astype(BF16), b.astype(BF16), (((1,), (1,)), ((), ())),
                           preferred_element_type=F32)


def _split3(x):
    x1 = x.astype(BF16)
    r1 = x - x1.astype(F32)
    x2 = r1.astype(BF16)
    x3 = (r1 - x2.astype(F32)).astype(BF16)
    return x3, x2, x1


def _mm_split3(m, x):
    x3, x2, x1 = (jnp.dot(m, t, preferred_element_type=F32) for t in _split3(x))
    return x3 + x2 + x1


def _head_sum(x, bd):
    rows = x.shape[0]
    n_grp = x.shape[1] // LANES
    terms = []
    for p in range(n_grp):
        xp = x[:, p * LANES:(p + 1) * LANES]
        head = xp.astype(BF16)
        terms += [head, (xp - head.astype(F32)).astype(BF16)]
    prod = jnp.dot(jnp.concatenate(terms, axis=0), bd, preferred_element_type=F32)
    sums = [prod[(2 * p) * rows:(2 * p + 1) * rows] + prod[(2 * p + 1) * rows:(2 * p + 2) * rows]
            for p in range(n_grp)]
    return jnp.concatenate(sums, axis=1)


def _block_diag_ones():
    ri = lax.broadcasted_iota(jnp.int32, (LANES, LANES), 0)
    ci = lax.broadcasted_iota(jnp.int32, (LANES, LANES), 1)
    shift = HEAD_DIM.bit_length() - 1
    return jnp.where((ri >> shift) == (ci >> shift), 1.0, 0.0).astype(BF16)


def _rms(x):
    return x * lax.rsqrt(jnp.mean(x * x, axis=-1, keepdims=True) + RMS_EPS)


def _ada_kernel(c_ref, w_ref, b_ref, o_ref):
    c = c_ref[...]
    rows = c.shape[0]
    terms = jnp.concatenate(_split3(c * jax.nn.sigmoid(c)), axis=0)
    prod = jnp.dot(terms, w_ref[...].astype(BF16), preferred_element_type=F32)
    o_ref[...] = prod[0:rows] + prod[rows:2 * rows] + prod[2 * rows:] + b_ref[...]


def _ada(cond, w, b):
    n = w.shape[1]
    rows = cond.shape[0]
    tn = ADA_COLS
    return pl.pallas_call(
        _ada_kernel,
        grid=(n // tn,),
        in_specs=[pl.BlockSpec((rows, D_MODEL), lambda i: (0, 0)),
                  pl.BlockSpec((D_MODEL, tn), lambda i: (0, i)),
                  pl.BlockSpec((1, tn), lambda i: (0, i))],
        out_specs=pl.BlockSpec((rows, tn), lambda i: (0, i)),
        out_shape=jax.ShapeDtypeStruct((rows, n), F32),
        compiler_params=pltpu.CompilerParams(dimension_semantics=("arbitrary",),
                                             vmem_limit_bytes=VMEM_LIMIT),
        name="ada",
    )(cond, w, b.reshape(1, n))


_IN_SPLIT = (3 * RWKV_WIDTH, 2 * CONV_WIDTH, 2 * D_MODEL)
_LORA_SPLIT = (4 * LORA_W, GATE_LORA)


def _run(*streams):
    streams = list(streams)
    while streams:
        streams = [g for g in streams if next(g, True) is None]


def _conv_module(uu, cw_ref, cb_ref, clg_ref, clb_ref, pad_scr, seq):
    nseq = TM // seq
    zpad = jnp.zeros((nseq, CONV_PAD, CONV_WIDTH), F32)
    pad_scr[0, :, 0:CONV_PAD, :] = zpad
    pad_scr[0, :, CONV_PAD + seq:, :] = zpad
    for s in range(nseq):
        pad_scr[0, s, CONV_PAD:CONV_PAD + seq, :] = uu[s * seq:(s + 1) * seq]
    live = seq + 2 * CONV_PAD - SUBLANES
    for m in range(1, SUBLANES):
        pad_scr[m, :, 0:live, :] = pad_scr[0, :, m:m + live, :]
    first = CONV_PAD - CONV_K // 2
    groups = (CONV_ROWS // SUBLANES, SUBLANES, CONV_WIDTH)
    parts = []
    for s in range(nseq):
        for r0 in range(0, seq, CONV_ROWS):
            acc = jnp.broadcast_to(cb_ref[...], groups)
            for t in range(CONV_K):
                m = (first + t) % SUBLANES
                lo = r0 + first + t - m
                acc = acc + pad_scr[m, s, lo:lo + CONV_ROWS, :].reshape(groups) * cw_ref[t][None]
            parts.append(acc.reshape(CONV_ROWS, CONV_WIDTH))
    cv = jnp.concatenate(parts, axis=0)
    cm = cv - jnp.mean(cv, axis=-1, keepdims=True)
    cvar = jnp.mean(cm * cm, axis=-1, keepdims=True)
    un = cm * lax.rsqrt(cvar + LN_EPS) * clg_ref[...] + clb_ref[...]
    return un * jax.nn.sigmoid(un)


def _front_kernel(x_ref, mod_ref, g_ref, win_ref, wlo_ref, rkv_ref, glu_ref, gates_ref, lora_ref, lg_ref):
    mod = mod_ref[0]
    sh1 = mod[:, 0:D_MODEL]
    sc1 = mod[:, D_MODEL:2 * D_MODEL]
    xn = (_rms(x_ref[...]) * g_ref[...] * (1.0 + sc1) + sh1).astype(BF16)
    c0, c1, c2 = itertools.accumulate(_IN_SPLIT)
    dot = functools.partial(jnp.dot, preferred_element_type=F32)
    rkv_ref[...] = dot(xn, win_ref[:, 0:c0])
    glu = dot(xn, win_ref[:, c0:c1])
    glu_ref[...] = glu[:, 0:CONV_WIDTH] * jax.nn.sigmoid(glu[:, CONV_WIDTH:])
    gates_ref[...] = jax.nn.sigmoid(dot(xn, win_ref[:, c1:c2])).astype(BF16)
    lora_ref[...] = dot(xn, wlo_ref[:, 0:_LORA_SPLIT[0]])
    lg_ref[...] = dot(xn, wlo_ref[:, _LORA_SPLIT[0]:])


def _front(x, mod3, mod_row, g, w_in, w_lora):
    n = x.shape[0]
    const = lambda i: (0, 0)
    tok = lambda i: (i, 0)
    widths = (_IN_SPLIT[0], CONV_WIDTH, _IN_SPLIT[2]) + _LORA_SPLIT
    return pl.pallas_call(
        _front_kernel,
        grid=(n // TM,),
        in_specs=[pl.BlockSpec((TM, D_MODEL), tok),
                  pl.BlockSpec((1, 1, 6 * D_MODEL), lambda i: (mod_row(i), 0, 0)),
                  pl.BlockSpec((1, D_MODEL), const),
                  pl.BlockSpec(w_in.shape, const),
                  pl.BlockSpec(w_lora.shape, const)],
        out_specs=[pl.BlockSpec((TM, w), tok) for w in widths],
        out_shape=[jax.ShapeDtypeStruct((n, w), BF16 if w == _IN_SPLIT[2] else F32) for w in widths],
        compiler_params=pltpu.CompilerParams(dimension_semantics=("arbitrary",),
                                             vmem_limit_bytes=VMEM_LIMIT),
        name="front",
    )(x, mod3, g, w_in, w_lora)


def _wkv_masks(d):
    shift = CHUNK.bit_length() - 1
    ri = lax.broadcasted_iota(jnp.int32, (LANES, LANES), 0)
    ci = lax.broadcasted_iota(jnp.int32, (LANES, LANES), 1)
    same = (ri >> shift) == (ci >> shift)
    rel = (ci - ri) * (1 - 2 * d)
    levels = [((ri >> (lv + 1)) == (ci >> (lv + 1))) & (((ri >> lv) & 1) == 1 - d) & (((ci >> lv) & 1) == d)
              for lv in range(shift)]
    rc = lax.broadcasted_iota(jnp.int32, (LANES // 2, LANES), 0)
    cc = lax.broadcasted_iota(jnp.int32, (LANES // 2, LANES), 1)
    compact = [((cc >> (lv + 1)) == (rc >> lv)) & (((cc >> lv) & 1) == d) for lv in range(shift)]
    return same & (rel < 0), same & (rel <= 0), levels, compact


def _wkv_kernel(zero_ref, rkv_ref, prev_ref, next_ref, lora_ref, s0_ref, mup_ref, mun_ref, kk_ref, ka_ref,
                rk_ref, w0_ref, w2_ref, a0_ref, a2_ref,
                y_ref, bonus_ref, sout_ref,
                s_scr, r_scr, kd_scr, v_scr, aa_scr, b_scr, lw_scr, cl_scr, bn_scr,
                rhs_scr, out_scr, upd_scr, ti_scr, vs_scr, wc_scr, aab_scr, t_scr, *, d, nseq, nblk):
    g = pl.program_id(0)
    n_tile = nseq * nblk
    n_prob = N_CHUNK * N_PAIR
    j_a = jnp.minimum(g, n_tile - 1) % nblk
    j_b = jnp.maximum(g - 1, 0) % nblk
    masks = _wkv_masks(d)
    row0 = zero_ref[0]
    shift = CHUNK.bit_length() - 1
    ri = lax.broadcasted_iota(jnp.int32, (LANES, LANES), 0)
    ci = lax.broadcasted_iota(jnp.int32, (LANES, LANES), 1)
    eye = (ri == ci).astype(F32)
    head0 = lax.broadcasted_iota(jnp.int32, (CHUNK, LANES), 1) < HEAD_DIM
    dot = functools.partial(jnp.dot, preferred_element_type=F32)

    def stack(t):
        return jnp.concatenate([jnp.where(head0, t, 0.0), jnp.where(head0, 0.0, t)], axis=0)

    def twice(t):
        return jnp.concatenate([t, t], axis=0)

    def chunk_rows(step, anchored=False):
        lo = (N_CHUNK - 1 - step if d else step) * CHUNK
        return pl.ds(pl.multiple_of(row0 + lo, CHUNK), CHUNK) if anchored else slice(lo, lo + CHUNK)

    def prepare(slot, j):
        blk = nblk - 1 - j if d else j
        u = rkv_ref[...]
        row = lax.broadcasted_iota(jnp.int32, (TM, 1), 0)
        prow = jnp.where(blk > 0, prev_ref[HALO - 1:HALO, :], 0.0)
        nrow = jnp.where(blk < nblk - 1, next_ref[0:1, :], 0.0)
        prev = jnp.where(row == 0, prow, pltpu.roll(u, 1, axis=0))
        nxt = jnp.where(row == TM - 1, nrow, pltpu.roll(u, TM - 1, axis=0))
        x = u + (prev - u) * mup_ref[...] + (nxt - u) * mun_ref[...]
        r = x[:, 0:RWKV_WIDTH]
        k = x[:, RWKV_WIDTH:2 * RWKV_WIDTH]
        v = x[:, 2 * RWKV_WIDTH:]
        r_scr[slot] = r
        v_scr[slot] = v
        yield
        bd = _block_diag_ones()
        kkr = k * kk_ref[...]
        kk = kkr * lax.rsqrt(jnp.maximum(_head_sum(kkr * kkr, bd), 1e-24))
        aa_scr[slot] = -kk
        yield
        lo = lora_ref[...]
        z = w0_ref[...] + _mm(jnp.tanh(lo[:, 0:2 * LORA_W]), w2_ref[...])
        lw = -DECAY_SCALE * jax.nn.sigmoid(z)
        lw_scr[slot, 0:TM, :] = lw
        a = jax.nn.sigmoid(a0_ref[...] + _mm(lo[:, 2 * LORA_W:4 * LORA_W], a2_ref[...]))
        kd = k * (1.0 + (a - 1.0) * ka_ref[...])
        kd_scr[slot] = kd
        b_scr[slot] = kk * a
        yield
        bn_scr[slot] = _head_sum(r * kd * rk_ref[...], bd) * v
        yield
        rt = lax.broadcasted_iota(jnp.int32, (TM, TM), 0)
        ct = lax.broadcasted_iota(jnp.int32, (TM, TM), 1)
        cum_m = jnp.where(((rt >> shift) == (ct >> shift)) & (((ct - rt) * (1 - 2 * d)) <= 0), 1.0, 0.0)
        cl_scr[slot] = _mm_split3(cum_m.astype(BF16), lw)
        yield

    def operands(step, slot):
        strict, incl, levels, _ = masks
        rows = chunk_rows(step, anchored=True)
        for p in range(N_PAIR):
            i = slot * n_prob + step * N_PAIR + p
            sl = slice(p * LANES, (p + 1) * LANES)
            lw = lw_scr[slot, rows, sl]
            cl = cl_scr[slot, rows, sl]
            tot = jnp.sum(lw, axis=0, keepdims=True)
            e_out = jnp.exp(-cl)
            e_end = jnp.exp(tot - cl)
            b_c = b_scr[slot, rows, sl]
            kd_c = kd_scr[slot, rows, sl]
            a_s = stack(aa_scr[slot, rows, sl] * jnp.exp(cl - lw)).astype(BF16)
            r_s = stack(r_scr[slot, rows, sl] * jnp.exp(cl)).astype(BF16)
            bk_s = jnp.concatenate([twice((b_c * e_out).astype(BF16)),
                                    twice((kd_c * e_out).astype(BF16))], axis=0)
            vs_scr[i] = stack(v_scr[slot, rows, sl]).astype(BF16)
            upd_scr[i] = jnp.concatenate([jnp.transpose(stack(b_c * e_end)),
                                          jnp.transpose(stack(kd_c * e_end))], axis=1).astype(BF16)
            wc_scr[i] = jnp.transpose(jnp.broadcast_to(jnp.exp(tot), (LANES, LANES)))
            ab = _mm_nt(jnp.concatenate([a_s, r_s], axis=0), bk_s)
            a_ab = jnp.where(strict, ab[0:LANES, 0:LANES], 0.0)
            a_ak = jnp.where(strict, ab[0:LANES, LANES:], 0.0)
            a_rb = jnp.where(incl, ab[LANES:, 0:LANES], 0.0)
            a_rk = jnp.where(incl, ab[LANES:, LANES:], 0.0)
            rhs_scr[i] = jnp.concatenate([a_s, a_ak.astype(BF16)], axis=1)
            out_scr[i] = jnp.concatenate([r_s, a_rb.astype(BF16), a_rk.astype(BF16)], axis=1)
            aab_scr[i] = a_ab
            t_scr[i] = eye + jnp.where(levels[0], a_ab, 0.0)
            yield

    def invert(steps, slot):
        _, _, levels, compact = masks
        idx = [slot * n_prob + step * N_PAIR + p for step in steps for p in range(N_PAIR)]
        for lv in range(1, len(levels)):
            s = 1 << lv
            gather = s >= SUBLANES
            pieces = [slice((2 * q + 1 - d) * s, (2 * q + 2 - d) * s) for q in range(LANES // (2 * s))]

            def later(x):
                return jnp.concatenate([x[rows] for rows in pieces], axis=0)

            def spread(x):
                zero = jnp.zeros((s, LANES), x.dtype)
                halves = [[x[q * s:(q + 1) * s], zero] if d else [zero, x[q * s:(q + 1) * s]]
                          for q in range(len(pieces))]
                return jnp.concatenate([h for pair in halves for h in pair], axis=0)

            ts = [t_scr[i] for i in idx]
            tbs = [t.astype(BF16) for t in ts]
            if gather:
                ms = [dot(jnp.where(compact[lv], later(aab_scr[i]), 0.0).astype(BF16), tb)
                      for i, tb in zip(idx, tbs)]
            else:
                ms = [dot(jnp.where(levels[lv], aab_scr[i], 0.0).astype(BF16), tb) for i, tb in zip(idx, tbs)]
            yield
            for i, t, tb, m in zip(idx, ts, tbs, ms):
                if gather:
                    z = dot(later(t).astype(BF16), spread(m.astype(BF16)))
                    for q, rows in enumerate(pieces):
                        t_scr[i, rows, :] = t[rows] + z[q * s:(q + 1) * s]
                else:
                    t_scr[i] = t + dot(tb, m.astype(BF16))
            yield
        for i in idx:
            ti_scr[i] = t_scr[i].astype(BF16)

    def scan(step, slot):
        rows = chunk_rows(step)
        base = slot * n_prob + step * N_PAIR
        pairs = range(N_PAIR)
        sts = [s_scr[p] for p in pairs]
        stb = [st.astype(BF16) for st in sts]
        vss = [vs_scr[base + p] for p in pairs]
        rhs = [dot(rhs_scr[base + p], jnp.concatenate([stb[p], vss[p]], axis=0)) for p in pairs]
        lw_scr[1 - slot, TM:, 0:LANES] = rhs[0][0:SUBLANES, :]
        yield
        pmb = []
        for p in pairs:
            rh = rhs[p].astype(BF16)
            rl = (rhs[p] - rh.astype(F32)).astype(BF16)
            pm = dot(ti_scr[base + p], jnp.concatenate([rh, rl], axis=1))
            pmb.append((pm[:, 0:LANES] + pm[:, LANES:]).astype(BF16))
        yield
        for p in pairs:
            s_scr[p] = wc_scr[base + p] * sts[p] + dot(
                upd_scr[base + p], jnp.concatenate([pmb[p], vss[p]], axis=0))
        yield
        for p in pairs:
            y_s = dot(out_scr[base + p], jnp.concatenate([stb[p], pmb[p], vss[p]], axis=0))
            y_ref[rows, p * LANES:(p + 1) * LANES] = (y_s[0:CHUNK] + y_s[CHUNK:]).astype(y_ref.dtype)
        yield

    chain = itertools.chain.from_iterable

    def stage_a(slot):
        return prepare(slot, j_a), chain(operands(s, slot) for s in range(N_CHUNK))

    def stage_b(slot, with_inverse=(), with_scan=()):
        @pl.when(j_b == 0)
        def _():
            zero = jnp.zeros((HEAD_DIM, HEAD_DIM), F32)
            for p in range(N_PAIR):
                top = jnp.concatenate([s0_ref[0, 2 * p], zero], axis=1)
                bot = jnp.concatenate([zero, s0_ref[0, 2 * p + 1]], axis=1)
                s_scr[p] = jnp.transpose(jnp.concatenate([top, bot], axis=0))

        bonus_ref[...] = bn_scr[slot].astype(bonus_ref.dtype)
        _run(invert(range(N_CHUNK), slot), *with_inverse)
        _run(chain(scan(s, slot) for s in range(N_CHUNK)), *with_scan)

        @pl.when(j_b == nblk - 1)
        def _():
            for p in range(N_PAIR):
                s_vk = jnp.transpose(s_scr[p])
                sout_ref[0, 2 * p] = s_vk[0:HEAD_DIM, 0:HEAD_DIM]
                sout_ref[0, 2 * p + 1] = s_vk[HEAD_DIM:, HEAD_DIM:]

    @pl.when(g == 0)
    def _():
        prep, ops = stage_a(0)
        _run(prep)
        _run(ops)

    for parity in range(2):
        @pl.when((g > 0) & (g < n_tile) & (g % 2 == parity))
        def _():
            prep, ops = stage_a(parity)
            stage_b(1 - parity, with_inverse=(prep,), with_scan=(ops,))

    @pl.when(g == n_tile)
    def _():
        stage_b((n_tile - 1) % 2)


def _wkv(rkv, lora, s0, mup, mun, k_k, k_a, r_k, w0, w2p, a0, a2p, *, d, nseq, nblk):
    n = rkv.shape[0]
    hb = TM // HALO
    n_tile = nseq * nblk

    def tile(t):
        s = t // nblk
        j = t - s * nblk
        return s, s * nblk + (nblk - 1 - j if d else j)

    ahead = lambda g: tile(jnp.minimum(g, n_tile - 1))
    behind = lambda g: tile(jnp.maximum(g - 1, 0))
    s0_seq = (lambda s: s) if s0.shape[0] > 1 else (lambda s: 0)
    const = lambda g: (0, 0)
    out_tile = lambda g: (behind(g)[1], 0)
    kern = functools.partial(_wkv_kernel, d=d, nseq=nseq, nblk=nblk)
    tok = pltpu.VMEM((2, TM, RWKV_WIDTH), F32)
    n_prob = 2 * N_CHUNK * N_PAIR
    return pl.pallas_call(
        kern,
        grid=(n_tile + 1,),
        in_specs=[
            pl.BlockSpec(memory_space=pltpu.SMEM),
            pl.BlockSpec((TM, 3 * RWKV_WIDTH), lambda g: (ahead(g)[1], 0)),
            pl.BlockSpec((HALO, 3 * RWKV_WIDTH), lambda g: (jnp.maximum(ahead(g)[1] * hb - 1, 0), 0)),
            pl.BlockSpec((HALO, 3 * RWKV_WIDTH),
                         lambda g: (jnp.minimum((ahead(g)[1] + 1) * hb, n // HALO - 1), 0)),
            pl.BlockSpec((TM, 4 * LORA_W), lambda g: (ahead(g)[1], 0)),
            pl.BlockSpec((1, N_HEADS, HEAD_DIM, HEAD_DIM), lambda g: (s0_seq(behind(g)[0]), 0, 0, 0)),
            pl.BlockSpec((1, 3 * RWKV_WIDTH), const),
            pl.BlockSpec((1, 3 * RWKV_WIDTH), const),
            pl.BlockSpec((1, RWKV_WIDTH), const),
            pl.BlockSpec((1, RWKV_WIDTH), const),
            pl.BlockSpec((1, RWKV_WIDTH), const),
            pl.BlockSpec((1, RWKV_WIDTH), const),
            pl.BlockSpec((2 * LORA_W, RWKV_WIDTH), const),
            pl.BlockSpec((1, RWKV_WIDTH), const),
            pl.BlockSpec((2 * LORA_W, RWKV_WIDTH), const),
        ],
        out_specs=[
            pl.BlockSpec((TM, RWKV_WIDTH), out_tile),
            pl.BlockSpec((TM, RWKV_WIDTH), out_tile),
            pl.BlockSpec((1, N_HEADS, HEAD_DIM, HEAD_DIM), lambda g: (behind(g)[0], 0, 0, 0)),
        ],
        out_shape=[jax.ShapeDtypeStruct((n, RWKV_WIDTH), BF16),
                   jax.ShapeDtypeStruct((n, RWKV_WIDTH), BF16),
                   jax.ShapeDtypeStruct((nseq, N_HEADS, HEAD_DIM, HEAD_DIM), F32)],
        scratch_shapes=[pltpu.VMEM((N_PAIR, LANES, LANES), F32),
                        tok, tok, tok, tok, tok,
                        pltpu.VMEM((2, TM + SUBLANES, RWKV_WIDTH), F32),
                        tok, tok,
                        pltpu.VMEM((n_prob, LANES, 2 * LANES), BF16),
                        pltpu.VMEM((n_prob, LANES, 3 * LANES), BF16),
                        pltpu.VMEM((n_prob, LANES, 2 * LANES), BF16),
                        pltpu.VMEM((n_prob, LANES, LANES), BF16),
                        pltpu.VMEM((n_prob, LANES, LANES), BF16),
                        pltpu.VMEM((n_prob, LANES, LANES), F32),
                        pltpu.VMEM((n_prob, LANES, LANES), F32),
                        pltpu.VMEM((n_prob, LANES, LANES), F32)],
        compiler_params=pltpu.CompilerParams(dimension_semantics=("arbitrary",),
                                             vmem_limit_bytes=VMEM_LIMIT),
        name="wkv",
    )(jnp.zeros((1,), jnp.int32), rkv, rkv, rkv, lora, s0, mup, mun, k_k, k_a, r_k, w0, w2p, a0, a2p)


def _mid_kernel(yf_ref, yb_ref, bf_ref, bb_ref, lg_ref, glu_ref, gates_ref, x_ref, mod_ref, g2w_ref, lnxg_ref,
                lnxb_ref, wor_ref, cw_ref, cb_ref, clg_ref, clb_ref, woc_ref, wo_ref,
                x1_ref, pad_scr, *, seq):
    g1 = mod_ref[0][:, 2 * D_MODEL:3 * D_MODEL]

    bd = _block_diag_ones()
    y = yf_ref[...].astype(F32) + yb_ref[...].astype(F32)
    yc = y - _head_sum(y, bd) * (1.0 / HEAD_DIM)
    var = _head_sum(yc * yc, bd) * (1.0 / HEAD_DIM)
    yn = yc * lax.rsqrt(var + GN_EPS) * lnxg_ref[...] + lnxb_ref[...]
    g = _mm(jax.nn.sigmoid(lg_ref[...]), g2w_ref[...])
    y_r = _mm((yn + bf_ref[...].astype(F32) + bb_ref[...].astype(F32)) * g, wor_ref[...])

    y_c = _mm(_conv_module(glu_ref[...], cw_ref, cb_ref, clg_ref, clb_ref, pad_scr, seq), woc_ref[...])

    gates = gates_ref[...]
    merged = gates[:, 0:D_MODEL] * y_r + gates[:, D_MODEL:] * y_c
    x1_ref[...] = x_ref[...] + g1 * _mm(merged, wo_ref[...])


def _mid(y_f, y_b, bn_f, bn_b, lg, glu, gates, x, mod3, mod_row, g2w, lnxg, lnxb, wor, cw, cb, clg, clb, woc,
         wo, *, seq):
    n = x.shape[0]
    const = lambda i: (0, 0)
    tok = lambda i: (i, 0)

    def full(a):
        return pl.BlockSpec(a.shape, lambda i: (0,) * a.ndim)

    return pl.pallas_call(
        functools.partial(_mid_kernel, seq=seq),
        grid=(n // TM,),
        in_specs=[pl.BlockSpec((TM, RWKV_WIDTH), tok)] * 4 + [
                  pl.BlockSpec((TM, GATE_LORA), tok),
                  pl.BlockSpec((TM, CONV_WIDTH), tok),
                  pl.BlockSpec((TM, 2 * D_MODEL), tok),
                  pl.BlockSpec((TM, D_MODEL), tok),
                  pl.BlockSpec((1, 1, 6 * D_MODEL), lambda i: (mod_row(i), 0, 0)),
                  full(g2w), full(lnxg), full(lnxb), full(wor), full(cw), full(cb), full(clg),
                  full(clb), full(woc), full(wo)],
        out_specs=pl.BlockSpec((TM, D_MODEL), tok),
        out_shape=jax.ShapeDtypeStruct((n, D_MODEL), F32),
        scratch_shapes=[pltpu.VMEM((SUBLANES, TM // seq, seq + 2 * CONV_PAD, CONV_WIDTH), F32)],
        compiler_params=pltpu.CompilerParams(dimension_semantics=("arbitrary",),
                                             vmem_limit_bytes=VMEM_LIMIT),
        name="mid",
    )(y_f, y_b, bn_f, bn_b, lg, glu, gates, x, mod3, g2w, lnxg, lnxb, wor, cw, cb, clg, clb, woc, wo)


def _mlp_kernel(x1_ref, mod_ref, n2g_ref, w1_ref, w2_ref, fg_ref, o_ref):
    mod = mod_ref[0]
    sh2 = mod[:, 3 * D_MODEL:4 * D_MODEL]
    sc2 = mod[:, 4 * D_MODEL:5 * D_MODEL]
    g2 = mod[:, 5 * D_MODEL:6 * D_MODEL]
    x1 = x1_ref[...]
    xn2 = _rms(x1) * n2g_ref[...] * (1.0 + sc2) + sh2
    h = jnp.maximum(_mm(xn2, w1_ref[...]), 0.0)
    x2 = x1 + g2 * _mm(h * h, w2_ref[...])
    o_ref[...] = _rms(x2) * fg_ref[...]


def _mlp(x1, mod3, mod_row, n2g, w1, w2, fg):
    n = x1.shape[0]
    const = lambda i: (0, 0)
    tok = lambda i: (i, 0)
    once = pl.Buffered(1)
    return pl.pallas_call(
        _mlp_kernel,
        grid=(n // TM_MLP,),
        in_specs=[pl.BlockSpec((TM_MLP, D_MODEL), tok),
                  pl.BlockSpec((1, 1, 6 * D_MODEL), lambda i: (mod_row(i * (TM_MLP // TM)), 0, 0)),
                  pl.BlockSpec((1, D_MODEL), const),
                  pl.BlockSpec(w1.shape, const, pipeline_mode=once),
                  pl.BlockSpec(w2.shape, const, pipeline_mode=once),
                  pl.BlockSpec((1, D_MODEL), const)],
        out_specs=pl.BlockSpec((TM_MLP, D_MODEL), tok),
        out_shape=jax.ShapeDtypeStruct((n, D_MODEL), F32),
        compiler_params=pltpu.CompilerParams(dimension_semantics=("arbitrary",),
                                             vmem_limit_bytes=VMEM_LIMIT),
        name="mlp",
    )(x1, mod3, n2g, w1, w2, fg)


def _pad_dir(w):
    return (jnp.eye(2, dtype=w.dtype)[:, :, None, None] * w[:, None]).reshape(2, 2 * w.shape[1], w.shape[2])


def kernel(x_prompt, x_sample, state_fwd, state_bwd, c, c_ctx, ada_w, ada_b, norm1_g, norm2_g, w_in,
           mu_prev, mu_next, decay_w0, decay_w1, decay_w2, iclr_a0, iclr_a1, iclr_a2, gate_g1, gate_g2,
           k_k, k_a, r_k, lnx_g, lnx_b, w_out_rwkv, conv_w, conv_b, conv_ln_g, conv_ln_b, w_out_conv,
           w_o, mlp_w1, mlp_w2, final_g):
    n_ctx, seq_ctx, _ = x_prompt.shape
    n_lat, seq_lat, _ = x_sample.shape
    depth = ada_w.shape[0]
    assert seq_ctx == TM and seq_lat % TM_MLP == 0 and GRID_W == CHUNK and c.shape[0] + 1 <= COND_ROWS
    lat_blk = seq_lat // TM
    row_ctx = lambda i: 0
    row_lat = lambda i: 1 + i // lat_blk

    xp = x_prompt.reshape(n_ctx * seq_ctx, D_MODEL)
    xs = x_sample.reshape(n_lat * seq_lat, D_MODEL)
    cond = jnp.concatenate([c_ctx[None, :], c, jnp.zeros((COND_ROWS - 1 - c.shape[0], D_MODEL), F32)], axis=0)
    zero_state = (jnp.zeros((1, N_HEADS, HEAD_DIM, HEAD_DIM), F32),) * 2
    row = lambda a: a.reshape(1, -1)
    assert depth == 1
    lay = lambda a: a.reshape(a.shape[1:])
    bf = lambda a: lay(a).astype(BF16)

    mod3 = _ada(cond, lay(ada_w), lay(ada_b)).reshape(COND_ROWS, 1, 6 * D_MODEL)
    w_lora = jnp.concatenate([decay_w1[0, 0], decay_w1[0, 1], iclr_a1[0, 0], iclr_a1[0, 1], gate_g1[0]],
                             axis=1).astype(BF16)
    conv_taps = jnp.broadcast_to(conv_w[0, :, None, :], (CONV_K, SUBLANES, CONV_WIDTH))
    front_w = (norm1_g, bf(w_in), w_lora)
    w2p = _pad_dir(lay(decay_w2)).astype(BF16)
    a2p = _pad_dir(lay(iclr_a2)).astype(BF16)
    wkv_w = [(mu_prev, mu_next, k_k, k_a, row(r_k), decay_w0[:, d], w2p[d], iclr_a0[:, d], a2p[d])
             for d in range(2)]
    mid_w = (bf(gate_g2), lnx_g, lnx_b, bf(w_out_rwkv), conv_taps, conv_b, conv_ln_g, conv_ln_b,
             bf(w_out_conv), bf(w_o))
    mlp_w = (norm2_g, bf(mlp_w1), bf(mlp_w2), row(final_g))
    s0_lat = (state_fwd[:, 0].astype(F32), state_bwd[:, 0].astype(F32))

    def layer(x, mod_row, s0, nseq, nblk, conv_seq):
        rkv, glu, gates, lora, lg = _front(x, mod3, mod_row, *front_w)
        y_f, bn_f, s_f = _wkv(rkv, lora, s0[0], *wkv_w[0], d=0, nseq=nseq, nblk=nblk)
        y_b, bn_b, s_b = _wkv(rkv, lora, s0[1], *wkv_w[1], d=1, nseq=nseq, nblk=nblk)
        x1 = _mid(y_f, y_b, bn_f, bn_b, lg, glu, gates, x, mod3, mod_row, *mid_w, seq=conv_seq)
        return _mlp(x1, mod3, mod_row, *mlp_w), (s_f, s_b)

    yp, s_ctx = layer(xp, row_ctx, zero_state, n_ctx, 1, seq_ctx)
    ys, _ = layer(xs, row_lat, s0_lat, n_lat, lat_blk, GRID_W)
    s_f, s_b = (s.astype(x_prompt.dtype)[:, None] for s in s_ctx)
    return (yp.reshape(x_prompt.shape), ys.reshape(x_sample.shape), s_f, s_b)
```

```python
import functools
import itertools

import jax
import jax.numpy as jnp
from jax import lax
from jax.experimental import pallas as pl
from jax.experimental.pallas import tpu as pltpu

D_MODEL = 1024
RWKV_WIDTH = 512
HEAD_DIM = 64
N_HEADS = RWKV_WIDTH // HEAD_DIM
CONV_WIDTH = 512
CONV_K = 31
D_FF = 4 * D_MODEL
GRID_W = 64
LORA_W = 64
GATE_LORA = 128
RMS_EPS = 1e-6
LN_EPS = 1e-5
GN_EPS = 64e-5

LANES = 128
SUBLANES = 8
TM = 256
TM_MLP = 512
COND_ROWS = 16
ADA_COLS = 1024
CHUNK = 64
N_CHUNK = TM // CHUNK
N_PAIR = RWKV_WIDTH // LANES
HALO = SUBLANES
CONV_PAD = 16
CONV_ROWS = 32
VMEM_LIMIT = 56 * 1024 * 1024

F32 = jnp.float32
BF16 = jnp.bfloat16
DECAY_SCALE = 0.6065306597126334


def _mm(a, b):
    return jnp.dot(a.astype(BF16), b.astype(BF16), preferred_element_type=F32)


def _mm_nt(a, b):
    return lax.dot_general(a.astype(BF16), b.astype(BF16), (((1,), (1,)), ((), ())),
                           preferred_element_type=F32)


def _split3(x):
    x1 = x.astype(BF16)
    r1 = x - x1.astype(F32)
    x2 = r1.astype(BF16)
    x3 = (r1 - x2.astype(F32)).astype(BF16)
    return x3, x2, x1


def _mm_split3(m, x):
    x3, x2, x1 = (jnp.dot(m, t, preferred_element_type=F32) for t in _split3(x))
    return x3 + x2 + x1


def _head_sum(x, bd):
    rows = x.shape[0]
    n_grp = x.shape[1] // LANES
    terms = []
    for p in range(n_grp):
        xp = x[:, p * LANES:(p + 1) * LANES]
        head = xp.astype(BF16)
        terms += [head, (xp - head.astype(F32)).astype(BF16)]
    prod = jnp.dot(jnp.concatenate(terms, axis=0), bd, preferred_element_type=F32)
    sums = [prod[(2 * p) * rows:(2 * p + 1) * rows] + prod[(2 * p + 1) * rows:(2 * p + 2) * rows]
            for p in range(n_grp)]
    return jnp.concatenate(sums, axis=1)


def _block_diag_ones():
    ri = lax.broadcasted_iota(jnp.int32, (LANES, LANES), 0)
    ci = lax.broadcasted_iota(jnp.int32, (LANES, LANES), 1)
    shift = HEAD_DIM.bit_length() - 1
    return jnp.where((ri >> shift) == (ci >> shift), 1.0, 0.0).astype(BF16)


def _rms(x):
    return x * lax.rsqrt(jnp.mean(x * x, axis=-1, keepdims=True) + RMS_EPS)


def _ada_kernel(c_ref, w_ref, b_ref, o_ref):
    c = c_ref[...]
    rows = c.shape[0]
    terms = jnp.concatenate(_split3(c * jax.nn.sigmoid(c)), axis=0)
    prod = jnp.dot(terms, w_ref[...].astype(BF16), preferred_element_type=F32)
    o_ref[...] = prod[0:rows] + prod[rows:2 * rows] + prod[2 * rows:] + b_ref[...]


def _ada(cond, w, b):
    n = w.shape[1]
    rows = cond.shape[0]
    tn = ADA_COLS
    return pl.pallas_call(
        _ada_kernel,
        grid=(n // tn,),
        in_specs=[pl.BlockSpec((rows, D_MODEL), lambda i: (0, 0)),
                  pl.BlockSpec((D_MODEL, tn), lambda i: (0, i)),
                  pl.BlockSpec((1, tn), lambda i: (0, i))],
        out_specs=pl.BlockSpec((rows, tn), lambda i: (0, i)),
        out_shape=jax.ShapeDtypeStruct((rows, n), F32),
        compiler_params=pltpu.CompilerParams(dimension_semantics=("arbitrary",),
                                             vmem_limit_bytes=VMEM_LIMIT),
        name="ada",
    )(cond, w, b.reshape(1, n))


_IN_SPLIT = (3 * RWKV_WIDTH, 2 * CONV_WIDTH, 2 * D_MODEL)
_LORA_SPLIT = (4 * LORA_W, GATE_LORA)


def _run(*streams):
    streams = list(streams)
    while streams:
        streams = [g for g in streams if next(g, True) is None]


def _conv_module(uu, cw_ref, cb_ref, clg_ref, clb_ref, pad_scr, seq):
    nseq = TM // seq
    zpad = jnp.zeros((nseq, CONV_PAD, CONV_WIDTH), F32)
    pad_scr[0, :, 0:CONV_PAD, :] = zpad
    pad_scr[0, :, CONV_PAD + seq:, :] = zpad
    for s in range(nseq):
        pad_scr[0, s, CONV_PAD:CONV_PAD + seq, :] = uu[s * seq:(s + 1) * seq]
    live = seq + 2 * CONV_PAD - SUBLANES
    for m in range(1, SUBLANES):
        pad_scr[m, :, 0:live, :] = pad_scr[0, :, m:m + live, :]
    first = CONV_PAD - CONV_K // 2
    groups = (CONV_ROWS // SUBLANES, SUBLANES, CONV_WIDTH)
    parts = []
    for s in range(nseq):
        for r0 in range(0, seq, CONV_ROWS):
            acc = jnp.broadcast_to(cb_ref[...], groups)
            for t in range(CONV_K):
                m = (first + t) % SUBLANES
                lo = r0 + first + t - m
                acc = acc + pad_scr[m, s, lo:lo + CONV_ROWS, :].reshape(groups) * cw_ref[t][None]
            parts.append(acc.reshape(CONV_ROWS, CONV_WIDTH))
    cv = jnp.concatenate(parts, axis=0)
    cm = cv - jnp.mean(cv, axis=-1, keepdims=True)
    cvar = jnp.mean(cm * cm, axis=-1, keepdims=True)
    un = cm * lax.rsqrt(cvar + LN_EPS) * clg_ref[...] + clb_ref[...]
    return un * jax.nn.sigmoid(un)


def _front_kernel(x_ref, mod_ref, g_ref, win_ref, wlo_ref, rkv_ref, glu_ref, gates_ref, lora_ref, lg_ref):
    mod = mod_ref[0]
    sh1 = mod[:, 0:D_MODEL]
    sc1 = mod[:, D_MODEL:2 * D_MODEL]
    xn = (_rms(x_ref[...]) * g_ref[...] * (1.0 + sc1) + sh1).astype(BF16)
    c0, c1, c2 = itertools.accumulate(_IN_SPLIT)
    dot = functools.partial(jnp.dot, preferred_element_type=F32)
    rkv_ref[...] = dot(xn, win_ref[:, 0:c0])
    glu = dot(xn, win_ref[:, c0:c1])
    glu_ref[...] = glu[:, 0:CONV_WIDTH] * jax.nn.sigmoid(glu[:, CONV_WIDTH:])
    gates_ref[...] = jax.nn.sigmoid(dot(xn, win_ref[:, c1:c2])).astype(BF16)
    lora_ref[...] = dot(xn, wlo_ref[:, 0:_LORA_SPLIT[0]])
    lg_ref[...] = dot(xn, wlo_ref[:, _LORA_SPLIT[0]:])


def _front(x, mod3, mod_row, g, w_in, w_lora):
    n = x.shape[0]
    const = lambda i: (0, 0)
    tok = lambda i: (i, 0)
    widths = (_IN_SPLIT[0], CONV_WIDTH, _IN_SPLIT[2]) + _LORA_SPLIT
    return pl.pallas_call(
        _front_kernel,
        grid=(n // TM,),
        in_specs=[pl.BlockSpec((TM, D_MODEL), tok),
                  pl.BlockSpec((1, 1, 6 * D_MODEL), lambda i: (mod_row(i), 0, 0)),
                  pl.BlockSpec((1, D_MODEL), const),
                  pl.BlockSpec(w_in.shape, const),
                  pl.BlockSpec(w_lora.shape, const)],
        out_specs=[pl.BlockSpec((TM, w), tok) for w in widths],
        out_shape=[jax.ShapeDtypeStruct((n, w), BF16 if w == _IN_SPLIT[2] else F32) for w in widths],
        compiler_params=pltpu.CompilerParams(dimension_semantics=("arbitrary",),
                                             vmem_limit_bytes=VMEM_LIMIT),
        name="front",
    )(x, mod3, g, w_in, w_lora)


def _wkv_masks(d):
    shift = CHUNK.bit_length() - 1
    ri = lax.broadcasted_iota(jnp.int32, (LANES, LANES), 0)
    ci = lax.broadcasted_iota(jnp.int32, (LANES, LANES), 1)
    same = (ri >> shift) == (ci >> shift)
    rel = (ci - ri) * (1 - 2 * d)
    levels = [((ri >> (lv + 1)) == (ci >> (lv + 1))) & (((ri >> lv) & 1) == 1 - d) & (((ci >> lv) & 1) == d)
              for lv in range(shift)]
    rc = lax.broadcasted_iota(jnp.int32, (LANES // 2, LANES), 0)
    cc = lax.broadcasted_iota(jnp.int32, (LANES // 2, LANES), 1)
    compact = [((cc >> (lv + 1)) == (rc >> lv)) & (((cc >> lv) & 1) == d) for lv in range(shift)]
    return same & (rel < 0), same & (rel <= 0), levels, compact


def _wkv_kernel(zero_ref, rkv_ref, prev_ref, next_ref, lora_ref, s0_ref, mup_ref, mun_ref, kk_ref, ka_ref,
                rk_ref, w0_ref, w2_ref, a0_ref, a2_ref,
                y_ref, bonus_ref, sout_ref,
                s_scr, r_scr, kd_scr, v_scr, aa_scr, b_scr, lw_scr, cl_scr, bn_scr,
                rhs_scr, out_scr, upd_scr, ti_scr, vs_scr, wc_scr, aab_scr, t_scr, *, d, nseq, nblk):
    g = pl.program_id(0)
    n_tile = nseq * nblk
    n_prob = N_CHUNK * N_PAIR
    j_a = jnp.minimum(g, n_tile - 1) % nblk
    j_b = jnp.maximum(g - 1, 0) % nblk
    masks = _wkv_masks(d)
    row0 = zero_ref[0]
    shift = CHUNK.bit_length() - 1
    ri = lax.broadcasted_iota(jnp.int32, (LANES, LANES), 0)
    ci = lax.broadcasted_iota(jnp.int32, (LANES, LANES), 1)
    eye = (ri == ci).astype(F32)
    head0 = lax.broadcasted_iota(jnp.int32, (CHUNK, LANES), 1) < HEAD_DIM
    dot = functools.partial(jnp.dot, preferred_element_type=F32)

    def stack(t):
        return jnp.concatenate([jnp.where(head0, t, 0.0), jnp.where(head0, 0.0, t)], axis=0)

    def twice(t):
        return jnp.concatenate([t, t], axis=0)

    def chunk_rows(step, anchored=False):
        lo = (N_CHUNK - 1 - step if d else step) * CHUNK
        return pl.ds(pl.multiple_of(row0 + lo, CHUNK), CHUNK) if anchored else slice(lo, lo + CHUNK)

    def prepare(slot, j):
        blk = nblk - 1 - j if d else j
        u = rkv_ref[...]
        row = lax.broadcasted_iota(jnp.int32, (TM, 1), 0)
        prow = jnp.where(blk > 0, prev_ref[HALO - 1:HALO, :], 0.0)
        nrow = jnp.where(blk < nblk - 1, next_ref[0:1, :], 0.0)
        prev = jnp.where(row == 0, prow, pltpu.roll(u, 1, axis=0))
        nxt = jnp.where(row == TM - 1, nrow, pltpu.roll(u, TM - 1, axis=0))
        x = u + (prev - u) * mup_ref[...] + (nxt - u) * mun_ref[...]
        r = x[:, 0:RWKV_WIDTH]
        k = x[:, RWKV_WIDTH:2 * RWKV_WIDTH]
        v = x[:, 2 * RWKV_WIDTH:]
        r_scr[slot] = r
        v_scr[slot] = v
        yield
        bd = _block_diag_ones()
        kkr = k * kk_ref[...]
        kk = kkr * lax.rsqrt(jnp.maximum(_head_sum(kkr * kkr, bd), 1e-24))
        aa_scr[slot] = -kk
        yield
        lo = lora_ref[...]
        z = w0_ref[...] + _mm(jnp.tanh(lo[:, 0:2 * LORA_W]), w2_ref[...])
        lw = -DECAY_SCALE * jax.nn.sigmoid(z)
        lw_scr[slot, 0:TM, :] = lw
        a = jax.nn.sigmoid(a0_ref[...] + _mm(lo[:, 2 * LORA_W:4 * LORA_W], a2_ref[...]))
        kd = k * (1.0 + (a - 1.0) * ka_ref[...])
        kd_scr[slot] = kd
        b_scr[slot] = kk * a
        yield
        bn_scr[slot] = _head_sum(r * kd * rk_ref[...], bd) * v
        yield
        rt = lax.broadcasted_iota(jnp.int32, (TM, TM), 0)
        ct = lax.broadcasted_iota(jnp.int32, (TM, TM), 1)
        cum_m = jnp.where(((rt >> shift) == (ct >> shift)) & (((ct - rt) * (1 - 2 * d)) <= 0), 1.0, 0.0)
        cl_scr[slot] = _mm_split3(cum_m.astype(BF16), lw)
        yield

    def operands(step, slot):
        strict, incl, levels, _ = masks
        rows = chunk_rows(step, anchored=True)
        for p in range(N_PAIR):
            i = slot * n_prob + step * N_PAIR + p
            sl = slice(p * LANES, (p + 1) * LANES)
            lw = lw_scr[slot, rows, sl]
            cl = cl_scr[slot, rows, sl]
            tot = jnp.sum(lw, axis=0, keepdims=True)
            e_out = jnp.exp(-cl)
            e_end = jnp.exp(tot - cl)
            b_c = b_scr[slot, rows, sl]
            kd_c = kd_scr[slot, rows, sl]
            a_s = stack(aa_scr[slot, rows, sl] * jnp.exp(cl - lw)).astype(BF16)
            r_s = stack(r_scr[slot, rows, sl] * jnp.exp(cl)).astype(BF16)
            bk_s = jnp.concatenate([twice((b_c * e_out).astype(BF16)),
                                    twice((kd_c * e_out).astype(BF16))], axis=0)
            vs_scr[i] = stack(v_scr[slot, rows, sl]).astype(BF16)
            upd_scr[i] = jnp.concatenate([jnp.transpose(stack(b_c * e_end)),
                                          jnp.transpose(stack(kd_c * e_end))], axis=1).astype(BF16)
            wc_scr[i] = jnp.transpose(jnp.broadcast_to(jnp.exp(tot), (LANES, LANES)))
            ab = _mm_nt(jnp.concatenate([a_s, r_s], axis=0), bk_s)
            a_ab = jnp.where(strict, ab[0:LANES, 0:LANES], 0.0)
            a_ak = jnp.where(strict, ab[0:LANES, LANES:], 0.0)
            a_rb = jnp.where(incl, ab[LANES:, 0:LANES], 0.0)
            a_rk = jnp.where(incl, ab[LANES:, LANES:], 0.0)
            rhs_scr[i] = jnp.concatenate([a_s, a_ak.astype(BF16)], axis=1)
            out_scr[i] = jnp.concatenate([r_s, a_rb.astype(BF16), a_rk.astype(BF16)], axis=1)
            aab_scr[i] = a_ab
            t_scr[i] = eye + jnp.where(levels[0], a_ab, 0.0)
            yield

    def invert(steps, slot):
        _, _, levels, compact = masks
        idx = [slot * n_prob + step * N_PAIR + p for step in steps for p in range(N_PAIR)]
        for lv in range(1, len(levels)):
            s = 1 << lv
            gather = s >= SUBLANES
            pieces = [slice((2 * q + 1 - d) * s, (2 * q + 2 - d) * s) for q in range(LANES // (2 * s))]

            def later(x):
                return jnp.concatenate([x[rows] for rows in pieces], axis=0)

            def spread(x):
                zero = jnp.zeros((s, LANES), x.dtype)
                halves = [[x[q * s:(q + 1) * s], zero] if d else [zero, x[q * s:(q + 1) * s]]
                          for q in range(len(pieces))]
                return jnp.concatenate([h for pair in halves for h in pair], axis=0)

            ts = [t_scr[i] for i in idx]
            tbs = [t.astype(BF16) for t in ts]
            if gather:
                ms = [dot(jnp.where(compact[lv], later(aab_scr[i]), 0.0).astype(BF16), tb)
                      for i, tb in zip(idx, tbs)]
            else:
                ms = [dot(jnp.where(levels[lv], aab_scr[i], 0.0).astype(BF16), tb) for i, tb in zip(idx, tbs)]
            yield
            for i, t, tb, m in zip(idx, ts, tbs, ms):
                if gather:
                    z = dot(later(t).astype(BF16), spread(m.astype(BF16)))
                    for q, rows in enumerate(pieces):
                        t_scr[i, rows, :] = t[rows] + z[q * s:(q + 1) * s]
                else:
                    t_scr[i] = t + dot(tb, m.astype(BF16))
            yield
        for i in idx:
            ti_scr[i] = t_scr[i].astype(BF16)

    def scan(step, slot):
        rows = chunk_rows(step)
        base = slot * n_prob + step * N_PAIR
        pairs = range(N_PAIR)
        sts = [s_scr[p] for p in pairs]
        stb = [st.astype(BF16) for st in sts]
        vss = [vs_scr[base + p] for p in pairs]
        rhs = [dot(rhs_scr[base + p], jnp.concatenate([stb[p], vss[p]], axis=0)) for p in pairs]
        lw_scr[1 - slot, TM:, 0:LANES] = rhs[0][0:SUBLANES, :]
        yield
        pmb = []
        for p in pairs:
            rh = rhs[p].astype(BF16)
            rl = (rhs[p] - rh.astype(F32)).astype(BF16)
            pm = dot(ti_scr[base + p], jnp.concatenate([rh, rl], axis=1))
            pmb.append((pm[:, 0:LANES] + pm[:, LANES:]).astype(BF16))
        yield
        for p in pairs:
            s_scr[p] = wc_scr[base + p] * sts[p] + dot(
                upd_scr[base + p], jnp.concatenate([pmb[p], vss[p]], axis=0))
        yield
        for p in pairs:
            y_s = dot(out_scr[base + p], jnp.concatenate([stb[p], pmb[p], vss[p]], axis=0))
            y_ref[rows, p * LANES:(p + 1) * LANES] = (y_s[0:CHUNK] + y_s[CHUNK:]).astype(y_ref.dtype)
        yield

    chain = itertools.chain.from_iterable

    def stage_a(slot):
        return prepare(slot, j_a), chain(operands(s, slot) for s in range(N_CHUNK))

    def stage_b(slot, with_inverse=(), with_scan=()):
        @pl.when(j_b == 0)
        def _():
            zero = jnp.zeros((HEAD_DIM, HEAD_DIM), F32)
            for p in range(N_PAIR):
                top = jnp.concatenate([s0_ref[0, 2 * p], zero], axis=1)
                bot = jnp.concatenate([zero, s0_ref[0, 2 * p + 1]], axis=1)
                s_scr[p] = jnp.transpose(jnp.concatenate([top, bot], axis=0))

        bonus_ref[...] = bn_scr[slot].astype(bonus_ref.dtype)
        _run(invert(range(N_CHUNK), slot), *with_inverse)
        _run(chain(scan(s, slot) for s in range(N_CHUNK)), *with_scan)

        @pl.when(j_b == nblk - 1)
        def _():
            for p in range(N_PAIR):
                s_vk = jnp.transpose(s_scr[p])
                sout_ref[0, 2 * p] = s_vk[0:HEAD_DIM, 0:HEAD_DIM]
                sout_ref[0, 2 * p + 1] = s_vk[HEAD_DIM:, HEAD_DIM:]

    @pl.when(g == 0)
    def _():
        prep, ops = stage_a(0)
        _run(prep)
        _run(ops)

    for parity in range(2):
        @pl.when((g > 0) & (g < n_tile) & (g % 2 == parity))
        def _():
            prep, ops = stage_a(parity)
            stage_b(1 - parity, with_inverse=(prep,), with_scan=(ops,))

    @pl.when(g == n_tile)
    def _():
        stage_b((n_tile - 1) % 2)


def _wkv(rkv, lora, s0, mup, mun, k_k, k_a, r_k, w0, w2p, a0, a2p, *, d, nseq, nblk):
    n = rkv.shape[0]
    hb = TM // HALO
    n_tile = nseq * nblk

    def tile(t):
        s = t // nblk
        j = t - s * nblk
        return s, s * nblk + (nblk - 1 - j if d else j)

    ahead = lambda g: tile(jnp.minimum(g, n_tile - 1))
    behind = lambda g: tile(jnp.maximum(g - 1, 0))
    s0_seq = (lambda s: s) if s0.shape[0] > 1 else (lambda s: 0)
    const = lambda g: (0, 0)
    out_tile = lambda g: (behind(g)[1], 0)
    kern = functools.partial(_wkv_kernel, d=d, nseq=nseq, nblk=nblk)
    tok = pltpu.VMEM((2, TM, RWKV_WIDTH), F32)
    n_prob = 2 * N_CHUNK * N_PAIR
    return pl.pallas_call(
        kern,
        grid=(n_tile + 1,),
        in_specs=[
            pl.BlockSpec(memory_space=pltpu.SMEM),
            pl.BlockSpec((TM, 3 * RWKV_WIDTH), lambda g: (ahead(g)[1], 0)),
            pl.BlockSpec((HALO, 3 * RWKV_WIDTH), lambda g: (jnp.maximum(ahead(g)[1] * hb - 1, 0), 0)),
            pl.BlockSpec((HALO, 3 * RWKV_WIDTH),
                         lambda g: (jnp.minimum((ahead(g)[1] + 1) * hb, n // HALO - 1), 0)),
            pl.BlockSpec((TM, 4 * LORA_W), lambda g: (ahead(g)[1], 0)),
            pl.BlockSpec((1, N_HEADS, HEAD_DIM, HEAD_DIM), lambda g: (s0_seq(behind(g)[0]), 0, 0, 0)),
            pl.BlockSpec((1, 3 * RWKV_WIDTH), const),
            pl.BlockSpec((1, 3 * RWKV_WIDTH), const),
            pl.BlockSpec((1, RWKV_WIDTH), const),
            pl.BlockSpec((1, RWKV_WIDTH), const),
            pl.BlockSpec((1, RWKV_WIDTH), const),
            pl.BlockSpec((1, RWKV_WIDTH), const),
            pl.BlockSpec((2 * LORA_W, RWKV_WIDTH), const),
            pl.BlockSpec((1, RWKV_WIDTH), const),
            pl.BlockSpec((2 * LORA_W, RWKV_WIDTH), const),
        ],
        out_specs=[
            pl.BlockSpec((TM, RWKV_WIDTH), out_tile),
            pl.BlockSpec((TM, RWKV_WIDTH), out_tile),
            pl.BlockSpec((1, N_HEADS, HEAD_DIM, HEAD_DIM), lambda g: (behind(g)[0], 0, 0, 0)),
        ],
        out_shape=[jax.ShapeDtypeStruct((n, RWKV_WIDTH), BF16),
                   jax.ShapeDtypeStruct((n, RWKV_WIDTH), BF16),
                   jax.ShapeDtypeStruct((nseq, N_HEADS, HEAD_DIM, HEAD_DIM), F32)],
        scratch_shapes=[pltpu.VMEM((N_PAIR, LANES, LANES), F32),
                        tok, tok, tok, tok, tok,
                        pltpu.VMEM((2, TM + SUBLANES, RWKV_WIDTH), F32),
                        tok, tok,
                        pltpu.VMEM((n_prob, LANES, 2 * LANES), BF16),
                        pltpu.VMEM((n_prob, LANES, 3 * LANES), BF16),
                        pltpu.VMEM((n_prob, LANES, 2 * LANES), BF16),
                        pltpu.VMEM((n_prob, LANES, LANES), BF16),
                        pltpu.VMEM((n_prob, LANES, LANES), BF16),
                        pltpu.VMEM((n_prob, LANES, LANES), F32),
                        pltpu.VMEM((n_prob, LANES, LANES), F32),
                        pltpu.VMEM((n_prob, LANES, LANES), F32)],
        compiler_params=pltpu.CompilerParams(dimension_semantics=("arbitrary",),
                                             vmem_limit_bytes=VMEM_LIMIT),
        name="wkv",
    )(jnp.zeros((1,), jnp.int32), rkv, rkv, rkv, lora, s0, mup, mun, k_k, k_a, r_k, w0, w2p, a0, a2p)


def _mid_kernel(yf_ref, yb_ref, bf_ref, bb_ref, lg_ref, glu_ref, gates_ref, x_ref, mod_ref, g2w_ref, lnxg_ref,
                lnxb_ref, wor_ref, cw_ref, cb_ref, clg_ref, clb_ref, woc_ref, wo_ref,
                x1_ref, pad_scr, *, seq):
    g1 = mod_ref[0][:, 2 * D_MODEL:3 * D_MODEL]

    bd = _block_diag_ones()
    y = yf_ref[...].astype(F32) + yb_ref[...].astype(F32)
    yc = y - _head_sum(y, bd) * (1.0 / HEAD_DIM)
    var = _head_sum(yc * yc, bd) * (1.0 / HEAD_DIM)
    yn = yc * lax.rsqrt(var + GN_EPS) * lnxg_ref[...] + lnxb_ref[...]
    g = _mm(jax.nn.sigmoid(lg_ref[...]), g2w_ref[...])
    y_r = _mm((yn + bf_ref[...].astype(F32) + bb_ref[...].astype(F32)) * g, wor_ref[...])

    y_c = _mm(_conv_module(glu_ref[...], cw_ref, cb_ref, clg_ref, clb_ref, pad_scr, seq), woc_ref[...])

    gates = gates_ref[...]
    merged = gates[:, 0:D_MODEL] * y_r + gates[:, D_MODEL:] * y_c
    x1_ref[...] = x_ref[...] + g1 * _mm(merged, wo_ref[...])


def _mid(y_f, y_b, bn_f, bn_b, lg, glu, gates, x, mod3, mod_row, g2w, lnxg, lnxb, wor, cw, cb, clg, clb, woc,
         wo, *, seq):
    n = x.shape[0]
    const = lambda i: (0, 0)
    tok = lambda i: (i, 0)

    def full(a):
        return pl.BlockSpec(a.shape, lambda i: (0,) * a.ndim)

    return pl.pallas_call(
        functools.partial(_mid_kernel, seq=seq),
        grid=(n // TM,),
        in_specs=[pl.BlockSpec((TM, RWKV_WIDTH), tok)] * 4 + [
                  pl.BlockSpec((TM, GATE_LORA), tok),
                  pl.BlockSpec((TM, CONV_WIDTH), tok),
                  pl.BlockSpec((TM, 2 * D_MODEL), tok),
                  pl.BlockSpec((TM, D_MODEL), tok),
                  pl.BlockSpec((1, 1, 6 * D_MODEL), lambda i: (mod_row(i), 0, 0)),
                  full(g2w), full(lnxg), full(lnxb), full(wor), full(cw), full(cb), full(clg),
                  full(clb), full(woc), full(wo)],
        out_specs=pl.BlockSpec((TM, D_MODEL), tok),
        out_shape=jax.ShapeDtypeStruct((n, D_MODEL), F32),
        scratch_shapes=[pltpu.VMEM((SUBLANES, TM // seq, seq + 2 * CONV_PAD, CONV_WIDTH), F32)],
        compiler_params=pltpu.CompilerParams(dimension_semantics=("arbitrary",),
                                             vmem_limit_bytes=VMEM_LIMIT),
        name="mid",
    )(y_f, y_b, bn_f, bn_b, lg, glu, gates, x, mod3, g2w, lnxg, lnxb, wor, cw, cb, clg, clb, woc, wo)


def _mlp_kernel(x1_ref, mod_ref, n2g_ref, w1_ref, w2_ref, fg_ref, o_ref):
    mod = mod_ref[0]
    sh2 = mod[:, 3 * D_MODEL:4 * D_MODEL]
    sc2 = mod[:, 4 * D_MODEL:5 * D_MODEL]
    g2 = mod[:, 5 * D_MODEL:6 * D_MODEL]
    x1 = x1_ref[...]
    xn2 = _rms(x1) * n2g_ref[...] * (1.0 + sc2) + sh2
    h = jnp.maximum(_mm(xn2, w1_ref[...]), 0.0)
    x2 = x1 + g2 * _mm(h * h, w2_ref[...])
    o_ref[...] = _rms(x2) * fg_ref[...]


def _mlp(x1, mod3, mod_row, n2g, w1, w2, fg):
    n = x1.shape[0]
    const = lambda i: (0, 0)
    tok = lambda i: (i, 0)
    once = pl.Buffered(1)
    return pl.pallas_call(
        _mlp_kernel,
        grid=(n // TM_MLP,),
        in_specs=[pl.BlockSpec((TM_MLP, D_MODEL), tok),
                  pl.BlockSpec((1, 1, 6 * D_MODEL), lambda i: (mod_row(i * (TM_MLP // TM)), 0, 0)),
                  pl.BlockSpec((1, D_MODEL), const),
                  pl.BlockSpec(w1.shape, const, pipeline_mode=once),
                  pl.BlockSpec(w2.shape, const, pipeline_mode=once),
                  pl.BlockSpec((1, D_MODEL), const)],
        out_specs=pl.BlockSpec((TM_MLP, D_MODEL), tok),
        out_shape=jax.ShapeDtypeStruct((n, D_MODEL), F32),
        compiler_params=pltpu.CompilerParams(dimension_semantics=("arbitrary",),
                                             vmem_limit_bytes=VMEM_LIMIT,
                                             allow_input_fusion=[False, False, False, True, True, False]),
        name="mlp",
    )(x1, mod3, n2g, w1, w2, fg)


def _pad_dir(w):
    return (jnp.eye(2, dtype=w.dtype)[:, :, None, None] * w[:, None]).reshape(2, 2 * w.shape[1], w.shape[2])


def kernel(x_prompt, x_sample, state_fwd, state_bwd, c, c_ctx, ada_w, ada_b, norm1_g, norm2_g, w_in,
           mu_prev, mu_next, decay_w0, decay_w1, decay_w2, iclr_a0, iclr_a1, iclr_a2, gate_g1, gate_g2,
           k_k, k_a, r_k, lnx_g, lnx_b, w_out_rwkv, conv_w, conv_b, conv_ln_g, conv_ln_b, w_out_conv,
           w_o, mlp_w1, mlp_w2, final_g):
    n_ctx, seq_ctx, _ = x_prompt.shape
    n_lat, seq_lat, _ = x_sample.shape
    depth = ada_w.shape[0]
    assert seq_ctx == TM and seq_lat % TM_MLP == 0 and GRID_W == CHUNK and c.shape[0] + 1 <= COND_ROWS
    lat_blk = seq_lat // TM
    row_ctx = lambda i: 0
    row_lat = lambda i: 1 + i // lat_blk

    xp = x_prompt.reshape(n_ctx * seq_ctx, D_MODEL)
    xs = x_sample.reshape(n_lat * seq_lat, D_MODEL)
    cond = jnp.concatenate([c_ctx[None, :], c, jnp.zeros((COND_ROWS - 1 - c.shape[0], D_MODEL), F32)], axis=0)
    zero_state = (jnp.zeros((1, N_HEADS, HEAD_DIM, HEAD_DIM), F32),) * 2
    row = lambda a: a.reshape(1, -1)
    assert depth == 1
    lay = lambda a: a.reshape(a.shape[1:])
    bf = lambda a: lay(a).astype(BF16)

    mod3 = _ada(cond, lay(ada_w), lay(ada_b)).reshape(COND_ROWS, 1, 6 * D_MODEL)
    w_lora = jnp.concatenate([decay_w1[0, 0], decay_w1[0, 1], iclr_a1[0, 0], iclr_a1[0, 1], gate_g1[0]],
                             axis=1).astype(BF16)
    conv_taps = jnp.broadcast_to(conv_w[0, :, None, :], (CONV_K, SUBLANES, CONV_WIDTH))
    front_w = (norm1_g, bf(w_in), w_lora)
    w2p = _pad_dir(lay(decay_w2)).astype(BF16)
    a2p = _pad_dir(lay(iclr_a2)).astype(BF16)
    wkv_w = [(mu_prev, mu_next, k_k, k_a, row(r_k), decay_w0[:, d], w2p[d], iclr_a0[:, d], a2p[d])
             for d in range(2)]
    mid_w = (bf(gate_g2), lnx_g, lnx_b, bf(w_out_rwkv), conv_taps, conv_b, conv_ln_g, conv_ln_b,
             bf(w_out_conv), bf(w_o))
    mlp_w = (norm2_g, bf(mlp_w1), bf(mlp_w2), row(final_g))
    s0_lat = (state_fwd[:, 0].astype(F32), state_bwd[:, 0].astype(F32))

    def layer(x, mod_row, s0, nseq, nblk, conv_seq):
        rkv, glu, gates, lora, lg = _front(x, mod3, mod_row, *front_w)
        y_f, bn_f, s_f = _wkv(rkv, lora, s0[0], *wkv_w[0], d=0, nseq=nseq, nblk=nblk)
        y_b, bn_b, s_b = _wkv(rkv, lora, s0[1], *wkv_w[1], d=1, nseq=nseq, nblk=nblk)
        x1 = _mid(y_f, y_b, bn_f, bn_b, lg, glu, gates, x, mod3, mod_row, *mid_w, seq=conv_seq)
        return _mlp(x1, mod3, mod_row, *mlp_w), (s_f, s_b)

    yp, s_ctx = layer(xp, row_ctx, zero_state, n_ctx, 1, seq_ctx)
    ys, _ = layer(xs, row_lat, s0_lat, n_lat, lat_blk, GRID_W)
    s_f, s_b = (s.astype(x_prompt.dtype)[:, None] for s in s_ctx)
    return (yp.reshape(x_prompt.shape), ys.reshape(x_sample.shape), s_f, s_b)
```
